```python
import math, functools
import jax, jax.numpy as jnp
from jax import lax
import numpy as np

D_MODEL = 1024
BATCH = 4
SEQ = 4096
DEPTH = 2
DEC_BATCH = 32
DEC_SEQ = 16
PAST_LEN = 2048

CHUNK = 64
Q_BLOCK = 128
N_EVEN = (DEPTH + 1) // 2
N_ODD = DEPTH // 2
ALPHA = (2 * DEPTH) ** 0.25
BETA = (8 * DEPTH) ** -0.25
EPS = 1e-5
NEG_INF = -1e30
MLA_HEADS = 8
Q_LORA = 256
KV_LORA = 128
MLA_NOPE_DIM = 64
MLA_ROPE_DIM = 32
MLA_QK_DIM = MLA_NOPE_DIM + MLA_ROPE_DIM
MLA_V_DIM = 64
MLA_ROPE_THETA = 10000.0
MLA_W = MLA_HEADS * MLA_V_DIM
DIFF_HEADS = 4
DIFF_HD = 64
DIFF_ROT = DIFF_HD // 4
ROPE_THETA = 500000.0
DIFF_W = DIFF_HEADS * 2 * DIFF_HD
SB_HEADS = 16
SB_HD = 64
SB_W = SB_HEADS * SB_HD
EVEN_SECTIONS = (Q_LORA, KV_LORA, MLA_ROPE_DIM, MLA_W, DIFF_W, DIFF_W, DIFF_W, DIFF_W)
EVEN_IN = Q_LORA + KV_LORA + MLA_ROPE_DIM + MLA_W + 4 * DIFF_W
EVEN_OUT = MLA_W + DIFF_W
DV_START = Q_LORA + KV_LORA + MLA_ROPE_DIM + MLA_W + 2 * DIFF_W

kernel_name = 'hybrid_stream_mla_diff_stickbreak_step'


def _layernorm(x, g, b):
    xf = x.astype(jnp.float32)
    mu = jnp.mean(xf, axis=-1, keepdims=True)
    xc = xf - mu
    var = jnp.mean(xc * xc, axis=-1, keepdims=True)
    return (xc * lax.rsqrt(var + EPS) * g.astype(jnp.float32) + b.astype(jnp.float32)).astype(x.dtype)


def _rmsnorm(x, g):
    xf = x.astype(jnp.float32)
    ms = jnp.mean(xf * xf, axis=-1, keepdims=True)
    return (xf * lax.rsqrt(ms + EPS) * g.astype(jnp.float32)).astype(x.dtype)


def _rope(x, pos, rot_dim, theta):
    half = rot_dim // 2
    inv = jnp.exp(-math.log(theta) * jnp.arange(half, dtype=jnp.float32) * (2.0 / rot_dim))
    ang = pos.astype(jnp.float32)[:, None] * inv[None, :]
    shape = (pos.shape[0],) + (1,) * (x.ndim - 3) + (half,)
    cos = jnp.cos(ang).reshape(shape).astype(x.dtype)
    sin = jnp.sin(ang).reshape(shape).astype(x.dtype)
    x1, x2, rest = x[..., :half], x[..., half:rot_dim], x[..., rot_dim:]
    return jnp.concatenate([x1 * cos - x2 * sin, x1 * sin + x2 * cos, rest], axis=-1)


def _split_cols(h, sizes):
    offs = np.cumsum(np.array(sizes))[:-1].tolist()
    return jnp.split(h, offs, axis=-1)


def _chunk_mask(qpos, kpos):
    return (kpos[None, :] // CHUNK) <= (qpos[:, None] // CHUNK)


def _mla_core(q_abs, q_rope, ckv, krope, qpos, kpos):
    s = (jnp.einsum('bqhc,bkc->bhqk', q_abs, ckv) + jnp.einsum('bqhr,bkr->bhqk', q_rope, krope)).astype(jnp.float32)
    s = jnp.where(_chunk_mask(qpos, kpos), s * (MLA_QK_DIM ** -0.5), NEG_INF)
    p = jax.nn.softmax(s, axis=-1).astype(ckv.dtype)
    return jnp.einsum('bhqk,bkc->bqhc', p, ckv)


def _diff_core(q, k, v, qpos, kpos, lam):
    s = jnp.einsum('bqhid,bkhid->bhiqk', q, k).astype(jnp.float32) * (DIFF_HD ** -0.5)
    s = jnp.where(_chunk_mask(qpos, kpos), s, NEG_INF)
    p = jax.nn.softmax(s, axis=-1)
    a = (p[:, :, 0] - lam * p[:, :, 1]).astype(v.dtype)
    return jnp.einsum('bhqk,bkhe->bqhe', a, v)


def _sb_core(q, k, v, qpos, kpos):
    z = jnp.einsum('bqhd,bkhd->bhqk', q, k).astype(jnp.float32) * (SB_HD ** -0.5)
    valid = kpos[None, :] < qpos[:, None]
    log_beta = jax.nn.log_sigmoid(z)
    log_rem = jnp.where(valid, jax.nn.log_sigmoid(-z), 0.0)
    later = lax.cumsum(log_rem, axis=3, reverse=True) - log_rem
    a = jnp.where(valid, jnp.exp(log_beta + later), 0.0).astype(v.dtype)
    return jnp.einsum('bhqk,bkhd->bqhd', a, v)


def _sweep_queries(core, q_args, kv_args, seq_len):
    nb = seq_len // Q_BLOCK
    kpos = jnp.arange(seq_len, dtype=jnp.int32)

    def to_blocks(a):
        return jnp.swapaxes(a.reshape((a.shape[0], nb, Q_BLOCK) + a.shape[2:]), 0, 1)

    def one_block(args):
        i, qb = args
        qpos = i * Q_BLOCK + jnp.arange(Q_BLOCK, dtype=jnp.int32)
        return core(*qb, *kv_args, qpos, kpos)

    out = lax.map(one_block, (jnp.arange(nb, dtype=jnp.int32), tuple(to_blocks(a) for a in q_args)))
    out = jnp.swapaxes(out, 0, 1)
    return out.reshape((out.shape[0], seq_len) + out.shape[3:])


def _even_layer(x, pos, caches, prm, lam_init):
    (w_in, q_g, w_uq, kv_g, w_uk, w_uv, lam_q, lam_k, subln_g, w_out, ln_g, ln_b) = prm
    b, s, _ = x.shape
    h = x @ w_in
    q_lat, ckv, krope, g_mla, dq, dk, dv, g_diff = _split_cols(h, EVEN_SECTIONS)
    q = (_rmsnorm(q_lat, q_g) @ w_uq).reshape(b, s, MLA_HEADS, MLA_QK_DIM)
    q_rope = _rope(q[..., MLA_NOPE_DIM:], pos, MLA_ROPE_DIM, MLA_ROPE_THETA)
    q_abs = jnp.einsum('bshn,chn->bshc', q[..., :MLA_NOPE_DIM], w_uk)
    ckv = _rmsnorm(ckv, kv_g)
    krope = _rope(krope, pos, MLA_ROPE_DIM, MLA_ROPE_THETA)
    dq = _rope(dq.reshape(b, s, DIFF_HEADS, 2, DIFF_HD), pos, DIFF_ROT, ROPE_THETA)
    dk = _rope(dk.reshape(b, s, DIFF_HEADS, 2, DIFF_HD), pos, DIFF_ROT, ROPE_THETA)
    dv = dv.reshape(b, s, DIFF_HEADS, 2 * DIFF_HD)
    lq = lam_q.astype(jnp.float32)
    lk = lam_k.astype(jnp.float32)
    lam = jnp.exp(jnp.sum(lq[0] * lk[0])) - jnp.exp(jnp.sum(lq[1] * lk[1])) + lam_init
    diff_core = functools.partial(_diff_core, lam=lam)
    if caches is None:
        o_lat = _sweep_queries(_mla_core, (q_abs, q_rope), (ckv, krope), s)
        o_diff = _sweep_queries(diff_core, (dq,), (dk, dv), s)
    else:
        ckv_c, krope_c, dk_c, dv_c = caches
        kpos = jnp.arange(ckv_c.shape[1] + s, dtype=jnp.int32)
        o_lat = _mla_core(q_abs, q_rope, jnp.concatenate([ckv_c, ckv], axis=1),
                          jnp.concatenate([krope_c, krope], axis=1), pos, kpos)
        o_diff = diff_core(dq, jnp.concatenate([dk_c, dk], axis=1), jnp.concatenate([dv_c, dv], axis=1), pos, kpos)
    o_mla = jnp.einsum('bshc,chv->bshv', o_lat, w_uv).reshape(b, s, MLA_W)
    o_diff = (_rmsnorm(o_diff, subln_g) * (1.0 - lam_init)).reshape(b, s, DIFF_W)
    o = jnp.concatenate([o_mla * jax.nn.silu(g_mla), o_diff * jax.nn.silu(g_diff)], axis=-1) @ w_out
    y = _layernorm(ALPHA * x + o, ln_g, ln_b)
    return y, (ckv, krope, dk, dv)


def _odd_layer(x, pos, caches, prm):
    w_in, w_out, ln_g, ln_b = prm
    b, s, _ = x.shape
    q, k, v, g = _split_cols(x @ w_in, (SB_W, SB_W, SB_W, SB_W))
    q = q.reshape(b, s, SB_HEADS, SB_HD)
    k = k.reshape(b, s, SB_HEADS, SB_HD)
    v = v.reshape(b, s, SB_HEADS, SB_HD)
    if caches is None:
        o = _sweep_queries(_sb_core, (q,), (k, v), s)
    else:
        k_c, v_c = caches
        kpos = jnp.arange(k_c.shape[1] + s, dtype=jnp.int32)
        o = _sb_core(q, jnp.concatenate([k_c, k], axis=1), jnp.concatenate([v_c, v], axis=1), pos, kpos)
    o = (o.reshape(b, s, SB_W) * jax.nn.silu(g)) @ w_out
    y = _layernorm(ALPHA * x + o, ln_g, ln_b)
    return y, (k, v)


def setup_inputs(seed: int = 0) -> dict:
    key = jax.random.key(seed)
    ks = jax.random.split(key, 24)

    def nrm(k, shape, scale):
        return jax.random.normal(k, shape, jnp.float32) * scale

    w_in_even = nrm(ks[8], (N_EVEN, D_MODEL, EVEN_IN), D_MODEL ** -0.5)
    w_in_even = w_in_even.at[:, :, DV_START:DV_START + DIFF_W].multiply(BETA)
    w_in_odd = nrm(ks[19], (N_ODD, D_MODEL, 4 * SB_W), D_MODEL ** -0.5)
    w_in_odd = w_in_odd.at[:, :, 2 * SB_W:3 * SB_W].multiply(BETA)
    return {
        'x_prompt': nrm(ks[0], (BATCH, SEQ, D_MODEL), 1.0),
        'x_sample': nrm(ks[1], (DEC_BATCH, DEC_SEQ, D_MODEL), 1.0),
        'cache_mla_ckv': nrm(ks[2], (N_EVEN, DEC_BATCH, PAST_LEN, KV_LORA), 1.0),
        'cache_mla_krope': nrm(ks[3], (N_EVEN, DEC_BATCH, PAST_LEN, MLA_ROPE_DIM), 1.0),
        'cache_diff_k': nrm(ks[4], (N_EVEN, DEC_BATCH, PAST_LEN, DIFF_HEADS, 2, DIFF_HD), 1.0),
        'cache_diff_v': nrm(ks[5], (N_EVEN, DEC_BATCH, PAST_LEN, DIFF_HEADS, 2 * DIFF_HD), BETA),
        'cache_sb_k': nrm(ks[6], (N_ODD, DEC_BATCH, PAST_LEN, SB_HEADS, SB_HD), 1.0),
        'cache_sb_v': nrm(ks[7], (N_ODD, DEC_BATCH, PAST_LEN, SB_HEADS, SB_HD), BETA),
        'w_in_even': w_in_even,
        'q_norm_g': 1.0 + nrm(ks[9], (N_EVEN, Q_LORA), 0.02),
        'w_uq': nrm(ks[10], (N_EVEN, Q_LORA, MLA_HEADS * MLA_QK_DIM), Q_LORA ** -0.5),
        'kv_norm_g': 1.0 + nrm(ks[11], (N_EVEN, KV_LORA), 0.02),
        'w_uk': nrm(ks[12], (N_EVEN, KV_LORA, MLA_HEADS, MLA_NOPE_DIM), KV_LORA ** -0.5),
        'w_uv': nrm(ks[13], (N_EVEN, KV_LORA, MLA_HEADS, MLA_V_DIM), KV_LORA ** -0.5 * BETA),
        'diff_lambda_q': nrm(ks[14], (N_EVEN, 2, DIFF_HD), 0.1),
        'diff_lambda_k': nrm(ks[15], (N_EVEN, 2, DIFF_HD), 0.1),
        'diff_subln_g': 1.0 + nrm(ks[16], (N_EVEN, 2 * DIFF_HD), 0.02),
        'w_out_even': nrm(ks[17], (N_EVEN, EVEN_OUT, D_MODEL), EVEN_OUT ** -0.5 * BETA),
        'ln_even_g': 1.0 + nrm(ks[18], (N_EVEN, D_MODEL), 0.02),
        'ln_even_b': nrm(ks[20], (N_EVEN, D_MODEL), 0.02),
        'w_in_odd': w_in_odd,
        'w_out_odd': nrm(ks[21], (N_ODD, SB_W, D_MODEL), SB_W ** -0.5 * BETA),
        'ln_odd_g': 1.0 + nrm(ks[22], (N_ODD, D_MODEL), 0.02),
        'ln_odd_b': nrm(ks[23], (N_ODD, D_MODEL), 0.02),
    }


def reference(x_prompt, x_sample, cache_mla_ckv, cache_mla_krope, cache_diff_k, cache_diff_v, cache_sb_k, cache_sb_v,
              w_in_even, q_norm_g, w_uq, kv_norm_g, w_uk, w_uv, diff_lambda_q, diff_lambda_k, diff_subln_g,
              w_out_even, ln_even_g, ln_even_b, w_in_odd, w_out_odd, ln_odd_g, ln_odd_b):
    pos_p = jnp.arange(x_prompt.shape[1], dtype=jnp.int32)
    pos_s = cache_sb_k.shape[2] + jnp.arange(x_sample.shape[1], dtype=jnp.int32)
    yp, ys = x_prompt, x_sample
    even_p, even_s, odd_p, odd_s = [], [], [], []
    for layer in range(DEPTH):
        i = layer // 2
        if layer % 2 == 0:
            prm = (w_in_even[i], q_norm_g[i], w_uq[i], kv_norm_g[i], w_uk[i], w_uv[i], diff_lambda_q[i],
                   diff_lambda_k[i], diff_subln_g[i], w_out_even[i], ln_even_g[i], ln_even_b[i])
            lam_init = 0.8 - 0.6 * math.exp(-0.3 * layer)
            yp, st_p = _even_layer(yp, pos_p, None, prm, lam_init)
            ys, st_s = _even_layer(ys, pos_s, (cache_mla_ckv[i], cache_mla_krope[i], cache_diff_k[i], cache_diff_v[i]),
                                   prm, lam_init)
            even_p.append(st_p)
            even_s.append(st_s)
        else:
            prm = (w_in_odd[i], w_out_odd[i], ln_odd_g[i], ln_odd_b[i])
            yp, st_p = _odd_layer(yp, pos_p, None, prm)
            ys, st_s = _odd_layer(ys, pos_s, (cache_sb_k[i], cache_sb_v[i]), prm)
            odd_p.append(st_p)
            odd_s.append(st_s)
    return (yp, ys,
            jnp.stack([t[0] for t in even_p]), jnp.stack([t[1] for t in even_p]),
            jnp.stack([t[2] for t in even_p]), jnp.stack([t[3] for t in even_p]),
            jnp.stack([t[0] for t in odd_p]), jnp.stack([t[1] for t in odd_p]),
            jnp.stack([t[0] for t in even_s]), jnp.stack([t[1] for t in even_s]),
            jnp.stack([t[2] for t in even_s]), jnp.stack([t[3] for t in even_s]),
            jnp.stack([t[0] for t in odd_s]), jnp.stack([t[1] for t in odd_s]))
```

```python
import functools
import math

import numpy as np
import jax
import jax.numpy as jnp
from jax import lax
from jax.experimental import pallas as pl
from jax.experimental.pallas import tpu as pltpu

F32 = jnp.float32
BF16 = jnp.bfloat16

D_MODEL = 1024
DEPTH = 2
CHUNK = 64
ALPHA = (2 * DEPTH) ** 0.25
EPS = 1e-5
NEG_INF = -1e30
MLA_HEADS = 8
Q_LORA = 256
KV_LORA = 128
MLA_NOPE_DIM = 64
MLA_ROPE_DIM = 32
MLA_QK_DIM = MLA_NOPE_DIM + MLA_ROPE_DIM
MLA_V_DIM = 64
MLA_ROPE_THETA = 10000.0
MLA_W = MLA_HEADS * MLA_V_DIM
DIFF_HEADS = 4
DIFF_HD = 64
DIFF_ROT = DIFF_HD // 4
ROPE_THETA = 500000.0
DIFF_W = DIFF_HEADS * 2 * DIFF_HD
SB_HEADS = 16
SB_HD = 64
SB_W = SB_HEADS * SB_HD

LANES = 128
MLA_QCAT = 2 * LANES
VMEM_LIMIT = 56 * 1024 * 1024

_NT = (((1,), (1,)), ((), ()))


def _params(sem):
    return pltpu.CompilerParams(dimension_semantics=sem, vmem_limit_bytes=VMEM_LIMIT)


def _const_spec(shape):
    nd = len(shape)
    return pl.BlockSpec(shape, lambda *_: (0,) * nd, pipeline_mode=pl.Buffered(1))


def _rope_tables(pos, rot_dim, theta, period):
    half = rot_dim // 2
    inv = jnp.exp(-math.log(theta) * jnp.arange(half, dtype=F32) * (2.0 / rot_dim))
    ang = pos.astype(F32)[:, None] * inv[None, :]
    cos, sin = jnp.cos(ang), jnp.sin(ang)
    lane = np.arange(LANES) % period
    idx = lane % half
    in_rot = lane < rot_dim
    first = lane < half
    second = in_rot & ~first
    cos_l, sin_l = cos[:, idx], sin[:, idx]
    c = jnp.where(in_rot[None, :], cos_l, 1.0)
    sa = jnp.where(first[None, :], -sin_l, 0.0)
    sb = jnp.where(second[None, :], sin_l, 0.0)
    return c, sa, sb, half


def _apply_rope(x, c, sa, sb, half):
    return x * c + pltpu.roll(x, LANES - half, 1) * sa + pltpu.roll(x, half, 1) * sb


def _silu(x):
    return x / (1.0 + jnp.exp(-x))


def _rms(x, g):
    ms = jnp.mean(x * x, axis=-1, keepdims=True)
    return x * lax.rsqrt(ms + EPS) * g


def _layernorm(r, g, b):
    mu = jnp.mean(r, axis=-1, keepdims=True)
    rc = r - mu
    var = jnp.mean(rc * rc, axis=-1, keepdims=True)
    return rc * lax.rsqrt(var + EPS) * g + b


_C_QLAT = 0
_C_CKV = Q_LORA
_C_KROPE = _C_CKV + KV_LORA
_C_GMLA = _C_KROPE + LANES
_C_DQ = _C_GMLA + MLA_W
_C_DK = _C_DQ + DIFF_W
_C_DV = _C_DK + DIFF_W
_C_GDIFF = _C_DV + DIFF_W
_EVEN_COLS = _C_GDIFF + DIFF_W


def _even_in_kernel(x_ref, w_ref, wqn_ref, wqr_ref, wuk_ref, qg_ref, kvg_ref,
                    mc_ref, msa_ref, msb_ref, dc_ref, dsa_ref, dsb_ref,
                    qcat_ref, kcat_ref, ckv_ref, krope_ref, gate_ref, dq_ref,
                    dkf_ref, dkb_ref, dvf_ref, dvb_ref):
    xb = x_ref[...].astype(BF16)

    def proj(c0, c1):
        return jnp.dot(xb, w_ref[:, c0:c1], preferred_element_type=F32)

    mla_half = MLA_ROPE_DIM // 2
    diff_half = DIFF_ROT // 2
    mc, msa, msb = mc_ref[...], msa_ref[...], msb_ref[...]
    dc, dsa, dsb = dc_ref[...], dsa_ref[...], dsb_ref[...]

    qn = _rms(proj(_C_QLAT, _C_QLAT + Q_LORA), qg_ref[...]).astype(BF16)
    q_nope = jnp.dot(qn, wqn_ref[...], preferred_element_type=F32).astype(BF16)
    q_abs = jnp.dot(q_nope, wuk_ref[...], preferred_element_type=F32)
    q_rp = jnp.dot(qn, wqr_ref[...], preferred_element_type=F32)
    scale = MLA_QK_DIM ** -0.5
    for h in range(MLA_HEADS):
        rot = _apply_rope(q_rp[:, h * LANES:(h + 1) * LANES], mc, msa, msb, mla_half)
        qcat_ref[:, h * MLA_QCAT:h * MLA_QCAT + LANES] = (q_abs[:, h * LANES:(h + 1) * LANES] * scale).astype(BF16)
        qcat_ref[:, h * MLA_QCAT + LANES:(h + 1) * MLA_QCAT] = (rot * scale).astype(BF16)

    kv = proj(_C_CKV, _C_GMLA)
    ckv = _rms(kv[:, :KV_LORA], kvg_ref[...])
    krope = _apply_rope(kv[:, KV_LORA:], mc, msa, msb, mla_half)
    ckv_ref[...] = ckv
    krope_ref[...] = krope[:, :MLA_ROPE_DIM]
    kcat_ref[:, :LANES] = ckv.astype(BF16)
    kcat_ref[:, LANES:] = krope.astype(BF16)

    gate_ref[:, :MLA_W] = _silu(proj(_C_GMLA, _C_DQ)).astype(BF16)
    gate_ref[:, MLA_W:] = _silu(proj(_C_GDIFF, _EVEN_COLS)).astype(BF16)

    dq = proj(_C_DQ, _C_DK)
    dk = proj(_C_DK, _C_DV)
    dv = proj(_C_DV, _C_GDIFF)
    dscale = DIFF_HD ** -0.5
    for h in range(DIFF_HEADS):
        sl = slice(h * LANES, (h + 1) * LANES)
        dq_ref[:, sl] = (_apply_rope(dq[:, sl], dc, dsa, dsb, diff_half) * dscale).astype(BF16)
        kr = _apply_rope(dk[:, sl], dc, dsa, dsb, diff_half)
        dkf_ref[:, sl] = kr
        dkb_ref[:, sl] = kr.astype(BF16)
    dvf_ref[...] = dv
    dvb_ref[...] = dv.astype(BF16)


def _even_in(x2d, wts, tabs, tab_rows_per_rep, tm):
    t = x2d.shape[0]
    tm = min(tm, t)
    nrep = tab_rows_per_rep // tm
    w_in, wqn, wqr, wuk, qg, kvg = wts
    row = lambda w: pl.BlockSpec((tm, w), lambda i: (i, 0))
    tab = pl.BlockSpec((tm, LANES), lambda i: (i % nrep, 0))
    outs = [
        (MLA_HEADS * MLA_QCAT, BF16),
        (MLA_QCAT, BF16),
        (KV_LORA, F32),
        (MLA_ROPE_DIM, F32),
        (MLA_W + DIFF_W, BF16),
        (DIFF_W, BF16),
        (DIFF_W, F32), (DIFF_W, BF16),
        (DIFF_W, F32), (DIFF_W, BF16),
    ]
    return pl.pallas_call(
        _even_in_kernel,
        grid=(t // tm,),
        in_specs=[row(D_MODEL), _const_spec(w_in.shape), _const_spec(wqn.shape), _const_spec(wqr.shape),
                  _const_spec(wuk.shape), _const_spec(qg.shape), _const_spec(kvg.shape)] + [tab] * 6,
        out_specs=[row(w) for w, _ in outs],
        out_shape=[jax.ShapeDtypeStruct((t, w), d) for w, d in outs],
        compiler_params=_params(("parallel",)),
        name="even_in",
    )(x2d, w_in, wqn, wqr, wuk, qg, kvg, *tabs)


def _softmax_step(qs, kb, vb, m_ref, l_ref, acc_ref, valid):
    s = lax.dot_general(qs, kb, _NT, preferred_element_type=F32)
    if valid is not None:
        s = jnp.where(valid, s, NEG_INF)
    m_old = m_ref[...]
    m_new = jnp.maximum(m_old, jnp.max(s, axis=-1, keepdims=True))
    alpha = jnp.exp(m_old - m_new)
    p = jnp.exp(s - m_new)
    l_ref[...] = alpha * l_ref[...] + jnp.sum(p, axis=-1, keepdims=True)
    acc_ref[...] = alpha * acc_ref[...] + jnp.dot(p.astype(BF16), vb, preferred_element_type=F32)
    m_ref[...] = m_new


def _chunk_causal_flash(qs, get_k, get_v, q0, tq, tk, m_ref, l_ref, acc_ref):
    m_rows = qs.shape[0]
    m_ref[...] = jnp.full(m_ref.shape, NEG_INF, F32)
    l_ref[...] = jnp.zeros(l_ref.shape, F32)
    acc_ref[...] = jnp.zeros(acc_ref.shape, F32)
    n_full = q0 // tk

    def body(j, carry):
        k0 = pl.multiple_of(j * tk, tk)
        _softmax_step(qs, get_k(k0), get_v(k0), m_ref, l_ref, acc_ref, None)
        return carry

    lax.fori_loop(0, n_full, body, 0)
    k0 = pl.multiple_of(n_full * tk, tk)
    kpos = k0 + lax.broadcasted_iota(jnp.int32, (m_rows, tk), 1)
    qpos = q0 + (lax.broadcasted_iota(jnp.int32, (m_rows, tk), 0) & (tq - 1))
    shift = CHUNK.bit_length() - 1
    valid = (kpos >> shift) <= (qpos >> shift)
    _softmax_step(qs, get_k(k0), get_v(k0), m_ref, l_ref, acc_ref, valid)


def _mla_prompt_kernel(q_ref, k_ref, o_ref, m_ref, l_ref, acc_ref, *, tq, tk):
    q0 = pl.program_id(1) * tq
    qs = jnp.concatenate([q_ref[0, :, h * MLA_QCAT:(h + 1) * MLA_QCAT] for h in range(MLA_HEADS)], axis=0)
    get_k = lambda k0: k_ref[0, pl.ds(k0, tk), :]
    get_v = lambda k0: k_ref[0, pl.ds(k0, tk), :KV_LORA]
    _chunk_causal_flash(qs, get_k, get_v, q0, tq, tk, m_ref, l_ref, acc_ref)
    out = acc_ref[...] / l_ref[...]
    for h in range(MLA_HEADS):
        o_ref[0, :, h * KV_LORA:(h + 1) * KV_LORA] = out[h * tq:(h + 1) * tq].astype(o_ref.dtype)


def _mla_prompt(qcat, kcat, tq, tk):
    b, s, _ = qcat.shape
    tq, tk = min(tq, s), min(tk, s)
    m_rows = MLA_HEADS * tq
    return pl.pallas_call(
        functools.partial(_mla_prompt_kernel, tq=tq, tk=tk),
        grid=(b, s // tq),
        in_specs=[pl.BlockSpec((1, tq, MLA_HEADS * MLA_QCAT), lambda bi, i: (bi, i, 0)),
                  pl.BlockSpec((1, s, MLA_QCAT), lambda bi, i: (bi, 0, 0))],
        out_specs=pl.BlockSpec((1, tq, MLA_HEADS * KV_LORA), lambda bi, i: (bi, i, 0)),
        out_shape=jax.ShapeDtypeStruct((b, s, MLA_HEADS * KV_LORA), BF16),
        scratch_shapes=[pltpu.VMEM((m_rows, 1), F32), pltpu.VMEM((m_rows, 1), F32),
                        pltpu.VMEM((m_rows, KV_LORA), F32)],
        compiler_params=_params(("parallel", "parallel")),
        name="mla_prompt",
    )(qcat, kcat)


def _lane_halves(x):
    lane = lax.broadcasted_iota(jnp.int32, x.shape, 1)
    zero = jnp.zeros_like(x)
    return jnp.concatenate([jnp.where(lane < LANES // 2, x, zero), jnp.where(lane < LANES // 2, zero, x)], axis=0)


def _diff_lambda(lq_ref, lk_ref, lam_init):
    e = jnp.exp(jnp.sum(lq_ref[...] * lk_ref[...], axis=-1, keepdims=True))
    return e[0:1] - e[1:2] + lam_init


def _diff_prompt_kernel(q_ref, k_ref, v_ref, lq_ref, lk_ref, o_ref, m_ref, l_ref, acc_ref, *, tq, tk, lam_init):
    q0 = pl.program_id(1) * tq
    lam = _diff_lambda(lq_ref, lk_ref, lam_init)
    for h in range(DIFF_HEADS):
        sl = slice(h * LANES, (h + 1) * LANES)
        qs = _lane_halves(q_ref[0, :, sl])
        get_k = lambda k0, sl=sl: k_ref[0, pl.ds(k0, tk), sl]
        get_v = lambda k0, sl=sl: v_ref[0, pl.ds(k0, tk), sl]
        _chunk_causal_flash(qs, get_k, get_v, q0, tq, tk, m_ref, l_ref, acc_ref)
        o = acc_ref[...] / l_ref[...]
        o_ref[0, :, sl] = o[:tq] - lam * o[tq:]


def _diff_prompt(dq, dk, dv, lq, lk, lam_init, tq, tk):
    b, s, _ = dq.shape
    tq, tk = min(tq, s), min(tk, s)
    qspec = pl.BlockSpec((1, tq, DIFF_W), lambda bi, i: (bi, i, 0))
    kvspec = pl.BlockSpec((1, s, DIFF_W), lambda bi, i: (bi, 0, 0))
    return pl.pallas_call(
        functools.partial(_diff_prompt_kernel, tq=tq, tk=tk, lam_init=lam_init),
        grid=(b, s // tq),
        in_specs=[qspec, kvspec, kvspec, _const_spec(lq.shape), _const_spec(lk.shape)],
        out_specs=qspec,
        out_shape=jax.ShapeDtypeStruct((b, s, DIFF_W), F32),
        scratch_shapes=[pltpu.VMEM((2 * tq, 1), F32), pltpu.VMEM((2 * tq, 1), F32),
                        pltpu.VMEM((2 * tq, LANES), F32)],
        compiler_params=_params(("parallel", "parallel")),
        name="diff_prompt",
    )(dq, dk, dv, lq, lk)


def _pad_rows(x, rows):
    return jnp.concatenate([x, jnp.zeros((rows - x.shape[0], x.shape[1]), x.dtype)], axis=0)


def _decode_softmax(s_c, s_n, v_cache, v_new, n_new):
    col = lax.broadcasted_iota(jnp.int32, s_n.shape, 1)
    s_n = jnp.where(col < n_new, s_n, NEG_INF)
    m = jnp.maximum(jnp.max(s_c, axis=-1, keepdims=True), jnp.max(s_n, axis=-1, keepdims=True))
    p_c = jnp.exp(s_c - m)
    p_n = jnp.exp(s_n - m)
    l = jnp.sum(p_c, axis=-1, keepdims=True) + jnp.sum(p_n, axis=-1, keepdims=True)
    acc = (jnp.dot(p_c.astype(BF16), v_cache, preferred_element_type=F32)
           + jnp.dot(p_n.astype(BF16), v_new, preferred_element_type=F32))
    return acc / l


def _mla_decode_kernel(q_ref, ckv_ref, kr_ref, kn_ref, o_ref, *, n_new):
    qs = jnp.concatenate([q_ref[0, :, h * MLA_QCAT:(h + 1) * MLA_QCAT] for h in range(MLA_HEADS)], axis=0)
    ckv = ckv_ref[0].astype(BF16)
    kr = kr_ref[0].astype(BF16)
    k_new = _pad_rows(kn_ref[0], LANES)
    s_c = (lax.dot_general(qs[:, :KV_LORA], ckv, _NT, preferred_element_type=F32)
           + lax.dot_general(qs[:, KV_LORA:KV_LORA + MLA_ROPE_DIM], kr, _NT, preferred_element_type=F32))
    s_n = lax.dot_general(qs, k_new, _NT, preferred_element_type=F32)
    out = _decode_softmax(s_c, s_n, ckv, k_new[:, :KV_LORA], n_new)
    for h in range(MLA_HEADS):
        o_ref[0, :, h * KV_LORA:(h + 1) * KV_LORA] = out[h * n_new:(h + 1) * n_new].astype(o_ref.dtype)


def _mla_decode(qcat, ckv_c, kr_c, kcat_new):
    b, n, _ = qcat.shape
    past = ckv_c.shape[1]
    return pl.pallas_call(
        functools.partial(_mla_decode_kernel, n_new=n),
        grid=(b,),
        in_specs=[pl.BlockSpec((1, n, MLA_HEADS * MLA_QCAT), lambda i: (i, 0, 0)),
                  pl.BlockSpec((1, past, KV_LORA), lambda i: (i, 0, 0)),
                  pl.BlockSpec((1, past, MLA_ROPE_DIM), lambda i: (i, 0, 0)),
                  pl.BlockSpec((1, n, MLA_QCAT), lambda i: (i, 0, 0))],
        out_specs=pl.BlockSpec((1, n, MLA_HEADS * KV_LORA), lambda i: (i, 0, 0)),
        out_shape=jax.ShapeDtypeStruct((b, n, MLA_HEADS * KV_LORA), BF16),
        compiler_params=_params(("parallel",)),
        name="mla_decode",
    )(qcat, ckv_c, kr_c, kcat_new)


def _diff_decode_kernel(q_ref, kc_ref, vc_ref, kn_ref, vn_ref, lq_ref, lk_ref, o_ref, *, n_new, lam_init):
    lam = _diff_lambda(lq_ref, lk_ref, lam_init)
    qs = _lane_halves(q_ref[0])
    s_c = lax.dot_general(qs, kc_ref[0].astype(BF16), _NT, preferred_element_type=F32)
    s_n = lax.dot_general(qs, _pad_rows(kn_ref[0], LANES), _NT, preferred_element_type=F32)
    o = _decode_softmax(s_c, s_n, vc_ref[0].astype(BF16), _pad_rows(vn_ref[0], LANES), n_new)
    o_ref[0] = o[:n_new] - lam * o[n_new:]


def _diff_decode(dq, kc, vc, dk_new, dv_new, lq, lk, lam_init):
    b, n, _ = dq.shape
    past = kc.shape[1]
    new = pl.BlockSpec((1, n, LANES), lambda i, h: (i, 0, h))
    cache = pl.BlockSpec((1, past, LANES), lambda i, h: (i, 0, h))
    return pl.pallas_call(
        functools.partial(_diff_decode_kernel, n_new=n, lam_init=lam_init),
        grid=(b, DIFF_HEADS),
        in_specs=[new, cache, cache, new, new, _const_spec(lq.shape), _const_spec(lk.shape)],
        out_specs=new,
        out_shape=jax.ShapeDtypeStruct((b, n, DIFF_W), F32),
        compiler_params=_params(("parallel", "parallel")),
        name="diff_decode",
    )(dq, kc, vc, dk_new, dv_new, lq, lk)


def _even_out_kernel(olat_ref, odiff_ref, gate_ref, x_ref, wuv_ref, sg_ref, wout_ref, lng_ref, lnb_ref, y_ref,
                     *, lam_init):
    gate = gate_ref[...]
    o_mla = jnp.dot(olat_ref[...], wuv_ref[...], preferred_element_type=F32)
    parts = [(o_mla * gate[:, :MLA_W].astype(F32)).astype(BF16)]
    od = odiff_ref[...]
    for h in range(DIFF_HEADS):
        sl = slice(h * LANES, (h + 1) * LANES)
        n = _rms(od[:, sl], sg_ref[...]) * (1.0 - lam_init)
        parts.append((n * gate[:, MLA_W + h * LANES:MLA_W + (h + 1) * LANES].astype(F32)).astype(BF16))
    o = jnp.dot(jnp.concatenate(parts, axis=1), wout_ref[...], preferred_element_type=F32)
    y_ref[...] = _layernorm(ALPHA * x_ref[...] + o, lng_ref[...], lnb_ref[...])


def _even_out(olat, odiff, gate, x2d, wuv_bd, sg, wout, lng, lnb, lam_init, tm):
    t = x2d.shape[0]
    tm = min(tm, t)
    row = lambda w: pl.BlockSpec((tm, w), lambda i: (i, 0))
    return pl.pallas_call(
        functools.partial(_even_out_kernel, lam_init=lam_init),
        grid=(t // tm,),
        in_specs=[row(MLA_HEADS * KV_LORA), row(DIFF_W), row(MLA_W + DIFF_W), row(D_MODEL),
                  _const_spec(wuv_bd.shape), _const_spec(sg.shape), _const_spec(wout.shape),
                  _const_spec(lng.shape), _const_spec(lnb.shape)],
        out_specs=row(D_MODEL),
        out_shape=jax.ShapeDtypeStruct((t, D_MODEL), F32),
        compiler_params=_params(("parallel",)),
        name="even_out",
    )(olat, odiff, gate, x2d, wuv_bd, sg, wout, lng, lnb)


def _odd_in_kernel(x_ref, w_ref, q_ref, kf_ref, kb_ref, vf_ref, vb_ref, g_ref):
    xb = x_ref[...].astype(BF16)

    def proj(i):
        return jnp.dot(xb, w_ref[:, i * SB_W:(i + 1) * SB_W], preferred_element_type=F32)

    q_ref[...] = (proj(0) * SB_HD ** -0.5).astype(BF16)
    k = proj(1)
    kf_ref[...] = k
    kb_ref[...] = k.astype(BF16)
    v = proj(2)
    vf_ref[...] = v
    vb_ref[...] = v.astype(BF16)
    g_ref[...] = _silu(proj(3)).astype(BF16)


def _odd_in(x2d, w_in, tm):
    t = x2d.shape[0]
    tm = min(tm, t)
    row = pl.BlockSpec((tm, SB_W), lambda i: (i, 0))
    dts = [BF16, F32, BF16, F32, BF16, BF16]
    return pl.pallas_call(
        _odd_in_kernel,
        grid=(t // tm,),
        in_specs=[pl.BlockSpec((tm, D_MODEL), lambda i: (i, 0)), _const_spec(w_in.shape)],
        out_specs=[row] * len(dts),
        out_shape=[jax.ShapeDtypeStruct((t, SB_W), d) for d in dts],
        compiler_params=_params(("parallel",)),
        name="odd_in",
    )(x2d, w_in)


def _sb_step(qs, kb, vb, u, c, valid):
    z = lax.dot_general(qs, kb, _NT, preferred_element_type=F32)
    lg = jnp.log(1.0 + jnp.exp(-jnp.abs(z)))
    sp = jnp.maximum(z, 0.0) + lg
    lb = jnp.minimum(z, 0.0) - lg
    if valid is not None:
        sp = jnp.where(valid, sp, 0.0)
    spb = sp.astype(BF16)
    later = jnp.dot(spb, u, preferred_element_type=F32)
    a = jnp.exp(lb - later - c)
    if valid is not None:
        a = jnp.where(valid, a, 0.0)
    pv = jnp.dot(a.astype(BF16), vb, preferred_element_type=F32)
    c_new = c + later[:, :1] + spb[:, :1].astype(F32)
    return pv, c_new


def _later_matrix(n):
    return (lax.broadcasted_iota(jnp.int32, (n, n), 0) > lax.broadcasted_iota(jnp.int32, (n, n), 1)).astype(BF16)


def _sb_prompt_kernel(q_ref, k_ref, v_ref, o_ref, c_ref, acc_ref, *, tq, pairs):
    i = pl.program_id(2)
    q0 = i * tq
    u = _later_matrix(tq)
    rows = lax.broadcasted_iota(jnp.int32, (2 * tq, tq), 0) & (tq - 1)
    cols = lax.broadcasted_iota(jnp.int32, (2 * tq, tq), 1)
    diag_valid = cols < rows
    lane = lax.broadcasted_iota(jnp.int32, (tq, LANES), 1)
    for p in range(pairs):
        sl = slice(p * LANES, (p + 1) * LANES)
        qs = _lane_halves(q_ref[0, :, sl])
        k0 = pl.multiple_of(q0, tq)
        pv, c = _sb_step(qs, k_ref[0, pl.ds(k0, tq), sl], v_ref[0, pl.ds(k0, tq), sl], u,
                         jnp.zeros((2 * tq, 1), F32), diag_valid)
        acc_ref[...] = pv
        c_ref[...] = c

        def body(jj, carry, sl=sl, qs=qs):
            kk = pl.multiple_of((i - 1 - jj) * tq, tq)
            pv, c = _sb_step(qs, k_ref[0, pl.ds(kk, tq), sl], v_ref[0, pl.ds(kk, tq), sl], u, c_ref[...], None)
            acc_ref[...] += pv
            c_ref[...] = c
            return carry

        lax.fori_loop(0, i, body, 0)
        acc = acc_ref[...]
        o_ref[0, :, sl] = jnp.where(lane < LANES // 2, acc[:tq], acc[tq:]).astype(o_ref.dtype)


def _sb_prompt(q, k, v, tq, pairs):
    b, s, w = q.shape
    tq = min(tq, s)
    gw = pairs * LANES
    qspec = pl.BlockSpec((1, tq, gw), lambda bi, g, i: (bi, i, g))
    kvspec = pl.BlockSpec((1, s, gw), lambda bi, g, i: (bi, 0, g))
    return pl.pallas_call(
        functools.partial(_sb_prompt_kernel, tq=tq, pairs=pairs),
        grid=(b, w // gw, s // tq),
        in_specs=[qspec, kvspec, kvspec],
        out_specs=qspec,
        out_shape=jax.ShapeDtypeStruct((b, s, w), BF16),
        scratch_shapes=[pltpu.VMEM((2 * tq, 1), F32), pltpu.VMEM((2 * tq, LANES), F32)],
        compiler_params=_params(("parallel", "parallel", "parallel")),
        name="sb_prompt",
    )(q, k, v)


def _sb_decode_kernel(q_ref, kc_ref, vc_ref, kn_ref, vn_ref, o_ref, *, n_new, tk):
    qs = _lane_halves(q_ref[0])
    rows = lax.broadcasted_iota(jnp.int32, (2 * n_new, LANES), 0) & (n_new - 1)
    cols = lax.broadcasted_iota(jnp.int32, (2 * n_new, LANES), 1)
    acc, c = _sb_step(qs, _pad_rows(kn_ref[0], LANES), _pad_rows(vn_ref[0], LANES), _later_matrix(LANES),
                      jnp.zeros((2 * n_new, 1), F32), cols < rows)
    u = _later_matrix(tk)
    past = kc_ref.shape[1]
    for j in reversed(range(past // tk)):
        pv, c = _sb_step(qs, kc_ref[0, j * tk:(j + 1) * tk, :].astype(BF16),
                         vc_ref[0, j * tk:(j + 1) * tk, :].astype(BF16), u, c, None)
        acc = acc + pv
    lane = lax.broadcasted_iota(jnp.int32, (n_new, LANES), 1)
    o_ref[0] = jnp.where(lane < LANES // 2, acc[:n_new], acc[n_new:]).astype(o_ref.dtype)


def _sb_decode(q, kc, vc, k_new, v_new, tk):
    b, n, w = q.shape
    past = kc.shape[1]
    tk = min(tk, past)
    new = pl.BlockSpec((1, n, LANES), lambda i, g: (i, 0, g))
    cache = pl.BlockSpec((1, past, LANES), lambda i, g: (i, 0, g))
    return pl.pallas_call(
        functools.partial(_sb_decode_kernel, n_new=n, tk=tk),
        grid=(b, w // LANES),
        in_specs=[new, cache, cache, new, new],
        out_specs=new,
        out_shape=jax.ShapeDtypeStruct((b, n, w), BF16),
        compiler_params=_params(("parallel", "parallel")),
        name="sb_decode",
    )(q, kc, vc, k_new, v_new)


def _odd_out_kernel(o_ref, g_ref, x_ref, wout_ref, lng_ref, lnb_ref, y_ref):
    a = (o_ref[...].astype(F32) * g_ref[...].astype(F32)).astype(BF16)
    o = jnp.dot(a, wout_ref[...], preferred_element_type=F32)
    y_ref[...] = _layernorm(ALPHA * x_ref[...] + o, lng_ref[...], lnb_ref[...])


def _odd_out(o, gate, x2d, wout, lng, lnb, tm):
    t = x2d.shape[0]
    tm = min(tm, t)
    row = lambda w: pl.BlockSpec((tm, w), lambda i: (i, 0))
    return pl.pallas_call(
        _odd_out_kernel,
        grid=(t // tm,),
        in_specs=[row(SB_W), row(SB_W), row(D_MODEL), _const_spec(wout.shape),
                  _const_spec(lng.shape), _const_spec(lnb.shape)],
        out_specs=row(D_MODEL),
        out_shape=jax.ShapeDtypeStruct((t, D_MODEL), F32),
        compiler_params=_params(("parallel",)),
        name="odd_out",
    )(o, gate, x2d, wout, lng, lnb)


def _even_weights(w_in, q_g, w_uq, kv_g, w_uk):
    pad = jnp.zeros((D_MODEL, LANES - MLA_ROPE_DIM), w_in.dtype)
    o = np.cumsum([0, Q_LORA, KV_LORA, MLA_ROPE_DIM, MLA_W, DIFF_W, DIFF_W, DIFF_W, DIFF_W])
    w_re = jnp.concatenate([w_in[:, o[0]:o[3]], pad, w_in[:, o[3]:]], axis=1).astype(BF16)
    w_uq3 = w_uq.reshape(Q_LORA, MLA_HEADS, MLA_QK_DIM)
    wqn = w_uq3[:, :, :MLA_NOPE_DIM].reshape(Q_LORA, MLA_HEADS * MLA_NOPE_DIM).astype(BF16)
    wqr = jnp.pad(w_uq3[:, :, MLA_NOPE_DIM:], ((0, 0), (0, 0), (0, LANES - MLA_ROPE_DIM)))
    wqr = wqr.reshape(Q_LORA, MLA_HEADS * LANES).astype(BF16)
    eye = jnp.eye(MLA_HEADS, dtype=w_uk.dtype)
    wuk_bd = jnp.einsum('chn,hg->hngc', w_uk, eye).reshape(MLA_HEADS * MLA_NOPE_DIM, MLA_HEADS * KV_LORA).astype(BF16)
    return w_re, wqn, wqr, wuk_bd, q_g.reshape(1, Q_LORA), kv_g.reshape(1, KV_LORA)


def _uv_block_diag(w_uv):
    eye = jnp.eye(MLA_HEADS, dtype=w_uv.dtype)
    return jnp.einsum('chv,hg->hcgv', w_uv, eye).reshape(MLA_HEADS * KV_LORA, MLA_HEADS * MLA_V_DIM).astype(BF16)


TM_PROJ = 512
TQ_SOFTMAX = 128
TK_SOFTMAX = 512
TQ_SB = 256
SB_PAIRS = 4
TK_SB_DECODE = 256


def kernel(x_prompt, x_sample, cache_mla_ckv, cache_mla_krope, cache_diff_k, cache_diff_v, cache_sb_k, cache_sb_v,
           w_in_even, q_norm_g, w_uq, kv_norm_g, w_uk, w_uv, diff_lambda_q, diff_lambda_k, diff_subln_g,
           w_out_even, ln_even_g, ln_even_b, w_in_odd, w_out_odd, ln_odd_g, ln_odd_b):
    bp, sp, _ = x_prompt.shape
    bs, ss, _ = x_sample.shape
    past = cache_sb_k.shape[2]
    if past % CHUNK or ss > CHUNK or (ss & (ss - 1)) or sp % TQ_SB:
        raise NotImplementedError("decode frames must share one chunk after a chunk-aligned cache")
    tp, ts = bp * sp, bs * ss
    pos_p = jnp.arange(sp, dtype=jnp.int32)
    pos_s = past + jnp.arange(ss, dtype=jnp.int32)
    pos_s_rows = jnp.tile(pos_s, bs)

    def tables(pos):
        mc, msa, msb, _ = _rope_tables(pos, MLA_ROPE_DIM, MLA_ROPE_THETA, LANES)
        dc, dsa, dsb, _ = _rope_tables(pos, DIFF_ROT, ROPE_THETA, DIFF_HD)
        return mc, msa, msb, dc, dsa, dsb

    yp = x_prompt.reshape(tp, D_MODEL)
    ys = x_sample.reshape(ts, D_MODEL)
    even_p, even_s, odd_p, odd_s = [], [], [], []
    for layer in range(DEPTH):
        i = layer // 2
        if layer % 2 == 0:
            lam_init = 0.8 - 0.6 * math.exp(-0.3 * layer)
            wts = _even_weights(w_in_even[i], q_norm_g[i], w_uq[i], kv_norm_g[i], w_uk[i])
            wuv_bd = _uv_block_diag(w_uv[i])
            sg = diff_subln_g[i].reshape(1, 2 * DIFF_HD)
            wout = w_out_even[i].astype(BF16)
            lng, lnb = ln_even_g[i].reshape(1, D_MODEL), ln_even_b[i].reshape(1, D_MODEL)
            lq, lk = diff_lambda_q[i], diff_lambda_k[i]
            qcat, kcat, ckv, krope, gate, dq, dkf, dkb, dvf, dvb = _even_in(yp, wts, tables(pos_p), sp, TM_PROJ)
            olat = _mla_prompt(qcat.reshape(bp, sp, -1), kcat.reshape(bp, sp, -1), TQ_SOFTMAX, TK_SOFTMAX)
            odiff = _diff_prompt(dq.reshape(bp, sp, -1), dkb.reshape(bp, sp, -1), dvb.reshape(bp, sp, -1),
                                 lq, lk, lam_init, 2 * TQ_SOFTMAX, TK_SOFTMAX)
            yp_new = _even_out(olat.reshape(tp, -1), odiff.reshape(tp, -1), gate, yp, wuv_bd, sg, wout, lng, lnb,
                               lam_init, TM_PROJ)
            even_p.append((ckv.reshape(bp, sp, KV_LORA), krope.reshape(bp, sp, MLA_ROPE_DIM),
                           dkf.reshape(bp, sp, DIFF_HEADS, 2, DIFF_HD), dvf.reshape(bp, sp, DIFF_HEADS, 2 * DIFF_HD)))
            qcat, kcat, ckv, krope, gate, dq, dkf, dkb, dvf, dvb = _even_in(ys, wts, tables(pos_s_rows), ts, TM_PROJ)
            olat = _mla_decode(qcat.reshape(bs, ss, -1), cache_mla_ckv[i], cache_mla_krope[i],
                               kcat.reshape(bs, ss, -1))
            odiff = _diff_decode(dq.reshape(bs, ss, -1), cache_diff_k[i].reshape(bs, past, DIFF_W),
                                 cache_diff_v[i].reshape(bs, past, DIFF_W), dkb.reshape(bs, ss, -1),
                                 dvb.reshape(bs, ss, -1), lq, lk, lam_init)
            ys_new = _even_out(olat.reshape(ts, -1), odiff.reshape(ts, -1), gate, ys, wuv_bd, sg, wout, lng, lnb,
                               lam_init, TM_PROJ)
            even_s.append((ckv.reshape(bs, ss, KV_LORA), krope.reshape(bs, ss, MLA_ROPE_DIM),
                           dkf.reshape(bs, ss, DIFF_HEADS, 2, DIFF_HD), dvf.reshape(bs, ss, DIFF_HEADS, 2 * DIFF_HD)))
            yp, ys = yp_new, ys_new
        else:
            w_in = w_in_odd[i].astype(BF16)
            wout = w_out_odd[i].astype(BF16)
            lng, lnb = ln_odd_g[i].reshape(1, D_MODEL), ln_odd_b[i].reshape(1, D_MODEL)
            q, kf, kb, vf, vb, gate = _odd_in(yp, w_in, TM_PROJ)
            o = _sb_prompt(q.reshape(bp, sp, -1), kb.reshape(bp, sp, -1), vb.reshape(bp, sp, -1), TQ_SB, SB_PAIRS)
            yp_new = _odd_out(o.reshape(tp, -1), gate, yp, wout, lng, lnb, TM_PROJ)
            odd_p.append((kf.reshape(bp, sp, SB_HEADS, SB_HD), vf.reshape(bp, sp, SB_HEADS, SB_HD)))
            q, kf, kb, vf, vb, gate = _odd_in(ys, w_in, TM_PROJ)
            o = _sb_decode(q.reshape(bs, ss, -1), cache_sb_k[i].reshape(bs, past, SB_W),
                           cache_sb_v[i].reshape(bs, past, SB_W), kb.reshape(bs, ss, -1), vb.reshape(bs, ss, -1),
                           TK_SB_DECODE)
            ys_new = _odd_out(o.reshape(ts, -1), gate, ys, wout, lng, lnb, TM_PROJ)
            odd_s.append((kf.reshape(bs, ss, SB_HEADS, SB_HD), vf.reshape(bs, ss, SB_HEADS, SB_HD)))
            yp, ys = yp_new, ys_new
    stack = lambda lst, k: jnp.stack([t[k] for t in lst])
    return (yp.reshape(bp, sp, D_MODEL), ys.reshape(bs, ss, D_MODEL),
            stack(even_p, 0), stack(even_p, 1), stack(even_p, 2), stack(even_p, 3),
            stack(odd_p, 0), stack(odd_p, 1),
            stack(even_s, 0), stack(even_s, 1), stack(even_s, 2), stack(even_s, 3),
            stack(odd_s, 0), stack(odd_s, 1))
```

```python
import functools
import math

import numpy as np
import jax
import jax.numpy as jnp
from jax import lax
from jax.experimental import pallas as pl
from jax.experimental.pallas import tpu as pltpu

F32 = jnp.float32
BF16 = jnp.bfloat16

D_MODEL = 1024
DEPTH = 2
CHUNK = 64
ALPHA = (2 * DEPTH) ** 0.25
EPS = 1e-5
NEG_INF = -1e30
MLA_HEADS = 8
Q_LORA = 256
KV_LORA = 128
MLA_NOPE_DIM = 64
MLA_ROPE_DIM = 32
MLA_QK_DIM = MLA_NOPE_DIM + MLA_ROPE_DIM
MLA_V_DIM = 64
MLA_ROPE_THETA = 10000.0
MLA_W = MLA_HEADS * MLA_V_DIM
DIFF_HEADS = 4
DIFF_HD = 64
DIFF_ROT = DIFF_HD // 4
ROPE_THETA = 500000.0
DIFF_W = DIFF_HEADS * 2 * DIFF_HD
SB_HEADS = 16
SB_HD = 64
SB_W = SB_HEADS * SB_HD

LANES = 128
MLA_QCAT = 2 * LANES
VMEM_LIMIT = 56 * 1024 * 1024

_NT = (((1,), (1,)), ((), ()))
LOG2E = math.log2(math.e)


def _params(sem):
    return pltpu.CompilerParams(dimension_semantics=sem, vmem_limit_bytes=VMEM_LIMIT)


def _const_spec(shape):
    nd = len(shape)
    return pl.BlockSpec(shape, lambda *_: (0,) * nd, pipeline_mode=pl.Buffered(1))


def _rope_tables(pos, rot_dim, theta, period):
    half = rot_dim // 2
    inv = jnp.exp(-math.log(theta) * jnp.arange(half, dtype=F32) * (2.0 / rot_dim))
    ang = pos.astype(F32)[:, None] * inv[None, :]
    cos, sin = jnp.cos(ang), jnp.sin(ang)
    lane = np.arange(LANES) % period
    idx = lane % half
    in_rot = lane < rot_dim
    first = lane < half
    second = in_rot & ~first
    cos_l, sin_l = cos[:, idx], sin[:, idx]
    c = jnp.where(in_rot[None, :], cos_l, 1.0)
    sa = jnp.where(first[None, :], -sin_l, 0.0)
    sb = jnp.where(second[None, :], sin_l, 0.0)
    return c, sa, sb, half


def _apply_rope(x, c, sa, sb, half):
    return x * c + pltpu.roll(x, LANES - half, 1) * sa + pltpu.roll(x, half, 1) * sb


def _silu(x):
    return x / (1.0 + jnp.exp(-x))


def _rms(x, g):
    ms = jnp.mean(x * x, axis=-1, keepdims=True)
    return x * lax.rsqrt(ms + EPS) * g


def _layernorm(r, g, b):
    mu = jnp.mean(r, axis=-1, keepdims=True)
    rc = r - mu
    var = jnp.mean(rc * rc, axis=-1, keepdims=True)
    return rc * lax.rsqrt(var + EPS) * g + b


_C_QLAT = 0
_C_CKV = Q_LORA
_C_KROPE = _C_CKV + KV_LORA
_C_GMLA = _C_KROPE + LANES
_C_DQ = _C_GMLA + MLA_W
_C_DK = _C_DQ + DIFF_W
_C_DV = _C_DK + DIFF_W
_C_GDIFF = _C_DV + DIFF_W
_EVEN_COLS = _C_GDIFF + DIFF_W


def _even_in_kernel(x_ref, w_ref, wqn_ref, wqr_ref, wuk_ref, qg_ref, kvg_ref,
                    mc_ref, msa_ref, msb_ref, dc_ref, dsa_ref, dsb_ref,
                    qcat_ref, kcat_ref, ckv_ref, krope_ref, gate_ref, dq_ref,
                    dkf_ref, dkb_ref, dvf_ref, dvb_ref):
    xb = x_ref[...].astype(BF16)

    def proj(c0, c1):
        return jnp.dot(xb, w_ref[:, c0:c1], preferred_element_type=F32)

    mla_half = MLA_ROPE_DIM // 2
    diff_half = DIFF_ROT // 2
    mc, msa, msb = mc_ref[...], msa_ref[...], msb_ref[...]
    dc, dsa, dsb = dc_ref[...], dsa_ref[...], dsb_ref[...]

    qn = _rms(proj(_C_QLAT, _C_QLAT + Q_LORA), qg_ref[...]).astype(BF16)
    q_nope = jnp.dot(qn, wqn_ref[...], preferred_element_type=F32).astype(BF16)
    q_abs = jnp.dot(q_nope, wuk_ref[...], preferred_element_type=F32)
    q_rp = jnp.dot(qn, wqr_ref[...], preferred_element_type=F32)
    scale = MLA_QK_DIM ** -0.5 * LOG2E
    for h in range(MLA_HEADS):
        rot = _apply_rope(q_rp[:, h * LANES:(h + 1) * LANES], mc, msa, msb, mla_half)
        qcat_ref[:, h * MLA_QCAT:h * MLA_QCAT + LANES] = (q_abs[:, h * LANES:(h + 1) * LANES] * scale).astype(BF16)
        qcat_ref[:, h * MLA_QCAT + LANES:(h + 1) * MLA_QCAT] = (rot * scale).astype(BF16)

    kv = proj(_C_CKV, _C_GMLA)
    ckv = _rms(kv[:, :KV_LORA], kvg_ref[...])
    krope = _apply_rope(kv[:, KV_LORA:], mc, msa, msb, mla_half)
    ckv_ref[...] = ckv
    krope_ref[...] = krope[:, :MLA_ROPE_DIM]
    kcat_ref[:, :LANES] = ckv.astype(BF16)
    kcat_ref[:, LANES:] = krope.astype(BF16)

    gate_ref[:, :MLA_W] = _silu(proj(_C_GMLA, _C_DQ)).astype(BF16)
    gate_ref[:, MLA_W:] = _silu(proj(_C_GDIFF, _EVEN_COLS)).astype(BF16)

    dq = proj(_C_DQ, _C_DK)
    dk = proj(_C_DK, _C_DV)
    dv = proj(_C_DV, _C_GDIFF)
    dscale = DIFF_HD ** -0.5 * LOG2E
    for h in range(DIFF_HEADS):
        sl = slice(h * LANES, (h + 1) * LANES)
        dq_ref[:, sl] = (_apply_rope(dq[:, sl], dc, dsa, dsb, diff_half) * dscale).astype(BF16)
        kr = _apply_rope(dk[:, sl], dc, dsa, dsb, diff_half)
        dkf_ref[:, sl] = kr
        dkb_ref[:, sl] = kr.astype(BF16)
    dvf_ref[...] = dv
    dvb_ref[...] = dv.astype(BF16)


def _even_in(x2d, wts, tabs, tab_rows_per_rep, tm):
    t = x2d.shape[0]
    tm = min(tm, t)
    nrep = tab_rows_per_rep // tm
    w_in, wqn, wqr, wuk, qg, kvg = wts
    row = lambda w: pl.BlockSpec((tm, w), lambda i: (i, 0))
    tab = pl.BlockSpec((tm, LANES), lambda i: (i % nrep, 0))
    outs = [
        (MLA_HEADS * MLA_QCAT, BF16),
        (MLA_QCAT, BF16),
        (KV_LORA, F32),
        (MLA_ROPE_DIM, F32),
        (MLA_W + DIFF_W, BF16),
        (DIFF_W, BF16),
        (DIFF_W, F32), (DIFF_W, BF16),
        (DIFF_W, F32), (DIFF_W, BF16),
    ]
    return pl.pallas_call(
        _even_in_kernel,
        grid=(t // tm,),
        in_specs=[row(D_MODEL), _const_spec(w_in.shape), _const_spec(wqn.shape), _const_spec(wqr.shape),
                  _const_spec(wuk.shape), _const_spec(qg.shape), _const_spec(kvg.shape)] + [tab] * 6,
        out_specs=[row(w) for w, _ in outs],
        out_shape=[jax.ShapeDtypeStruct((t, w), d) for w, d in outs],
        compiler_params=_params(("parallel",)),
        name="even_in",
    )(x2d, w_in, wqn, wqr, wuk, qg, kvg, *tabs)


def _softmax_step(s, vt, m_ref, l_ref, acc_ref, valid):
    if valid is not None:
        s = jnp.where(valid, s, NEG_INF)
    m_old = m_ref[...]
    m_new = jnp.maximum(m_old, jnp.max(s, axis=0, keepdims=True))
    alpha = jnp.exp2(m_old - m_new)
    p = jnp.exp2(s - m_new)
    l_ref[...] = alpha * l_ref[...] + jnp.sum(p, axis=0, keepdims=True)
    acc_ref[...] = alpha * acc_ref[...] + jnp.dot(vt, p.astype(BF16), preferred_element_type=F32)
    m_ref[...] = m_new


def _chunk_causal_flash(qs, get_k, get_vt, q0, tq, tk, s_ref, m_ref, l_ref, acc_ref):
    m_rows = qs.shape[0]
    m_ref[...] = jnp.full(m_ref.shape, NEG_INF, F32)
    l_ref[...] = jnp.zeros(l_ref.shape, F32)
    acc_ref[...] = jnp.zeros(acc_ref.shape, F32)
    n_full = q0 // tk

    def scores(j):
        return lax.dot_general(get_k(pl.multiple_of(j * tk, tk)), qs, _NT, preferred_element_type=F32)

    sa_ref, sb_ref = s_ref
    sa_ref[...] = scores(0)

    def step(j, cur_ref, nxt_ref):
        nxt_ref[...] = scores(j + 1)
        _softmax_step(cur_ref[...], get_vt(j), m_ref, l_ref, acc_ref, None)

    def body(jj, carry):
        step(2 * jj, sa_ref, sb_ref)
        step(2 * jj + 1, sb_ref, sa_ref)
        return carry

    lax.fori_loop(0, n_full // 2, body, 0)
    kpos = n_full * tk + lax.broadcasted_iota(jnp.int32, (tk, m_rows), 0)
    qpos = q0 + (lax.broadcasted_iota(jnp.int32, (tk, m_rows), 1) & (tq - 1))
    shift = CHUNK.bit_length() - 1
    valid = (kpos >> shift) <= (qpos >> shift)

    @pl.when(n_full % 2 == 1)
    def _():
        step(n_full - 1, sa_ref, sb_ref)
        _softmax_step(sb_ref[...], get_vt(n_full), m_ref, l_ref, acc_ref, valid)

    @pl.when(n_full % 2 == 0)
    def _():
        _softmax_step(sa_ref[...], get_vt(n_full), m_ref, l_ref, acc_ref, valid)


def _mla_prompt_kernel(q_ref, k_ref, vt_ref, o_ref, sa_ref, sb_ref, m_ref, l_ref, acc_ref, *, tq, tk):
    s_ref = (sa_ref, sb_ref)
    q0 = pl.program_id(1) * tq
    qs = jnp.concatenate([q_ref[0, :, h * MLA_QCAT:(h + 1) * MLA_QCAT] for h in range(MLA_HEADS)], axis=0)
    get_k = lambda k0: k_ref[0, pl.ds(k0, tk), :]
    get_vt = lambda j: vt_ref[0, j]
    _chunk_causal_flash(qs, get_k, get_vt, q0, tq, tk, s_ref, m_ref, l_ref, acc_ref)
    out = acc_ref[...] / l_ref[...]
    for h in range(MLA_HEADS):
        o_ref[0, :, h * KV_LORA:(h + 1) * KV_LORA] = out[:, h * tq:(h + 1) * tq].T.astype(o_ref.dtype)


def _key_major(x, tk):
    b, s, w = x.shape
    return x.reshape(b, s // tk, tk, w).transpose(0, 1, 3, 2)


def _mla_prompt(qcat, kcat, tq, tk):
    b, s, _ = qcat.shape
    tq, tk = min(tq, s), min(tk, s)
    m_rows = MLA_HEADS * tq
    vt = _key_major(kcat[:, :, :KV_LORA], tk)
    return pl.pallas_call(
        functools.partial(_mla_prompt_kernel, tq=tq, tk=tk),
        grid=(b, s // tq),
        in_specs=[pl.BlockSpec((1, tq, MLA_HEADS * MLA_QCAT), lambda bi, i: (bi, i, 0)),
                  pl.BlockSpec((1, s, MLA_QCAT), lambda bi, i: (bi, 0, 0)),
                  pl.BlockSpec((1, s // tk, KV_LORA, tk), lambda bi, i: (bi, 0, 0, 0))],
        out_specs=pl.BlockSpec((1, tq, MLA_HEADS * KV_LORA), lambda bi, i: (bi, i, 0)),
        out_shape=jax.ShapeDtypeStruct((b, s, MLA_HEADS * KV_LORA), BF16),
        scratch_shapes=[pltpu.VMEM((tk, m_rows), F32), pltpu.VMEM((tk, m_rows), F32), pltpu.VMEM((1, m_rows), F32),
                        pltpu.VMEM((1, m_rows), F32), pltpu.VMEM((KV_LORA, m_rows), F32)],
        compiler_params=_params(("parallel", "parallel")),
        name="mla_prompt",
    )(qcat, kcat, vt)


def _lane_halves(x):
    lane = lax.broadcasted_iota(jnp.int32, x.shape, 1)
    zero = jnp.zeros_like(x)
    return jnp.concatenate([jnp.where(lane < LANES // 2, x, zero), jnp.where(lane < LANES // 2, zero, x)], axis=0)


def _diff_lambda(lq_ref, lk_ref, lam_init):
    e = jnp.exp(jnp.sum(lq_ref[...] * lk_ref[...], axis=-1, keepdims=True))
    return e[0:1] - e[1:2] + lam_init


def _diff_prompt_kernel(q_ref, k_ref, vt_ref, lq_ref, lk_ref, o_ref, sa_ref, sb_ref, m_ref, l_ref, acc_ref,
                        *, tq, tk, lam_init):
    s_ref = (sa_ref, sb_ref)
    q0 = pl.program_id(1) * tq
    lam = _diff_lambda(lq_ref, lk_ref, lam_init)
    for h in range(DIFF_HEADS):
        sl = slice(h * LANES, (h + 1) * LANES)
        qs = _lane_halves(q_ref[0, :, sl])
        get_k = lambda k0, sl=sl: k_ref[0, pl.ds(k0, tk), sl]
        get_vt = lambda j, sl=sl: vt_ref[0, j, sl, :]
        _chunk_causal_flash(qs, get_k, get_vt, q0, tq, tk, s_ref, m_ref, l_ref, acc_ref)
        o = acc_ref[...] / l_ref[...]
        o_ref[0, :, sl] = (o[:, :tq] - lam * o[:, tq:]).T


def _diff_prompt(dq, dk, dv, lq, lk, lam_init, tq, tk):
    b, s, _ = dq.shape
    tq, tk = min(tq, s), min(tk, s)
    qspec = pl.BlockSpec((1, tq, DIFF_W), lambda bi, i: (bi, i, 0))
    return pl.pallas_call(
        functools.partial(_diff_prompt_kernel, tq=tq, tk=tk, lam_init=lam_init),
        grid=(b, s // tq),
        in_specs=[qspec, pl.BlockSpec((1, s, DIFF_W), lambda bi, i: (bi, 0, 0)),
                  pl.BlockSpec((1, s // tk, DIFF_W, tk), lambda bi, i: (bi, 0, 0, 0)),
                  _const_spec(lq.shape), _const_spec(lk.shape)],
        out_specs=qspec,
        out_shape=jax.ShapeDtypeStruct((b, s, DIFF_W), F32),
        scratch_shapes=[pltpu.VMEM((tk, 2 * tq), F32), pltpu.VMEM((tk, 2 * tq), F32), pltpu.VMEM((1, 2 * tq), F32),
                        pltpu.VMEM((1, 2 * tq), F32), pltpu.VMEM((LANES, 2 * tq), F32)],
        compiler_params=_params(("parallel", "parallel")),
        name="diff_prompt",
    )(dq, dk, _key_major(dv, tk), lq, lk)


def _pad_rows(x, rows):
    return jnp.concatenate([x, jnp.zeros((rows - x.shape[0], x.shape[1]), x.dtype)], axis=0)


def _decode_softmax(s_c, s_n, v_cache, v_new, n_new):
    col = lax.broadcasted_iota(jnp.int32, s_n.shape, 1)
    s_n = jnp.where(col < n_new, s_n, NEG_INF)
    m = jnp.maximum(jnp.max(s_c, axis=-1, keepdims=True), jnp.max(s_n, axis=-1, keepdims=True))
    p_c = jnp.exp2(s_c - m)
    p_n = jnp.exp2(s_n - m)
    l = jnp.sum(p_c, axis=-1, keepdims=True) + jnp.sum(p_n, axis=-1, keepdims=True)
    acc = (jnp.dot(p_c.astype(BF16), v_cache, preferred_element_type=F32)
           + jnp.dot(p_n.astype(BF16), v_new, preferred_element_type=F32))
    return acc / l


def _mla_decode_kernel(q_ref, ckv_ref, kr_ref, kn_ref, o_ref, *, n_new):
    qs = jnp.concatenate([q_ref[0, :, h * MLA_QCAT:(h + 1) * MLA_QCAT] for h in range(MLA_HEADS)], axis=0)
    ckv = ckv_ref[0].astype(BF16)
    kr = kr_ref[0].astype(BF16)
    k_new = _pad_rows(kn_ref[0], LANES)
    s_c = (lax.dot_general(qs[:, :KV_LORA], ckv, _NT, preferred_element_type=F32)
           + lax.dot_general(qs[:, KV_LORA:KV_LORA + MLA_ROPE_DIM], kr, _NT, preferred_element_type=F32))
    s_n = lax.dot_general(qs, k_new, _NT, preferred_element_type=F32)
    out = _decode_softmax(s_c, s_n, ckv, k_new[:, :KV_LORA], n_new)
    for h in range(MLA_HEADS):
        o_ref[0, :, h * KV_LORA:(h + 1) * KV_LORA] = out[h * n_new:(h + 1) * n_new].astype(o_ref.dtype)


def _mla_decode(qcat, ckv_c, kr_c, kcat_new):
    b, n, _ = qcat.shape
    past = ckv_c.shape[1]
    return pl.pallas_call(
        functools.partial(_mla_decode_kernel, n_new=n),
        grid=(b,),
        in_specs=[pl.BlockSpec((1, n, MLA_HEADS * MLA_QCAT), lambda i: (i, 0, 0)),
                  pl.BlockSpec((1, past, KV_LORA), lambda i: (i, 0, 0)),
                  pl.BlockSpec((1, past, MLA_ROPE_DIM), lambda i: (i, 0, 0)),
                  pl.BlockSpec((1, n, MLA_QCAT), lambda i: (i, 0, 0))],
        out_specs=pl.BlockSpec((1, n, MLA_HEADS * KV_LORA), lambda i: (i, 0, 0)),
        out_shape=jax.ShapeDtypeStruct((b, n, MLA_HEADS * KV_LORA), BF16),
        compiler_params=_params(("parallel",)),
        name="mla_decode",
    )(qcat, ckv_c, kr_c, kcat_new)


def _diff_decode_kernel(q_ref, kc_ref, vc_ref, kn_ref, vn_ref, lq_ref, lk_ref, o_ref, *, n_new, lam_init):
    lam = _diff_lambda(lq_ref, lk_ref, lam_init)
    qs = _lane_halves(q_ref[0])
    s_c = lax.dot_general(qs, kc_ref[0].astype(BF16), _NT, preferred_element_type=F32)
    s_n = lax.dot_general(qs, _pad_rows(kn_ref[0], LANES), _NT, preferred_element_type=F32)
    o = _decode_softmax(s_c, s_n, vc_ref[0].astype(BF16), _pad_rows(vn_ref[0], LANES), n_new)
    o_ref[0] = o[:n_new] - lam * o[n_new:]


def _diff_decode(dq, kc, vc, dk_new, dv_new, lq, lk, lam_init):
    b, n, _ = dq.shape
    past = kc.shape[1]
    new = pl.BlockSpec((1, n, LANES), lambda i, h: (i, 0, h))
    cache = pl.BlockSpec((1, past, LANES), lambda i, h: (i, 0, h))
    return pl.pallas_call(
        functools.partial(_diff_decode_kernel, n_new=n, lam_init=lam_init),
        grid=(b, DIFF_HEADS),
        in_specs=[new, cache, cache, new, new, _const_spec(lq.shape), _const_spec(lk.shape)],
        out_specs=new,
        out_shape=jax.ShapeDtypeStruct((b, n, DIFF_W), F32),
        compiler_params=_params(("parallel", "parallel")),
        name="diff_decode",
    )(dq, kc, vc, dk_new, dv_new, lq, lk)


def _even_out_kernel(olat_ref, odiff_ref, gate_ref, x_ref, wuv_ref, sg_ref, wout_ref, lng_ref, lnb_ref, y_ref,
                     *, lam_init):
    gate = gate_ref[...]
    o_mla = jnp.dot(olat_ref[...], wuv_ref[...], preferred_element_type=F32)
    parts = [(o_mla * gate[:, :MLA_W].astype(F32)).astype(BF16)]
    od = odiff_ref[...]
    for h in range(DIFF_HEADS):
        sl = slice(h * LANES, (h + 1) * LANES)
        n = _rms(od[:, sl], sg_ref[...]) * (1.0 - lam_init)
        parts.append((n * gate[:, MLA_W + h * LANES:MLA_W + (h + 1) * LANES].astype(F32)).astype(BF16))
    o = jnp.dot(jnp.concatenate(parts, axis=1), wout_ref[...], preferred_element_type=F32)
    y_ref[...] = _layernorm(ALPHA * x_ref[...] + o, lng_ref[...], lnb_ref[...])


def _even_out(olat, odiff, gate, x2d, wuv_bd, sg, wout, lng, lnb, lam_init, tm):
    t = x2d.shape[0]
    tm = min(tm, t)
    row = lambda w: pl.BlockSpec((tm, w), lambda i: (i, 0))
    return pl.pallas_call(
        functools.partial(_even_out_kernel, lam_init=lam_init),
        grid=(t // tm,),
        in_specs=[row(MLA_HEADS * KV_LORA), row(DIFF_W), row(MLA_W + DIFF_W), row(D_MODEL),
                  _const_spec(wuv_bd.shape), _const_spec(sg.shape), _const_spec(wout.shape),
                  _const_spec(lng.shape), _const_spec(lnb.shape)],
        out_specs=row(D_MODEL),
        out_shape=jax.ShapeDtypeStruct((t, D_MODEL), F32),
        compiler_params=_params(("parallel",)),
        name="even_out",
    )(olat, odiff, gate, x2d, wuv_bd, sg, wout, lng, lnb)


def _odd_in_kernel(x_ref, w_ref, q_ref, kf_ref, kb_ref, vf_ref, vb_ref, g_ref):
    xb = x_ref[...].astype(BF16)

    def proj(i):
        return jnp.dot(xb, w_ref[:, i * SB_W:(i + 1) * SB_W], preferred_element_type=F32)

    q_ref[...] = (proj(0) * SB_HD ** -0.5).astype(BF16)
    k = proj(1)
    kf_ref[...] = k
    kb_ref[...] = k.astype(BF16)
    v = proj(2)
    vf_ref[...] = v
    vb_ref[...] = v.astype(BF16)
    g_ref[...] = _silu(proj(3)).astype(BF16)


def _odd_in(x2d, w_in, tm):
    t = x2d.shape[0]
    tm = min(tm, t)
    row = pl.BlockSpec((tm, SB_W), lambda i: (i, 0))
    dts = [BF16, F32, BF16, F32, BF16, BF16]
    return pl.pallas_call(
        _odd_in_kernel,
        grid=(t // tm,),
        in_specs=[pl.BlockSpec((tm, D_MODEL), lambda i: (i, 0)), _const_spec(w_in.shape)],
        out_specs=[row] * len(dts),
        out_shape=[jax.ShapeDtypeStruct((t, SB_W), d) for d in dts],
        compiler_params=_params(("parallel",)),
        name="odd_in",
    )(x2d, w_in)


def _sb_step(qs, kb, vb, u, c, valid):
    z = lax.dot_general(qs, kb, _NT, preferred_element_type=F32)
    lg = jnp.log(1.0 + jnp.exp(-jnp.abs(z)))
    sp = jnp.maximum(z, 0.0) + lg
    lb = jnp.minimum(z, 0.0) - lg
    if valid is not None:
        sp = jnp.where(valid, sp, 0.0)
    spb = sp.astype(BF16)
    later = jnp.dot(spb, u, preferred_element_type=F32)
    a = jnp.exp(lb - later - c)
    if valid is not None:
        a = jnp.where(valid, a, 0.0)
    pv = jnp.dot(a.astype(BF16), vb, preferred_element_type=F32)
    c_new = c + later[:, :1] + spb[:, :1].astype(F32)
    return pv, c_new


def _later_matrix(n):
    return (lax.broadcasted_iota(jnp.int32, (n, n), 0) > lax.broadcasted_iota(jnp.int32, (n, n), 1)).astype(BF16)


def _sb_block_t(qs, kb, vt, ut, valid):
    z = lax.dot_general(kb, qs, _NT, preferred_element_type=F32)
    mx = jnp.maximum(z, 0.0)
    mn = jnp.minimum(z, 0.0)
    lg = jnp.log(1.0 + jnp.exp(mn - mx))
    sp = mx + lg
    if valid is not None:
        sp = jnp.where(valid, sp, 0.0)
    spb = sp.astype(BF16)
    later = jnp.dot(ut, spb, preferred_element_type=F32)
    a = jnp.exp(mn - lg - later)
    if valid is not None:
        a = jnp.where(valid, a, 0.0)
    pv = jnp.dot(vt, a.astype(BF16), preferred_element_type=F32)
    return pv, later[:1, :] + spb[:1, :].astype(F32)


def _sb_prompt_kernel(q_ref, k_ref, vt_ref, o_ref, c_ref, acc_ref, *, tq, pairs):
    i = pl.program_id(2)
    ut = (lax.broadcasted_iota(jnp.int32, (tq, tq), 1) > lax.broadcasted_iota(jnp.int32, (tq, tq), 0)).astype(BF16)
    keys = lax.broadcasted_iota(jnp.int32, (tq, 2 * tq), 0)
    queries = lax.broadcasted_iota(jnp.int32, (tq, 2 * tq), 1) & (tq - 1)
    diag_valid = keys < queries
    half = LANES // 2
    for p in range(pairs):
        sl = slice(p * LANES, (p + 1) * LANES)
        qs = _lane_halves(q_ref[0, :, sl])

        def block(j, valid, sl=sl, qs=qs):
            return _sb_block_t(qs, k_ref[0, pl.ds(pl.multiple_of(j * tq, tq), tq), sl], vt_ref[0, j, sl, :], ut, valid)

        pv, tot = block(i, diag_valid)
        acc_ref[...] = pv
        c_ref[...] = tot

        def two_blocks(jj, carry, block=block):
            j = i - 1 - 2 * jj
            pv1, tot1 = block(j, None)
            pv2, tot2 = block(j - 1, None)
            c = c_ref[...]
            acc_ref[...] += pv1 * jnp.exp(-c) + pv2 * jnp.exp(-(c + tot1))
            c_ref[...] = c + tot1 + tot2
            return carry

        lax.fori_loop(0, i // 2, two_blocks, 0)

        @pl.when(i % 2 == 1)
        def _(block=block):
            pv, _ = block(0, None)
            acc_ref[...] += pv * jnp.exp(-c_ref[...])

        acc = acc_ref[...]
        o_t = jnp.concatenate([acc[:half, :tq], acc[half:, tq:]], axis=0)
        o_ref[0, :, sl] = o_t.T.astype(o_ref.dtype)


def _sb_prompt(q, k, v, tq, pairs):
    b, s, w = q.shape
    tq = min(tq, s)
    gw = pairs * LANES
    qspec = pl.BlockSpec((1, tq, gw), lambda bi, g, i: (bi, i, g))
    return pl.pallas_call(
        functools.partial(_sb_prompt_kernel, tq=tq, pairs=pairs),
        grid=(b, w // gw, s // tq),
        in_specs=[qspec, pl.BlockSpec((1, s, gw), lambda bi, g, i: (bi, 0, g)),
                  pl.BlockSpec((1, s // tq, gw, tq), lambda bi, g, i: (bi, 0, g, 0))],
        out_specs=qspec,
        out_shape=jax.ShapeDtypeStruct((b, s, w), BF16),
        scratch_shapes=[pltpu.VMEM((1, 2 * tq), F32), pltpu.VMEM((LANES, 2 * tq), F32)],
        compiler_params=_params(("parallel", "parallel", "parallel")),
        name="sb_prompt",
    )(q, k, _key_major(v, tq))


def _sb_decode_kernel(q_ref, kc_ref, vc_ref, kn_ref, vn_ref, o_ref, *, n_new, tk):
    qs = _lane_halves(q_ref[0])
    rows = lax.broadcasted_iota(jnp.int32, (2 * n_new, LANES), 0) & (n_new - 1)
    cols = lax.broadcasted_iota(jnp.int32, (2 * n_new, LANES), 1)
    acc, c = _sb_step(qs, _pad_rows(kn_ref[0], LANES), _pad_rows(vn_ref[0], LANES), _later_matrix(LANES),
                      jnp.zeros((2 * n_new, 1), F32), cols < rows)
    u = _later_matrix(tk)
    past = kc_ref.shape[1]
    for j in reversed(range(past // tk)):
        pv, c = _sb_step(qs, kc_ref[0, j * tk:(j + 1) * tk, :].astype(BF16),
                         vc_ref[0, j * tk:(j + 1) * tk, :].astype(BF16), u, c, None)
        acc = acc + pv
    lane = lax.broadcasted_iota(jnp.int32, (n_new, LANES), 1)
    o_ref[0] = jnp.where(lane < LANES // 2, acc[:n_new], acc[n_new:]).astype(o_ref.dtype)


def _sb_decode(q, kc, vc, k_new, v_new, tk):
    b, n, w = q.shape
    past = kc.shape[1]
    tk = min(tk, past)
    new = pl.BlockSpec((1, n, LANES), lambda i, g: (i, 0, g))
    cache = pl.BlockSpec((1, past, LANES), lambda i, g: (i, 0, g))
    return pl.pallas_call(
        functools.partial(_sb_decode_kernel, n_new=n, tk=tk),
        grid=(b, w // LANES),
        in_specs=[new, cache, cache, new, new],
        out_specs=new,
        out_shape=jax.ShapeDtypeStruct((b, n, w), BF16),
        compiler_params=_params(("parallel", "parallel")),
        name="sb_decode",
    )(q, kc, vc, k_new, v_new)


def _odd_out_kernel(o_ref, g_ref, x_ref, wout_ref, lng_ref, lnb_ref, y_ref):
    a = (o_ref[...].astype(F32) * g_ref[...].astype(F32)).astype(BF16)
    o = jnp.dot(a, wout_ref[...], preferred_element_type=F32)
    y_ref[...] = _layernorm(ALPHA * x_ref[...] + o, lng_ref[...], lnb_ref[...])


def _odd_out(o, gate, x2d, wout, lng, lnb, tm):
    t = x2d.shape[0]
    tm = min(tm, t)
    row = lambda w: pl.BlockSpec((tm, w), lambda i: (i, 0))
    return pl.pallas_call(
        _odd_out_kernel,
        grid=(t // tm,),
        in_specs=[row(SB_W), row(SB_W), row(D_MODEL), _const_spec(wout.shape),
                  _const_spec(lng.shape), _const_spec(lnb.shape)],
        out_specs=row(D_MODEL),
        out_shape=jax.ShapeDtypeStruct((t, D_MODEL), F32),
        compiler_params=_params(("parallel",)),
        name="odd_out",
    )(o, gate, x2d, wout, lng, lnb)


def _even_weights(w_in, q_g, w_uq, kv_g, w_uk):
    pad = jnp.zeros((D_MODEL, LANES - MLA_ROPE_DIM), w_in.dtype)
    o = np.cumsum([0, Q_LORA, KV_LORA, MLA_ROPE_DIM, MLA_W, DIFF_W, DIFF_W, DIFF_W, DIFF_W])
    w_re = jnp.concatenate([w_in[:, o[0]:o[3]], pad, w_in[:, o[3]:]], axis=1).astype(BF16)
    w_uq3 = w_uq.reshape(Q_LORA, MLA_HEADS, MLA_QK_DIM)
    wqn = w_uq3[:, :, :MLA_NOPE_DIM].reshape(Q_LORA, MLA_HEADS * MLA_NOPE_DIM).astype(BF16)
    wqr = jnp.pad(w_uq3[:, :, MLA_NOPE_DIM:], ((0, 0), (0, 0), (0, LANES - MLA_ROPE_DIM)))
    wqr = wqr.reshape(Q_LORA, MLA_HEADS * LANES).astype(BF16)
    eye = jnp.eye(MLA_HEADS, dtype=w_uk.dtype)
    wuk_bd = jnp.einsum('chn,hg->hngc', w_uk, eye).reshape(MLA_HEADS * MLA_NOPE_DIM, MLA_HEADS * KV_LORA).astype(BF16)
    return w_re, wqn, wqr, wuk_bd, q_g.reshape(1, Q_LORA), kv_g.reshape(1, KV_LORA)


def _uv_block_diag(w_uv):
    eye = jnp.eye(MLA_HEADS, dtype=w_uv.dtype)
    return jnp.einsum('chv,hg->hcgv', w_uv, eye).reshape(MLA_HEADS * KV_LORA, MLA_HEADS * MLA_V_DIM).astype(BF16)


TM_PROJ = 512
TQ_SOFTMAX = 128
TK_SOFTMAX = 512
TQ_SB = 256
SB_PAIRS = 4
TK_SB_DECODE = 256


def kernel(x_prompt, x_sample, cache_mla_ckv, cache_mla_krope, cache_diff_k, cache_diff_v, cache_sb_k, cache_sb_v,
           w_in_even, q_norm_g, w_uq, kv_norm_g, w_uk, w_uv, diff_lambda_q, diff_lambda_k, diff_subln_g,
           w_out_even, ln_even_g, ln_even_b, w_in_odd, w_out_odd, ln_odd_g, ln_odd_b):
    bp, sp, _ = x_prompt.shape
    bs, ss, _ = x_sample.shape
    past = cache_sb_k.shape[2]
    if past % CHUNK or ss > CHUNK or (ss & (ss - 1)) or sp % TQ_SB:
        raise NotImplementedError("decode frames must share one chunk after a chunk-aligned cache")
    tp, ts = bp * sp, bs * ss
    pos_p = jnp.arange(sp, dtype=jnp.int32)
    pos_s = past + jnp.arange(ss, dtype=jnp.int32)
    pos_s_rows = jnp.tile(pos_s, bs)

    def tables(pos):
        mc, msa, msb, _ = _rope_tables(pos, MLA_ROPE_DIM, MLA_ROPE_THETA, LANES)
        dc, dsa, dsb, _ = _rope_tables(pos, DIFF_ROT, ROPE_THETA, DIFF_HD)
        return mc, msa, msb, dc, dsa, dsb

    yp = x_prompt.reshape(tp, D_MODEL)
    ys = x_sample.reshape(ts, D_MODEL)
    even_p, even_s, odd_p, odd_s = [], [], [], []
    for layer in range(DEPTH):
        i = layer // 2
        if layer % 2 == 0:
            lam_init = 0.8 - 0.6 * math.exp(-0.3 * layer)
            wts = _even_weights(w_in_even[i], q_norm_g[i], w_uq[i], kv_norm_g[i], w_uk[i])
            wuv_bd = _uv_block_diag(w_uv[i])
            sg = diff_subln_g[i].reshape(1, 2 * DIFF_HD)
            wout = w_out_even[i].astype(BF16)
            lng, lnb = ln_even_g[i].reshape(1, D_MODEL), ln_even_b[i].reshape(1, D_MODEL)
            lq, lk = diff_lambda_q[i], diff_lambda_k[i]
            qcat, kcat, ckv, krope, gate, dq, dkf, dkb, dvf, dvb = _even_in(yp, wts, tables(pos_p), sp, TM_PROJ)
            olat = _mla_prompt(qcat.reshape(bp, sp, -1), kcat.reshape(bp, sp, -1), TQ_SOFTMAX, TK_SOFTMAX)
            odiff = _diff_prompt(dq.reshape(bp, sp, -1), dkb.reshape(bp, sp, -1), dvb.reshape(bp, sp, -1),
                                 lq, lk, lam_init, 2 * TQ_SOFTMAX, TK_SOFTMAX)
            yp_new = _even_out(olat.reshape(tp, -1), odiff.reshape(tp, -1), gate, yp, wuv_bd, sg, wout, lng, lnb,
                               lam_init, TM_PROJ)
            even_p.append((ckv.reshape(bp, sp, KV_LORA), krope.reshape(bp, sp, MLA_ROPE_DIM),
                           dkf.reshape(bp, sp, DIFF_HEADS, 2, DIFF_HD), dvf.reshape(bp, sp, DIFF_HEADS, 2 * DIFF_HD)))
            qcat, kcat, ckv, krope, gate, dq, dkf, dkb, dvf, dvb = _even_in(ys, wts, tables(pos_s_rows), ts, TM_PROJ)
            olat = _mla_decode(qcat.reshape(bs, ss, -1), cache_mla_ckv[i], cache_mla_krope[i],
                               kcat.reshape(bs, ss, -1))
            odiff = _diff_decode(dq.reshape(bs, ss, -1), cache_diff_k[i].reshape(bs, past, DIFF_W),
                                 cache_diff_v[i].reshape(bs, past, DIFF_W), dkb.reshape(bs, ss, -1),
                                 dvb.reshape(bs, ss, -1), lq, lk, lam_init)
            ys_new = _even_out(olat.reshape(ts, -1), odiff.reshape(ts, -1), gate, ys, wuv_bd, sg, wout, lng, lnb,
                               lam_init, TM_PROJ)
            even_s.append((ckv.reshape(bs, ss, KV_LORA), krope.reshape(bs, ss, MLA_ROPE_DIM),
                           dkf.reshape(bs, ss, DIFF_HEADS, 2, DIFF_HD), dvf.reshape(bs, ss, DIFF_HEADS, 2 * DIFF_HD)))
            yp, ys = yp_new, ys_new
        else:
            w_in = w_in_odd[i].astype(BF16)
            wout = w_out_odd[i].astype(BF16)
            lng, lnb = ln_odd_g[i].reshape(1, D_MODEL), ln_odd_b[i].reshape(1, D_MODEL)
            q, kf, kb, vf, vb, gate = _odd_in(yp, w_in, TM_PROJ)
            o = _sb_prompt(q.reshape(bp, sp, -1), kb.reshape(bp, sp, -1), vb.reshape(bp, sp, -1), TQ_SB, SB_PAIRS)
            yp_new = _odd_out(o.reshape(tp, -1), gate, yp, wout, lng, lnb, TM_PROJ)
            odd_p.append((kf.reshape(bp, sp, SB_HEADS, SB_HD), vf.reshape(bp, sp, SB_HEADS, SB_HD)))
            q, kf, kb, vf, vb, gate = _odd_in(ys, w_in, TM_PROJ)
            o = _sb_decode(q.reshape(bs, ss, -1), cache_sb_k[i].reshape(bs, past, SB_W),
                           cache_sb_v[i].reshape(bs, past, SB_W), kb.reshape(bs, ss, -1), vb.reshape(bs, ss, -1),
                           TK_SB_DECODE)
            ys_new = _odd_out(o.reshape(ts, -1), gate, ys, wout, lng, lnb, TM_PROJ)
            odd_s.append((kf.reshape(bs, ss, SB_HEADS, SB_HD), vf.reshape(bs, ss, SB_HEADS, SB_HD)))
            yp, ys = yp_new, ys_new
    stack = lambda lst, k: jnp.stack([t[k] for t in lst])
    return (yp.reshape(bp, sp, D_MODEL), ys.reshape(bs, ss, D_MODEL),
            stack(even_p, 0), stack(even_p, 1), stack(even_p, 2), stack(even_p, 3),
            stack(odd_p, 0), stack(odd_p, 1),
            stack(even_s, 0), stack(even_s, 1), stack(even_s, 2), stack(even_s, 3),
            stack(odd_s, 0), stack(odd_s, 1))
```

```python
import functools
import math

import numpy as np
import jax
import jax.numpy as jnp
from jax import lax
from jax.experimental import pallas as pl
from jax.experimental.pallas import tpu as pltpu

F32 = jnp.float32
BF16 = jnp.bfloat16

D_MODEL = 1024
DEPTH = 2
CHUNK = 64
ALPHA = (2 * DEPTH) ** 0.25
EPS = 1e-5
NEG_INF = -1e30
MLA_HEADS = 8
Q_LORA = 256
KV_LORA = 128
MLA_NOPE_DIM = 64
MLA_ROPE_DIM = 32
MLA_QK_DIM = MLA_NOPE_DIM + MLA_ROPE_DIM
MLA_V_DIM = 64
MLA_ROPE_THETA = 10000.0
MLA_W = MLA_HEADS * MLA_V_DIM
DIFF_HEADS = 4
DIFF_HD = 64
DIFF_ROT = DIFF_HD // 4
ROPE_THETA = 500000.0
DIFF_W = DIFF_HEADS * 2 * DIFF_HD
SB_HEADS = 16
SB_HD = 64
SB_W = SB_HEADS * SB_HD

LANES = 128
MLA_QCAT = 2 * LANES
VMEM_LIMIT = 56 * 1024 * 1024

_NT = (((1,), (1,)), ((), ()))
LOG2E = math.log2(math.e)


def _params(sem):
    return pltpu.CompilerParams(dimension_semantics=sem, vmem_limit_bytes=VMEM_LIMIT)


def _const_spec(shape):
    nd = len(shape)
    return pl.BlockSpec(shape, lambda *_: (0,) * nd, pipeline_mode=pl.Buffered(1))


def _rope_tables(pos, rot_dim, theta, period):
    half = rot_dim // 2
    inv = jnp.exp(-math.log(theta) * jnp.arange(half, dtype=F32) * (2.0 / rot_dim))
    ang = pos.astype(F32)[:, None] * inv[None, :]
    cos, sin = jnp.cos(ang), jnp.sin(ang)
    lane = np.arange(LANES) % period
    idx = lane % half
    in_rot = lane < rot_dim
    first = lane < half
    second = in_rot & ~first
    cos_l, sin_l = cos[:, idx], sin[:, idx]
    c = jnp.where(in_rot[None, :], cos_l, 1.0)
    sa = jnp.where(first[None, :], -sin_l, 0.0)
    sb = jnp.where(second[None, :], sin_l, 0.0)
    return c, sa, sb, half


def _apply_rope(x, c, sa, sb, half):
    return x * c + pltpu.roll(x, LANES - half, 1) * sa + pltpu.roll(x, half, 1) * sb


def _silu(x):
    return x / (1.0 + jnp.exp(-x))


def _rms(x, g):
    ms = jnp.mean(x * x, axis=-1, keepdims=True)
    return x * lax.rsqrt(ms + EPS) * g


def _layernorm(r, g, b):
    mu = jnp.mean(r, axis=-1, keepdims=True)
    rc = r - mu
    var = jnp.mean(rc * rc, axis=-1, keepdims=True)
    return rc * lax.rsqrt(var + EPS) * g + b


_C_QLAT = 0
_C_CKV = Q_LORA
_C_KROPE = _C_CKV + KV_LORA
_C_GMLA = _C_KROPE + LANES
_C_DQ = _C_GMLA + MLA_W
_C_DK = _C_DQ + DIFF_W
_C_DV = _C_DK + DIFF_W
_C_GDIFF = _C_DV + DIFF_W
_EVEN_COLS = _C_GDIFF + DIFF_W


def _even_in_kernel(x_ref, w_ref, wqn_ref, wqr_ref, wuk_ref, qg_ref, kvg_ref,
                    mc_ref, msa_ref, msb_ref, dc_ref, dsa_ref, dsb_ref,
                    qcat_ref, kcat_ref, ckv_ref, krope_ref, gate_ref, dq_ref,
                    dkf_ref, dkb_ref, dvf_ref, dvb_ref):
    xb = x_ref[...].astype(BF16)

    def proj(c0, c1):
        return jnp.dot(xb, w_ref[:, c0:c1], preferred_element_type=F32)

    mla_half = MLA_ROPE_DIM // 2
    diff_half = DIFF_ROT // 2
    mc, msa, msb = mc_ref[...], msa_ref[...], msb_ref[...]
    dc, dsa, dsb = dc_ref[...], dsa_ref[...], dsb_ref[...]

    qn = _rms(proj(_C_QLAT, _C_QLAT + Q_LORA), qg_ref[...]).astype(BF16)
    q_nope = jnp.dot(qn, wqn_ref[...], preferred_element_type=F32).astype(BF16)
    q_abs = jnp.dot(q_nope, wuk_ref[...], preferred_element_type=F32)
    q_rp = jnp.dot(qn, wqr_ref[...], preferred_element_type=F32)
    scale = MLA_QK_DIM ** -0.5 * LOG2E
    for h in range(MLA_HEADS):
        rot = _apply_rope(q_rp[:, h * LANES:(h + 1) * LANES], mc, msa, msb, mla_half)
        qcat_ref[:, h * MLA_QCAT:h * MLA_QCAT + LANES] = (q_abs[:, h * LANES:(h + 1) * LANES] * scale).astype(BF16)
        qcat_ref[:, h * MLA_QCAT + LANES:(h + 1) * MLA_QCAT] = (rot * scale).astype(BF16)

    kv = proj(_C_CKV, _C_GMLA)
    ckv = _rms(kv[:, :KV_LORA], kvg_ref[...])
    krope = _apply_rope(kv[:, KV_LORA:], mc, msa, msb, mla_half)
    ckv_ref[...] = ckv
    krope_ref[...] = krope[:, :MLA_ROPE_DIM]
    kcat_ref[:, :LANES] = ckv.astype(BF16)
    kcat_ref[:, LANES:] = krope.astype(BF16)

    gate_ref[:, :MLA_W] = _silu(proj(_C_GMLA, _C_DQ)).astype(BF16)
    gate_ref[:, MLA_W:] = _silu(proj(_C_GDIFF, _EVEN_COLS)).astype(BF16)

    dq = proj(_C_DQ, _C_DK)
    dk = proj(_C_DK, _C_DV)
    dv = proj(_C_DV, _C_GDIFF)
    dscale = DIFF_HD ** -0.5 * LOG2E
    for h in range(DIFF_HEADS):
        sl = slice(h * LANES, (h + 1) * LANES)
        dq_ref[:, sl] = (_apply_rope(dq[:, sl], dc, dsa, dsb, diff_half) * dscale).astype(BF16)
        kr = _apply_rope(dk[:, sl], dc, dsa, dsb, diff_half)
        dkf_ref[:, sl] = kr
        dkb_ref[:, sl] = kr.astype(BF16)
    dvf_ref[...] = dv
    dvb_ref[...] = dv.astype(BF16)


def _even_in(x2d, wts, tabs, tab_rows_per_rep, tm):
    t = x2d.shape[0]
    tm = min(tm, t)
    nrep = tab_rows_per_rep // tm
    w_in, wqn, wqr, wuk, qg, kvg = wts
    row = lambda w: pl.BlockSpec((tm, w), lambda i: (i, 0))
    tab = pl.BlockSpec((tm, LANES), lambda i: (i % nrep, 0))
    outs = [
        (MLA_HEADS * MLA_QCAT, BF16),
        (MLA_QCAT, BF16),
        (KV_LORA, F32),
        (MLA_ROPE_DIM, F32),
        (MLA_W + DIFF_W, BF16),
        (DIFF_W, BF16),
        (DIFF_W, F32), (DIFF_W, BF16),
        (DIFF_W, F32), (DIFF_W, BF16),
    ]
    return pl.pallas_call(
        _even_in_kernel,
        grid=(t // tm,),
        in_specs=[row(D_MODEL), _const_spec(w_in.shape), _const_spec(wqn.shape), _const_spec(wqr.shape),
                  _const_spec(wuk.shape), _const_spec(qg.shape), _const_spec(kvg.shape)] + [tab] * 6,
        out_specs=[row(w) for w, _ in outs],
        out_shape=[jax.ShapeDtypeStruct((t, w), d) for w, d in outs],
        compiler_params=_params(("parallel",)),
        name="even_in",
    )(x2d, w_in, wqn, wqr, wuk, qg, kvg, *tabs)


def _softmax_step(s, vt, m_ref, l_ref, acc_ref, valid):
    if valid is not None:
        s = jnp.where(valid, s, NEG_INF)
    m_old = m_ref[...]
    m_new = jnp.maximum(m_old, jnp.max(s, axis=0, keepdims=True))
    alpha = jnp.exp2(m_old - m_new)
    p = jnp.exp2(s - m_new)
    l_ref[...] = alpha * l_ref[...] + jnp.sum(p, axis=0, keepdims=True)
    acc_ref[...] = alpha * acc_ref[...] + jnp.dot(vt, p.astype(BF16), preferred_element_type=F32)
    m_ref[...] = m_new


def _chunk_causal_flash(qs, get_k, get_vt, q0, tq, tk, s_ref, m_ref, l_ref, acc_ref):
    m_rows = qs.shape[0]
    m_ref[...] = jnp.full(m_ref.shape, NEG_INF, F32)
    l_ref[...] = jnp.zeros(l_ref.shape, F32)
    acc_ref[...] = jnp.zeros(acc_ref.shape, F32)
    n_full = q0 // tk

    def scores(j):
        return lax.dot_general(get_k(pl.multiple_of(j * tk, tk)), qs, _NT, preferred_element_type=F32)

    sa_ref, sb_ref = s_ref
    sa_ref[...] = scores(0)

    def step(j, cur_ref, nxt_ref):
        nxt_ref[...] = scores(j + 1)
        _softmax_step(cur_ref[...], get_vt(j), m_ref, l_ref, acc_ref, None)

    def body(jj, carry):
        step(2 * jj, sa_ref, sb_ref)
        step(2 * jj + 1, sb_ref, sa_ref)
        return carry

    lax.fori_loop(0, n_full // 2, body, 0)
    kpos = n_full * tk + lax.broadcasted_iota(jnp.int32, (tk, m_rows), 0)
    qpos = q0 + (lax.broadcasted_iota(jnp.int32, (tk, m_rows), 1) & (tq - 1))
    shift = CHUNK.bit_length() - 1
    valid = (kpos >> shift) <= (qpos >> shift)

    @pl.when(n_full % 2 == 1)
    def _():
        step(n_full - 1, sa_ref, sb_ref)
        _softmax_step(sb_ref[...], get_vt(n_full), m_ref, l_ref, acc_ref, valid)

    @pl.when(n_full % 2 == 0)
    def _():
        _softmax_step(sa_ref[...], get_vt(n_full), m_ref, l_ref, acc_ref, valid)


def _mla_prompt_kernel(q_ref, k_ref, vt_ref, o_ref, sa_ref, sb_ref, m_ref, l_ref, acc_ref, *, tq, tk):
    s_ref = (sa_ref, sb_ref)
    q0 = pl.program_id(1) * tq
    qs = jnp.concatenate([q_ref[0, :, h * MLA_QCAT:(h + 1) * MLA_QCAT] for h in range(MLA_HEADS)], axis=0)
    get_k = lambda k0: k_ref[0, pl.ds(k0, tk), :]
    get_vt = lambda j: vt_ref[0, j]
    _chunk_causal_flash(qs, get_k, get_vt, q0, tq, tk, s_ref, m_ref, l_ref, acc_ref)
    out = acc_ref[...] / l_ref[...]
    for h in range(MLA_HEADS):
        o_ref[0, :, h * KV_LORA:(h + 1) * KV_LORA] = out[:, h * tq:(h + 1) * tq].T.astype(o_ref.dtype)


def _key_major(x, tk):
    b, s, w = x.shape
    return x.reshape(b, s // tk, tk, w).transpose(0, 1, 3, 2)


def _mla_prompt(qcat, kcat, tq, tk):
    b, s, _ = qcat.shape
    tq, tk = min(tq, s), min(tk, s)
    m_rows = MLA_HEADS * tq
    vt = _key_major(kcat[:, :, :KV_LORA], tk)
    return pl.pallas_call(
        functools.partial(_mla_prompt_kernel, tq=tq, tk=tk),
        grid=(b, s // tq),
        in_specs=[pl.BlockSpec((1, tq, MLA_HEADS * MLA_QCAT), lambda bi, i: (bi, i, 0)),
                  pl.BlockSpec((1, s, MLA_QCAT), lambda bi, i: (bi, 0, 0)),
                  pl.BlockSpec((1, s // tk, KV_LORA, tk), lambda bi, i: (bi, 0, 0, 0))],
        out_specs=pl.BlockSpec((1, tq, MLA_HEADS * KV_LORA), lambda bi, i: (bi, i, 0)),
        out_shape=jax.ShapeDtypeStruct((b, s, MLA_HEADS * KV_LORA), BF16),
        scratch_shapes=[pltpu.VMEM((tk, m_rows), F32), pltpu.VMEM((tk, m_rows), F32), pltpu.VMEM((1, m_rows), F32),
                        pltpu.VMEM((1, m_rows), F32), pltpu.VMEM((KV_LORA, m_rows), F32)],
        compiler_params=_params(("parallel", "parallel")),
        name="mla_prompt",
    )(qcat, kcat, vt)


def _lane_halves(x):
    lane = lax.broadcasted_iota(jnp.int32, x.shape, 1)
    zero = jnp.zeros_like(x)
    return jnp.concatenate([jnp.where(lane < LANES // 2, x, zero), jnp.where(lane < LANES // 2, zero, x)], axis=0)


def _diff_lambda(lq_ref, lk_ref, lam_init):
    e = jnp.exp(jnp.sum(lq_ref[...] * lk_ref[...], axis=-1, keepdims=True))
    return e[0:1] - e[1:2] + lam_init


def _diff_prompt_kernel(q_ref, k_ref, vt_ref, lq_ref, lk_ref, o_ref, sa_ref, sb_ref, m_ref, l_ref, acc_ref,
                        *, tq, tk, lam_init):
    s_ref = (sa_ref, sb_ref)
    q0 = pl.program_id(1) * tq
    lam = _diff_lambda(lq_ref, lk_ref, lam_init)
    for h in range(DIFF_HEADS):
        sl = slice(h * LANES, (h + 1) * LANES)
        qs = _lane_halves(q_ref[0, :, sl])
        get_k = lambda k0, sl=sl: k_ref[0, pl.ds(k0, tk), sl]
        get_vt = lambda j, sl=sl: vt_ref[0, j, sl, :]
        _chunk_causal_flash(qs, get_k, get_vt, q0, tq, tk, s_ref, m_ref, l_ref, acc_ref)
        o = acc_ref[...] / l_ref[...]
        o_ref[0, :, sl] = (o[:, :tq] - lam * o[:, tq:]).T


def _diff_prompt(dq, dk, dv, lq, lk, lam_init, tq, tk):
    b, s, _ = dq.shape
    tq, tk = min(tq, s), min(tk, s)
    qspec = pl.BlockSpec((1, tq, DIFF_W), lambda bi, i: (bi, i, 0))
    return pl.pallas_call(
        functools.partial(_diff_prompt_kernel, tq=tq, tk=tk, lam_init=lam_init),
        grid=(b, s // tq),
        in_specs=[qspec, pl.BlockSpec((1, s, DIFF_W), lambda bi, i: (bi, 0, 0)),
                  pl.BlockSpec((1, s // tk, DIFF_W, tk), lambda bi, i: (bi, 0, 0, 0)),
                  _const_spec(lq.shape), _const_spec(lk.shape)],
        out_specs=qspec,
        out_shape=jax.ShapeDtypeStruct((b, s, DIFF_W), F32),
        scratch_shapes=[pltpu.VMEM((tk, 2 * tq), F32), pltpu.VMEM((tk, 2 * tq), F32), pltpu.VMEM((1, 2 * tq), F32),
                        pltpu.VMEM((1, 2 * tq), F32), pltpu.VMEM((LANES, 2 * tq), F32)],
        compiler_params=_params(("parallel", "parallel")),
        name="diff_prompt",
    )(dq, dk, _key_major(dv, tk), lq, lk)


def _pad_rows(x, rows):
    return jnp.concatenate([x, jnp.zeros((rows - x.shape[0], x.shape[1]), x.dtype)], axis=0)


def _decode_softmax(s_c, s_n, v_cache, v_new, n_new):
    col = lax.broadcasted_iota(jnp.int32, s_n.shape, 1)
    s_n = jnp.where(col < n_new, s_n, NEG_INF)
    m = jnp.maximum(jnp.max(s_c, axis=-1, keepdims=True), jnp.max(s_n, axis=-1, keepdims=True))
    p_c = jnp.exp2(s_c - m)
    p_n = jnp.exp2(s_n - m)
    l = jnp.sum(p_c, axis=-1, keepdims=True) + jnp.sum(p_n, axis=-1, keepdims=True)
    acc = (jnp.dot(p_c.astype(BF16), v_cache, preferred_element_type=F32)
           + jnp.dot(p_n.astype(BF16), v_new, preferred_element_type=F32))
    return acc / l


def _mla_decode_kernel(q_ref, ckv_ref, kr_ref, kn_ref, o_ref, *, n_new):
    qs = jnp.concatenate([q_ref[0, :, h * MLA_QCAT:(h + 1) * MLA_QCAT] for h in range(MLA_HEADS)], axis=0)
    ckv = ckv_ref[0].astype(BF16)
    kr = kr_ref[0].astype(BF16)
    k_new = _pad_rows(kn_ref[0], LANES)
    s_c = (lax.dot_general(qs[:, :KV_LORA], ckv, _NT, preferred_element_type=F32)
           + lax.dot_general(qs[:, KV_LORA:KV_LORA + MLA_ROPE_DIM], kr, _NT, preferred_element_type=F32))
    s_n = lax.dot_general(qs, k_new, _NT, preferred_element_type=F32)
    out = _decode_softmax(s_c, s_n, ckv, k_new[:, :KV_LORA], n_new)
    for h in range(MLA_HEADS):
        o_ref[0, :, h * KV_LORA:(h + 1) * KV_LORA] = out[h * n_new:(h + 1) * n_new].astype(o_ref.dtype)


def _mla_decode(qcat, ckv_c, kr_c, kcat_new):
    b, n, _ = qcat.shape
    past = ckv_c.shape[1]
    return pl.pallas_call(
        functools.partial(_mla_decode_kernel, n_new=n),
        grid=(b,),
        in_specs=[pl.BlockSpec((1, n, MLA_HEADS * MLA_QCAT), lambda i: (i, 0, 0)),
                  pl.BlockSpec((1, past, KV_LORA), lambda i: (i, 0, 0)),
                  pl.BlockSpec((1, past, MLA_ROPE_DIM), lambda i: (i, 0, 0)),
                  pl.BlockSpec((1, n, MLA_QCAT), lambda i: (i, 0, 0))],
        out_specs=pl.BlockSpec((1, n, MLA_HEADS * KV_LORA), lambda i: (i, 0, 0)),
        out_shape=jax.ShapeDtypeStruct((b, n, MLA_HEADS * KV_LORA), BF16),
        compiler_params=_params(("parallel",)),
        name="mla_decode",
    )(qcat, ckv_c, kr_c, kcat_new)


def _diff_decode_kernel(q_ref, kc_ref, vc_ref, kn_ref, vn_ref, lq_ref, lk_ref, o_ref, *, n_new, lam_init):
    lam = _diff_lambda(lq_ref, lk_ref, lam_init)
    qs = _lane_halves(q_ref[0])
    s_c = lax.dot_general(qs, kc_ref[0].astype(BF16), _NT, preferred_element_type=F32)
    s_n = lax.dot_general(qs, _pad_rows(kn_ref[0], LANES), _NT, preferred_element_type=F32)
    o = _decode_softmax(s_c, s_n, vc_ref[0].astype(BF16), _pad_rows(vn_ref[0], LANES), n_new)
    o_ref[0] = o[:n_new] - lam * o[n_new:]


def _diff_decode(dq, kc, vc, dk_new, dv_new, lq, lk, lam_init):
    b, n, _ = dq.shape
    past = kc.shape[1]
    new = pl.BlockSpec((1, n, LANES), lambda i, h: (i, 0, h))
    cache = pl.BlockSpec((1, past, LANES), lambda i, h: (i, 0, h))
    return pl.pallas_call(
        functools.partial(_diff_decode_kernel, n_new=n, lam_init=lam_init),
        grid=(b, DIFF_HEADS),
        in_specs=[new, cache, cache, new, new, _const_spec(lq.shape), _const_spec(lk.shape)],
        out_specs=new,
        out_shape=jax.ShapeDtypeStruct((b, n, DIFF_W), F32),
        compiler_params=_params(("parallel", "parallel")),
        name="diff_decode",
    )(dq, kc, vc, dk_new, dv_new, lq, lk)


def _even_out_kernel(olat_ref, odiff_ref, gate_ref, x_ref, wuv_ref, sg_ref, wout_ref, lng_ref, lnb_ref, y_ref,
                     *, lam_init):
    gate = gate_ref[...]
    o_mla = jnp.dot(olat_ref[...], wuv_ref[...], preferred_element_type=F32)
    parts = [(o_mla * gate[:, :MLA_W].astype(F32)).astype(BF16)]
    od = odiff_ref[...]
    for h in range(DIFF_HEADS):
        sl = slice(h * LANES, (h + 1) * LANES)
        n = _rms(od[:, sl], sg_ref[...]) * (1.0 - lam_init)
        parts.append((n * gate[:, MLA_W + h * LANES:MLA_W + (h + 1) * LANES].astype(F32)).astype(BF16))
    o = jnp.dot(jnp.concatenate(parts, axis=1), wout_ref[...], preferred_element_type=F32)
    y_ref[...] = _layernorm(ALPHA * x_ref[...] + o, lng_ref[...], lnb_ref[...])


def _even_out(olat, odiff, gate, x2d, wuv_bd, sg, wout, lng, lnb, lam_init, tm):
    t = x2d.shape[0]
    tm = min(tm, t)
    row = lambda w: pl.BlockSpec((tm, w), lambda i: (i, 0))
    return pl.pallas_call(
        functools.partial(_even_out_kernel, lam_init=lam_init),
        grid=(t // tm,),
        in_specs=[row(MLA_HEADS * KV_LORA), row(DIFF_W), row(MLA_W + DIFF_W), row(D_MODEL),
                  _const_spec(wuv_bd.shape), _const_spec(sg.shape), _const_spec(wout.shape),
                  _const_spec(lng.shape), _const_spec(lnb.shape)],
        out_specs=row(D_MODEL),
        out_shape=jax.ShapeDtypeStruct((t, D_MODEL), F32),
        compiler_params=_params(("parallel",)),
        name="even_out",
    )(olat, odiff, gate, x2d, wuv_bd, sg, wout, lng, lnb)


def _odd_in_kernel(x_ref, w_ref, q_ref, kf_ref, kb_ref, vf_ref, vb_ref, g_ref):
    xb = x_ref[...].astype(BF16)

    def proj(i):
        return jnp.dot(xb, w_ref[:, i * SB_W:(i + 1) * SB_W], preferred_element_type=F32)

    q_ref[...] = (proj(0) * (SB_HD ** -0.5 * LOG2E)).astype(BF16)
    k = proj(1)
    kf_ref[...] = k
    kb_ref[...] = k.astype(BF16)
    v = proj(2)
    vf_ref[...] = v
    vb_ref[...] = v.astype(BF16)
    g_ref[...] = _silu(proj(3)).astype(BF16)


def _odd_in(x2d, w_in, tm):
    t = x2d.shape[0]
    tm = min(tm, t)
    row = pl.BlockSpec((tm, SB_W), lambda i: (i, 0))
    dts = [BF16, F32, BF16, F32, BF16, BF16]
    return pl.pallas_call(
        _odd_in_kernel,
        grid=(t // tm,),
        in_specs=[pl.BlockSpec((tm, D_MODEL), lambda i: (i, 0)), _const_spec(w_in.shape)],
        out_specs=[row] * len(dts),
        out_shape=[jax.ShapeDtypeStruct((t, SB_W), d) for d in dts],
        compiler_params=_params(("parallel",)),
        name="odd_in",
    )(x2d, w_in)


SB_EXIT_LOG2 = 160.0


def _log2(x):
    return jnp.log(x) * LOG2E


def _sb_step(qs, kb, vb, u, c, valid):
    z = lax.dot_general(qs, kb, _NT, preferred_element_type=F32)
    lg = _log2(1.0 + jnp.exp2(-jnp.abs(z)))
    sp = jnp.maximum(z, 0.0) + lg
    lb = jnp.minimum(z, 0.0) - lg
    if valid is not None:
        sp = jnp.where(valid, sp, 0.0)
    spb = sp.astype(BF16)
    later = jnp.dot(spb, u, preferred_element_type=F32)
    a = jnp.exp2(lb - later - c)
    if valid is not None:
        a = jnp.where(valid, a, 0.0)
    pv = jnp.dot(a.astype(BF16), vb, preferred_element_type=F32)
    c_new = c + later[:, :1] + spb[:, :1].astype(F32)
    return pv, c_new


def _later_matrix(n):
    return (lax.broadcasted_iota(jnp.int32, (n, n), 0) > lax.broadcasted_iota(jnp.int32, (n, n), 1)).astype(BF16)


def _softplus2(z):
    return jnp.maximum(z, 0.0) + _log2(1.0 + jnp.exp2(-jnp.abs(z)))


def _sb_prompt_kernel(q_ref, k_ref, vt_ref, o_ref, c_ref, acc_ref, x_ref, sp0_ref, *, tq, pairs):
    i = pl.program_id(2)
    ut = (lax.broadcasted_iota(jnp.int32, (tq, tq), 1) > lax.broadcasted_iota(jnp.int32, (tq, tq), 0)).astype(BF16)
    keys = lax.broadcasted_iota(jnp.int32, (tq, 2 * tq), 0)
    queries = lax.broadcasted_iota(jnp.int32, (tq, 2 * tq), 1) & (tq - 1)
    diag_valid = keys < queries
    half = LANES // 2
    sls = [slice(p * LANES, (p + 1) * LANES) for p in range(pairs)]
    qss = [_lane_halves(q_ref[0, :, sl]) for sl in sls]

    def all_pairs(j, valid, first):
        k0 = pl.multiple_of(j * tq, tq)
        zs = [lax.dot_general(k_ref[0, pl.ds(k0, tq), sl], qss[p], _NT, preferred_element_type=F32)
              for p, sl in enumerate(sls)]
        laters = []
        for p, z in enumerate(zs):
            sp = _softplus2(z)
            x_ref[p] = z - sp
            if valid is not None:
                sp = jnp.where(valid, sp, 0.0)
            spb = sp.astype(BF16)
            sp0_ref[p] = spb[:1, :].astype(F32)
            laters.append(jnp.dot(ut, spb, preferred_element_type=F32))
        cmin = None
        for p, sl in enumerate(sls):
            a = jnp.exp2(x_ref[p] - laters[p])
            if valid is not None:
                a = jnp.where(valid, a, 0.0)
            pv = jnp.dot(vt_ref[0, j, sl, :], a.astype(BF16), preferred_element_type=F32)
            tot = laters[p][:1, :] + sp0_ref[p]
            if first:
                acc_ref[p] = pv
                c = tot
            else:
                c0 = c_ref[p]
                acc_ref[p] += pv * jnp.exp2(-c0)
                c = c0 + tot
            c_ref[p] = c
            cmin = c if cmin is None else jnp.minimum(cmin, c)
        return jnp.min(cmin)

    cmin = all_pairs(i, diag_valid, True)
    lax.while_loop(lambda jc: (jc[0] >= 0) & (jc[1] < SB_EXIT_LOG2),
                   lambda jc: (jc[0] - 1, all_pairs(jc[0], None, False)),
                   (i - 1, cmin))

    for p, sl in enumerate(sls):
        acc = acc_ref[p]
        o_t = jnp.concatenate([acc[:half, :tq], acc[half:, tq:]], axis=0)
        o_ref[0, :, sl] = o_t.T.astype(o_ref.dtype)


def _sb_prompt(q, k, v, tq, pairs):
    b, s, w = q.shape
    tq = min(tq, s)
    gw = pairs * LANES
    qspec = pl.BlockSpec((1, tq, gw), lambda bi, g, i: (bi, i, g))
    return pl.pallas_call(
        functools.partial(_sb_prompt_kernel, tq=tq, pairs=pairs),
        grid=(b, w // gw, s // tq),
        in_specs=[qspec, pl.BlockSpec((1, s, gw), lambda bi, g, i: (bi, 0, g)),
                  pl.BlockSpec((1, s // tq, gw, tq), lambda bi, g, i: (bi, 0, g, 0))],
        out_specs=qspec,
        out_shape=jax.ShapeDtypeStruct((b, s, w), BF16),
        scratch_shapes=[pltpu.VMEM((pairs, 1, 2 * tq), F32), pltpu.VMEM((pairs, LANES, 2 * tq), F32),
                        pltpu.VMEM((pairs, tq, 2 * tq), F32), pltpu.VMEM((pairs, 1, 2 * tq), F32)],
        compiler_params=_params(("parallel", "parallel", "parallel")),
        name="sb_prompt",
    )(q, k, _key_major(v, tq))


def _sb_decode_kernel(q_ref, kc_ref, vc_ref, kn_ref, vn_ref, o_ref, *, n_new, tk):
    qs = _lane_halves(q_ref[0])
    rows = lax.broadcasted_iota(jnp.int32, (2 * n_new, LANES), 0) & (n_new - 1)
    cols = lax.broadcasted_iota(jnp.int32, (2 * n_new, LANES), 1)
    acc, c = _sb_step(qs, _pad_rows(kn_ref[0], LANES), _pad_rows(vn_ref[0], LANES), _later_matrix(LANES),
                      jnp.zeros((2 * n_new, 1), F32), cols < rows)
    u = _later_matrix(tk)
    past = kc_ref.shape[1]
    for j in reversed(range(past // tk)):
        def visit(acc, c, j=j):
            pv, c = _sb_step(qs, kc_ref[0, j * tk:(j + 1) * tk, :].astype(BF16),
                             vc_ref[0, j * tk:(j + 1) * tk, :].astype(BF16), u, c, None)
            return acc + pv, c

        acc, c = lax.cond(jnp.min(c) < SB_EXIT_LOG2, visit, lambda acc, c: (acc, c), acc, c)
    lane = lax.broadcasted_iota(jnp.int32, (n_new, LANES), 1)
    o_ref[0] = jnp.where(lane < LANES // 2, acc[:n_new], acc[n_new:]).astype(o_ref.dtype)


def _sb_decode(q, kc, vc, k_new, v_new, tk):
    b, n, w = q.shape
    past = kc.shape[1]
    tk = min(tk, past)
    new = pl.BlockSpec((1, n, LANES), lambda i, g: (i, 0, g))
    cache = pl.BlockSpec((1, past, LANES), lambda i, g: (i, 0, g))
    return pl.pallas_call(
        functools.partial(_sb_decode_kernel, n_new=n, tk=tk),
        grid=(b, w // LANES),
        in_specs=[new, cache, cache, new, new],
        out_specs=new,
        out_shape=jax.ShapeDtypeStruct((b, n, w), BF16),
        compiler_params=_params(("parallel", "parallel")),
        name="sb_decode",
    )(q, kc, vc, k_new, v_new)


def _odd_out_kernel(o_ref, g_ref, x_ref, wout_ref, lng_ref, lnb_ref, y_ref):
    a = (o_ref[...].astype(F32) * g_ref[...].astype(F32)).astype(BF16)
    o = jnp.dot(a, wout_ref[...], preferred_element_type=F32)
    y_ref[...] = _layernorm(ALPHA * x_ref[...] + o, lng_ref[...], lnb_ref[...])


def _odd_out(o, gate, x2d, wout, lng, lnb, tm):
    t = x2d.shape[0]
    tm = min(tm, t)
    row = lambda w: pl.BlockSpec((tm, w), lambda i: (i, 0))
    return pl.pallas_call(
        _odd_out_kernel,
        grid=(t // tm,),
        in_specs=[row(SB_W), row(SB_W), row(D_MODEL), _const_spec(wout.shape),
                  _const_spec(lng.shape), _const_spec(lnb.shape)],
        out_specs=row(D_MODEL),
        out_shape=jax.ShapeDtypeStruct((t, D_MODEL), F32),
        compiler_params=_params(("parallel",)),
        name="odd_out",
    )(o, gate, x2d, wout, lng, lnb)


def _even_weights(w_in, q_g, w_uq, kv_g, w_uk):
    pad = jnp.zeros((D_MODEL, LANES - MLA_ROPE_DIM), w_in.dtype)
    o = np.cumsum([0, Q_LORA, KV_LORA, MLA_ROPE_DIM, MLA_W, DIFF_W, DIFF_W, DIFF_W, DIFF_W])
    w_re = jnp.concatenate([w_in[:, o[0]:o[3]], pad, w_in[:, o[3]:]], axis=1).astype(BF16)
    w_uq3 = w_uq.reshape(Q_LORA, MLA_HEADS, MLA_QK_DIM)
    wqn = w_uq3[:, :, :MLA_NOPE_DIM].reshape(Q_LORA, MLA_HEADS * MLA_NOPE_DIM).astype(BF16)
    wqr = jnp.pad(w_uq3[:, :, MLA_NOPE_DIM:], ((0, 0), (0, 0), (0, LANES - MLA_ROPE_DIM)))
    wqr = wqr.reshape(Q_LORA, MLA_HEADS * LANES).astype(BF16)
    eye = jnp.eye(MLA_HEADS, dtype=w_uk.dtype)
    wuk_bd = jnp.einsum('chn,hg->hngc', w_uk, eye).reshape(MLA_HEADS * MLA_NOPE_DIM, MLA_HEADS * KV_LORA).astype(BF16)
    return w_re, wqn, wqr, wuk_bd, q_g.reshape(1, Q_LORA), kv_g.reshape(1, KV_LORA)


def _uv_block_diag(w_uv):
    eye = jnp.eye(MLA_HEADS, dtype=w_uv.dtype)
    return jnp.einsum('chv,hg->hcgv', w_uv, eye).reshape(MLA_HEADS * KV_LORA, MLA_HEADS * MLA_V_DIM).astype(BF16)


TM_PROJ = 512
TQ_SOFTMAX = 128
TK_SOFTMAX = 512
TQ_SB = 256
SB_PAIRS = 8
TK_SB_DECODE = 256


def kernel(x_prompt, x_sample, cache_mla_ckv, cache_mla_krope, cache_diff_k, cache_diff_v, cache_sb_k, cache_sb_v,
           w_in_even, q_norm_g, w_uq, kv_norm_g, w_uk, w_uv, diff_lambda_q, diff_lambda_k, diff_subln_g,
           w_out_even, ln_even_g, ln_even_b, w_in_odd, w_out_odd, ln_odd_g, ln_odd_b):
    bp, sp, _ = x_prompt.shape
    bs, ss, _ = x_sample.shape
    past = cache_sb_k.shape[2]
    if past % CHUNK or ss > CHUNK or (ss & (ss - 1)) or sp % TQ_SB:
        raise NotImplementedError("decode frames must share one chunk after a chunk-aligned cache")
    tp, ts = bp * sp, bs * ss
    pos_p = jnp.arange(sp, dtype=jnp.int32)
    pos_s = past + jnp.arange(ss, dtype=jnp.int32)
    pos_s_rows = jnp.tile(pos_s, bs)

    def tables(pos):
        mc, msa, msb, _ = _rope_tables(pos, MLA_ROPE_DIM, MLA_ROPE_THETA, LANES)
        dc, dsa, dsb, _ = _rope_tables(pos, DIFF_ROT, ROPE_THETA, DIFF_HD)
        return mc, msa, msb, dc, dsa, dsb

    yp = x_prompt.reshape(tp, D_MODEL)
    ys = x_sample.reshape(ts, D_MODEL)
    even_p, even_s, odd_p, odd_s = [], [], [], []
    for layer in range(DEPTH):
        i = layer // 2
        if layer % 2 == 0:
            lam_init = 0.8 - 0.6 * math.exp(-0.3 * layer)
            wts = _even_weights(w_in_even[i], q_norm_g[i], w_uq[i], kv_norm_g[i], w_uk[i])
            wuv_bd = _uv_block_diag(w_uv[i])
            sg = diff_subln_g[i].reshape(1, 2 * DIFF_HD)
            wout = w_out_even[i].astype(BF16)
            lng, lnb = ln_even_g[i].reshape(1, D_MODEL), ln_even_b[i].reshape(1, D_MODEL)
            lq, lk = diff_lambda_q[i], diff_lambda_k[i]
            qcat, kcat, ckv, krope, gate, dq, dkf, dkb, dvf, dvb = _even_in(yp, wts, tables(pos_p), sp, TM_PROJ)
            olat = _mla_prompt(qcat.reshape(bp, sp, -1), kcat.reshape(bp, sp, -1), TQ_SOFTMAX, TK_SOFTMAX)
            odiff = _diff_prompt(dq.reshape(bp, sp, -1), dkb.reshape(bp, sp, -1), dvb.reshape(bp, sp, -1),
                                 lq, lk, lam_init, 2 * TQ_SOFTMAX, TK_SOFTMAX)
            yp_new = _even_out(olat.reshape(tp, -1), odiff.reshape(tp, -1), gate, yp, wuv_bd, sg, wout, lng, lnb,
                               lam_init, TM_PROJ)
            even_p.append((ckv.reshape(bp, sp, KV_LORA), krope.reshape(bp, sp, MLA_ROPE_DIM),
                           dkf.reshape(bp, sp, DIFF_HEADS, 2, DIFF_HD), dvf.reshape(bp, sp, DIFF_HEADS, 2 * DIFF_HD)))
            qcat, kcat, ckv, krope, gate, dq, dkf, dkb, dvf, dvb = _even_in(ys, wts, tables(pos_s_rows), ts, TM_PROJ)
            olat = _mla_decode(qcat.reshape(bs, ss, -1), cache_mla_ckv[i], cache_mla_krope[i],
                               kcat.reshape(bs, ss, -1))
            odiff = _diff_decode(dq.reshape(bs, ss, -1), cache_diff_k[i].reshape(bs, past, DIFF_W),
                                 cache_diff_v[i].reshape(bs, past, DIFF_W), dkb.reshape(bs, ss, -1),
                                 dvb.reshape(bs, ss, -1), lq, lk, lam_init)
            ys_new = _even_out(olat.reshape(ts, -1), odiff.reshape(ts, -1), gate, ys, wuv_bd, sg, wout, lng, lnb,
                               lam_init, TM_PROJ)
            even_s.append((ckv.reshape(bs, ss, KV_LORA), krope.reshape(bs, ss, MLA_ROPE_DIM),
                           dkf.reshape(bs, ss, DIFF_HEADS, 2, DIFF_HD), dvf.reshape(bs, ss, DIFF_HEADS, 2 * DIFF_HD)))
            yp, ys = yp_new, ys_new
        else:
            w_in = w_in_odd[i].astype(BF16)
            wout = w_out_odd[i].astype(BF16)
            lng, lnb = ln_odd_g[i].reshape(1, D_MODEL), ln_odd_b[i].reshape(1, D_MODEL)
            q, kf, kb, vf, vb, gate = _odd_in(yp, w_in, TM_PROJ)
            o = _sb_prompt(q.reshape(bp, sp, -1), kb.reshape(bp, sp, -1), vb.reshape(bp, sp, -1), TQ_SB, SB_PAIRS)
            yp_new = _odd_out(o.reshape(tp, -1), gate, yp, wout, lng, lnb, TM_PROJ)
            odd_p.append((kf.reshape(bp, sp, SB_HEADS, SB_HD), vf.reshape(bp, sp, SB_HEADS, SB_HD)))
            q, kf, kb, vf, vb, gate = _odd_in(ys, w_in, TM_PROJ)
            o = _sb_decode(q.reshape(bs, ss, -1), cache_sb_k[i].reshape(bs, past, SB_W),
                           cache_sb_v[i].reshape(bs, past, SB_W), kb.reshape(bs, ss, -1), vb.reshape(bs, ss, -1),
                           TK_SB_DECODE)
            ys_new = _odd_out(o.reshape(ts, -1), gate, ys, wout, lng, lnb, TM_PROJ)
            odd_s.append((kf.reshape(bs, ss, SB_HEADS, SB_HD), vf.reshape(bs, ss, SB_HEADS, SB_HD)))
            yp, ys = yp_new, ys_new
    stack = lambda lst, k: jnp.stack([t[k] for t in lst])
    return (yp.reshape(bp, sp, D_MODEL), ys.reshape(bs, ss, D_MODEL),
            stack(even_p, 0), stack(even_p, 1), stack(even_p, 2), stack(even_p, 3),
            stack(odd_p, 0), stack(odd_p, 1),
            stack(even_s, 0), stack(even_s, 1), stack(even_s, 2), stack(even_s, 3),
            stack(odd_s, 0), stack(odd_s, 1))
```

```python
import functools
import math

import numpy as np
import jax
import jax.numpy as jnp
from jax import lax
from jax.experimental import pallas as pl
from jax.experimental.pallas import tpu as pltpu

F32 = jnp.float32
BF16 = jnp.bfloat16

D_MODEL = 1024
DEPTH = 2
CHUNK = 64
ALPHA = (2 * DEPTH) ** 0.25
EPS = 1e-5
NEG_INF = -1e30
MLA_HEADS = 8
Q_LORA = 256
KV_LORA = 128
MLA_NOPE_DIM = 64
MLA_ROPE_DIM = 32
MLA_QK_DIM = MLA_NOPE_DIM + MLA_ROPE_DIM
MLA_V_DIM = 64
MLA_ROPE_THETA = 10000.0
MLA_W = MLA_HEADS * MLA_V_DIM
DIFF_HEADS = 4
DIFF_HD = 64
DIFF_ROT = DIFF_HD // 4
ROPE_THETA = 500000.0
DIFF_W = DIFF_HEADS * 2 * DIFF_HD
SB_HEADS = 16
SB_HD = 64
SB_W = SB_HEADS * SB_HD

LANES = 128
MLA_QCAT = 2 * LANES
VMEM_LIMIT = 56 * 1024 * 1024

_NT = (((1,), (1,)), ((), ()))
LOG2E = math.log2(math.e)


def _params(sem):
    return pltpu.CompilerParams(dimension_semantics=sem, vmem_limit_bytes=VMEM_LIMIT)


def _const_spec(shape):
    nd = len(shape)
    return pl.BlockSpec(shape, lambda *_: (0,) * nd, pipeline_mode=pl.Buffered(1))


def _rope_tables(pos, rot_dim, theta, period):
    half = rot_dim // 2
    inv = jnp.exp(-math.log(theta) * jnp.arange(half, dtype=F32) * (2.0 / rot_dim))
    ang = pos.astype(F32)[:, None] * inv[None, :]
    cos, sin = jnp.cos(ang), jnp.sin(ang)
    lane = np.arange(LANES) % period
    idx = lane % half
    in_rot = lane < rot_dim
    first = lane < half
    second = in_rot & ~first
    cos_l, sin_l = cos[:, idx], sin[:, idx]
    c = jnp.where(in_rot[None, :], cos_l, 1.0)
    sa = jnp.where(first[None, :], -sin_l, 0.0)
    sb = jnp.where(second[None, :], sin_l, 0.0)
    return c, sa, sb, half


def _apply_rope(x, c, sa, sb, half):
    return x * c + pltpu.roll(x, LANES - half, 1) * sa + pltpu.roll(x, half, 1) * sb


def _silu(x):
    return x / (1.0 + jnp.exp(-x))


def _rms(x, g):
    ms = jnp.mean(x * x, axis=-1, keepdims=True)
    return x * lax.rsqrt(ms + EPS) * g


def _layernorm(r, g, b):
    mu = jnp.mean(r, axis=-1, keepdims=True)
    rc = r - mu
    var = jnp.mean(rc * rc, axis=-1, keepdims=True)
    return rc * lax.rsqrt(var + EPS) * g + b


_C_QLAT = 0
_C_CKV = Q_LORA
_C_KROPE = _C_CKV + KV_LORA
_C_GMLA = _C_KROPE + LANES
_C_DQ = _C_GMLA + MLA_W
_C_DK = _C_DQ + DIFF_W
_C_DV = _C_DK + DIFF_W
_C_GDIFF = _C_DV + DIFF_W
_EVEN_COLS = _C_GDIFF + DIFF_W


def _even_in_kernel(x_ref, w_ref, wqn_ref, wqr_ref, wuk_ref, qg_ref, kvg_ref,
                    mc_ref, msa_ref, msb_ref, dc_ref, dsa_ref, dsb_ref,
                    qcat_ref, kcat_ref, ckv_ref, krope_ref, gate_ref, dq_ref,
                    dkf_ref, dkb_ref, dvf_ref, dvb_ref):
    xb = x_ref[...].astype(BF16)

    def proj(c0, c1):
        return jnp.dot(xb, w_ref[:, c0:c1], preferred_element_type=F32)

    mla_half = MLA_ROPE_DIM // 2
    diff_half = DIFF_ROT // 2
    mc, msa, msb = mc_ref[...], msa_ref[...], msb_ref[...]
    dc, dsa, dsb = dc_ref[...], dsa_ref[...], dsb_ref[...]

    qn = _rms(proj(_C_QLAT, _C_QLAT + Q_LORA), qg_ref[...]).astype(BF16)
    q_nope = jnp.dot(qn, wqn_ref[...], preferred_element_type=F32).astype(BF16)
    q_abs = jnp.dot(q_nope, wuk_ref[...], preferred_element_type=F32)
    q_rp = jnp.dot(qn, wqr_ref[...], preferred_element_type=F32)
    scale = MLA_QK_DIM ** -0.5 * LOG2E
    for h in range(MLA_HEADS):
        rot = _apply_rope(q_rp[:, h * LANES:(h + 1) * LANES], mc, msa, msb, mla_half)
        qcat_ref[:, h * MLA_QCAT:h * MLA_QCAT + LANES] = (q_abs[:, h * LANES:(h + 1) * LANES] * scale).astype(BF16)
        qcat_ref[:, h * MLA_QCAT + LANES:(h + 1) * MLA_QCAT] = (rot * scale).astype(BF16)

    kv = proj(_C_CKV, _C_GMLA)
    ckv = _rms(kv[:, :KV_LORA], kvg_ref[...])
    krope = _apply_rope(kv[:, KV_LORA:], mc, msa, msb, mla_half)
    ckv_ref[...] = ckv
    krope_ref[...] = krope[:, :MLA_ROPE_DIM]
    kcat_ref[:, :LANES] = ckv.astype(BF16)
    kcat_ref[:, LANES:] = krope.astype(BF16)

    gate_ref[:, :MLA_W] = _silu(proj(_C_GMLA, _C_DQ)).astype(BF16)
    gate_ref[:, MLA_W:] = _silu(proj(_C_GDIFF, _EVEN_COLS)).astype(BF16)

    dq = proj(_C_DQ, _C_DK)
    dk = proj(_C_DK, _C_DV)
    dv = proj(_C_DV, _C_GDIFF)
    dscale = DIFF_HD ** -0.5 * LOG2E
    for h in range(DIFF_HEADS):
        sl = slice(h * LANES, (h + 1) * LANES)
        dq_ref[:, sl] = (_apply_rope(dq[:, sl], dc, dsa, dsb, diff_half) * dscale).astype(BF16)
        kr = _apply_rope(dk[:, sl], dc, dsa, dsb, diff_half)
        dkf_ref[:, sl] = kr
        dkb_ref[:, sl] = kr.astype(BF16)
    dvf_ref[...] = dv
    dvb_ref[...] = dv.astype(BF16)


def _even_in(x2d, wts, tabs, tab_rows_per_rep, tm):
    t = x2d.shape[0]
    tm = min(tm, t)
    nrep = tab_rows_per_rep // tm
    w_in, wqn, wqr, wuk, qg, kvg = wts
    row = lambda w: pl.BlockSpec((tm, w), lambda i: (i, 0))
    tab = pl.BlockSpec((tm, LANES), lambda i: (i % nrep, 0))
    outs = [
        (MLA_HEADS * MLA_QCAT, BF16),
        (MLA_QCAT, BF16),
        (KV_LORA, F32),
        (MLA_ROPE_DIM, F32),
        (MLA_W + DIFF_W, BF16),
        (DIFF_W, BF16),
        (DIFF_W, F32), (DIFF_W, BF16),
        (DIFF_W, F32), (DIFF_W, BF16),
    ]
    return pl.pallas_call(
        _even_in_kernel,
        grid=(t // tm,),
        in_specs=[row(D_MODEL), _const_spec(w_in.shape), _const_spec(wqn.shape), _const_spec(wqr.shape),
                  _const_spec(wuk.shape), _const_spec(qg.shape), _const_spec(kvg.shape)] + [tab] * 6,
        out_specs=[row(w) for w, _ in outs],
        out_shape=[jax.ShapeDtypeStruct((t, w), d) for w, d in outs],
        compiler_params=_params(("parallel",)),
        name="even_in",
    )(x2d, w_in, wqn, wqr, wuk, qg, kvg, *tabs)


def _softmax_step(s, vt, m_ref, l_ref, acc_ref, valid):
    if valid is not None:
        s = jnp.where(valid, s, NEG_INF)
    m_old = m_ref[...]
    m_new = jnp.maximum(m_old, jnp.max(s, axis=0, keepdims=True))
    alpha = jnp.exp2(m_old - m_new)
    p = jnp.exp2(s - m_new)
    l_ref[...] = alpha * l_ref[...] + jnp.sum(p, axis=0, keepdims=True)
    acc_ref[...] = alpha * acc_ref[...] + jnp.dot(vt, p.astype(BF16), preferred_element_type=F32)
    m_ref[...] = m_new


def _chunk_causal_flash(qs, get_k, get_vt, q0, tq, tk, s_ref, m_ref, l_ref, acc_ref):
    m_rows = qs.shape[0]
    m_ref[...] = jnp.full(m_ref.shape, NEG_INF, F32)
    l_ref[...] = jnp.zeros(l_ref.shape, F32)
    acc_ref[...] = jnp.zeros(acc_ref.shape, F32)
    n_full = q0 // tk

    def scores(j):
        return lax.dot_general(get_k(pl.multiple_of(j * tk, tk)), qs, _NT, preferred_element_type=F32)

    sa_ref, sb_ref = s_ref
    sa_ref[...] = scores(0)

    def step(j, cur_ref, nxt_ref):
        nxt_ref[...] = scores(j + 1)
        _softmax_step(cur_ref[...], get_vt(j), m_ref, l_ref, acc_ref, None)

    def body(jj, carry):
        step(2 * jj, sa_ref, sb_ref)
        step(2 * jj + 1, sb_ref, sa_ref)
        return carry

    lax.fori_loop(0, n_full // 2, body, 0)
    kpos = n_full * tk + lax.broadcasted_iota(jnp.int32, (tk, m_rows), 0)
    qpos = q0 + (lax.broadcasted_iota(jnp.int32, (tk, m_rows), 1) & (tq - 1))
    shift = CHUNK.bit_length() - 1
    valid = (kpos >> shift) <= (qpos >> shift)

    @pl.when(n_full % 2 == 1)
    def _():
        step(n_full - 1, sa_ref, sb_ref)
        _softmax_step(sb_ref[...], get_vt(n_full), m_ref, l_ref, acc_ref, valid)

    @pl.when(n_full % 2 == 0)
    def _():
        _softmax_step(sa_ref[...], get_vt(n_full), m_ref, l_ref, acc_ref, valid)


def _mla_prompt_kernel(q_ref, k_ref, vt_ref, o_ref, sa_ref, sb_ref, m_ref, l_ref, acc_ref, *, tq, tk):
    s_ref = (sa_ref, sb_ref)
    q0 = pl.program_id(1) * tq
    qs = jnp.concatenate([q_ref[0, :, h * MLA_QCAT:(h + 1) * MLA_QCAT] for h in range(MLA_HEADS)], axis=0)
    get_k = lambda k0: k_ref[0, pl.ds(k0, tk), :]
    get_vt = lambda j: vt_ref[0, j]
    _chunk_causal_flash(qs, get_k, get_vt, q0, tq, tk, s_ref, m_ref, l_ref, acc_ref)
    out = acc_ref[...] / l_ref[...]
    for h in range(MLA_HEADS):
        o_ref[0, :, h * KV_LORA:(h + 1) * KV_LORA] = out[:, h * tq:(h + 1) * tq].T.astype(o_ref.dtype)


def _key_major(x, tk):
    b, s, w = x.shape
    return x.reshape(b, s // tk, tk, w).transpose(0, 1, 3, 2)


def _mla_prompt(qcat, kcat, tq, tk):
    b, s, _ = qcat.shape
    tq, tk = min(tq, s), min(tk, s)
    m_rows = MLA_HEADS * tq
    vt = _key_major(kcat[:, :, :KV_LORA], tk)
    return pl.pallas_call(
        functools.partial(_mla_prompt_kernel, tq=tq, tk=tk),
        grid=(b, s // tq),
        in_specs=[pl.BlockSpec((1, tq, MLA_HEADS * MLA_QCAT), lambda bi, i: (bi, i, 0)),
                  pl.BlockSpec((1, s, MLA_QCAT), lambda bi, i: (bi, 0, 0)),
                  pl.BlockSpec((1, s // tk, KV_LORA, tk), lambda bi, i: (bi, 0, 0, 0))],
        out_specs=pl.BlockSpec((1, tq, MLA_HEADS * KV_LORA), lambda bi, i: (bi, i, 0)),
        out_shape=jax.ShapeDtypeStruct((b, s, MLA_HEADS * KV_LORA), BF16),
        scratch_shapes=[pltpu.VMEM((tk, m_rows), F32), pltpu.VMEM((tk, m_rows), F32), pltpu.VMEM((1, m_rows), F32),
                        pltpu.VMEM((1, m_rows), F32), pltpu.VMEM((KV_LORA, m_rows), F32)],
        compiler_params=_params(("parallel", "parallel")),
        name="mla_prompt",
    )(qcat, kcat, vt)


def _lane_halves(x):
    lane = lax.broadcasted_iota(jnp.int32, x.shape, 1)
    zero = jnp.zeros_like(x)
    return jnp.concatenate([jnp.where(lane < LANES // 2, x, zero), jnp.where(lane < LANES // 2, zero, x)], axis=0)


def _diff_lambda(lq_ref, lk_ref, lam_init):
    e = jnp.exp(jnp.sum(lq_ref[...] * lk_ref[...], axis=-1, keepdims=True))
    return e[0:1] - e[1:2] + lam_init


def _diff_prompt_kernel(q_ref, k_ref, vt_ref, lq_ref, lk_ref, o_ref, sa_ref, sb_ref, m_ref, l_ref, acc_ref,
                        *, tq, tk, lam_init):
    s_ref = (sa_ref, sb_ref)
    q0 = pl.program_id(1) * tq
    lam = _diff_lambda(lq_ref, lk_ref, lam_init)
    for h in range(DIFF_HEADS):
        sl = slice(h * LANES, (h + 1) * LANES)
        qs = _lane_halves(q_ref[0, :, sl])
        get_k = lambda k0, sl=sl: k_ref[0, pl.ds(k0, tk), sl]
        get_vt = lambda j, sl=sl: vt_ref[0, j, sl, :]
        _chunk_causal_flash(qs, get_k, get_vt, q0, tq, tk, s_ref, m_ref, l_ref, acc_ref)
        o = acc_ref[...] / l_ref[...]
        o_ref[0, :, sl] = (o[:, :tq] - lam * o[:, tq:]).T


def _diff_prompt(dq, dk, dv, lq, lk, lam_init, tq, tk):
    b, s, _ = dq.shape
    tq, tk = min(tq, s), min(tk, s)
    qspec = pl.BlockSpec((1, tq, DIFF_W), lambda bi, i: (bi, i, 0))
    return pl.pallas_call(
        functools.partial(_diff_prompt_kernel, tq=tq, tk=tk, lam_init=lam_init),
        grid=(b, s // tq),
        in_specs=[qspec, pl.BlockSpec((1, s, DIFF_W), lambda bi, i: (bi, 0, 0)),
                  pl.BlockSpec((1, s // tk, DIFF_W, tk), lambda bi, i: (bi, 0, 0, 0)),
                  _const_spec(lq.shape), _const_spec(lk.shape)],
        out_specs=qspec,
        out_shape=jax.ShapeDtypeStruct((b, s, DIFF_W), F32),
        scratch_shapes=[pltpu.VMEM((tk, 2 * tq), F32), pltpu.VMEM((tk, 2 * tq), F32), pltpu.VMEM((1, 2 * tq), F32),
                        pltpu.VMEM((1, 2 * tq), F32), pltpu.VMEM((LANES, 2 * tq), F32)],
        compiler_params=_params(("parallel", "parallel")),
        name="diff_prompt",
    )(dq, dk, _key_major(dv, tk), lq, lk)


def _pad_rows(x, rows):
    return jnp.concatenate([x, jnp.zeros((rows - x.shape[0], x.shape[1]), x.dtype)], axis=0)


def _decode_softmax(s_c, s_n, v_cache, v_new, n_new):
    col = lax.broadcasted_iota(jnp.int32, s_n.shape, 1)
    s_n = jnp.where(col < n_new, s_n, NEG_INF)
    m = jnp.maximum(jnp.max(s_c, axis=-1, keepdims=True), jnp.max(s_n, axis=-1, keepdims=True))
    p_c = jnp.exp2(s_c - m)
    p_n = jnp.exp2(s_n - m)
    l = jnp.sum(p_c, axis=-1, keepdims=True) + jnp.sum(p_n, axis=-1, keepdims=True)
    acc = (jnp.dot(p_c.astype(BF16), v_cache, preferred_element_type=F32)
           + jnp.dot(p_n.astype(BF16), v_new, preferred_element_type=F32))
    return acc / l


def _mla_decode_kernel(q_ref, ckv_ref, kr_ref, kn_ref, o_ref, *, n_new):
    qs = jnp.concatenate([q_ref[0, :, h * MLA_QCAT:(h + 1) * MLA_QCAT] for h in range(MLA_HEADS)], axis=0)
    ckv = ckv_ref[0].astype(BF16)
    kr = kr_ref[0].astype(BF16)
    k_new = _pad_rows(kn_ref[0], LANES)
    s_c = (lax.dot_general(qs[:, :KV_LORA], ckv, _NT, preferred_element_type=F32)
           + lax.dot_general(qs[:, KV_LORA:KV_LORA + MLA_ROPE_DIM], kr, _NT, preferred_element_type=F32))
    s_n = lax.dot_general(qs, k_new, _NT, preferred_element_type=F32)
    out = _decode_softmax(s_c, s_n, ckv, k_new[:, :KV_LORA], n_new)
    for h in range(MLA_HEADS):
        o_ref[0, :, h * KV_LORA:(h + 1) * KV_LORA] = out[h * n_new:(h + 1) * n_new].astype(o_ref.dtype)


def _mla_decode(qcat, ckv_c, kr_c, kcat_new):
    b, n, _ = qcat.shape
    past = ckv_c.shape[1]
    return pl.pallas_call(
        functools.partial(_mla_decode_kernel, n_new=n),
        grid=(b,),
        in_specs=[pl.BlockSpec((1, n, MLA_HEADS * MLA_QCAT), lambda i: (i, 0, 0)),
                  pl.BlockSpec((1, past, KV_LORA), lambda i: (i, 0, 0)),
                  pl.BlockSpec((1, past, MLA_ROPE_DIM), lambda i: (i, 0, 0)),
                  pl.BlockSpec((1, n, MLA_QCAT), lambda i: (i, 0, 0))],
        out_specs=pl.BlockSpec((1, n, MLA_HEADS * KV_LORA), lambda i: (i, 0, 0)),
        out_shape=jax.ShapeDtypeStruct((b, n, MLA_HEADS * KV_LORA), BF16),
        compiler_params=_params(("parallel",)),
        name="mla_decode",
    )(qcat, ckv_c, kr_c, kcat_new)


def _diff_decode_kernel(q_ref, kc_ref, vc_ref, kn_ref, vn_ref, lq_ref, lk_ref, o_ref, *, n_new, lam_init):
    lam = _diff_lambda(lq_ref, lk_ref, lam_init)
    qs = _lane_halves(q_ref[0])
    s_c = lax.dot_general(qs, kc_ref[0].astype(BF16), _NT, preferred_element_type=F32)
    s_n = lax.dot_general(qs, _pad_rows(kn_ref[0], LANES), _NT, preferred_element_type=F32)
    o = _decode_softmax(s_c, s_n, vc_ref[0].astype(BF16), _pad_rows(vn_ref[0], LANES), n_new)
    o_ref[0] = o[:n_new] - lam * o[n_new:]


def _diff_decode(dq, kc, vc, dk_new, dv_new, lq, lk, lam_init):
    b, n, _ = dq.shape
    past = kc.shape[1]
    new = pl.BlockSpec((1, n, LANES), lambda i, h: (i, 0, h))
    cache = pl.BlockSpec((1, past, LANES), lambda i, h: (i, 0, h))
    return pl.pallas_call(
        functools.partial(_diff_decode_kernel, n_new=n, lam_init=lam_init),
        grid=(b, DIFF_HEADS),
        in_specs=[new, cache, cache, new, new, _const_spec(lq.shape), _const_spec(lk.shape)],
        out_specs=new,
        out_shape=jax.ShapeDtypeStruct((b, n, DIFF_W), F32),
        compiler_params=_params(("parallel", "parallel")),
        name="diff_decode",
    )(dq, kc, vc, dk_new, dv_new, lq, lk)


def _even_out_kernel(olat_ref, odiff_ref, gate_ref, x_ref, wuv_ref, sg_ref, wout_ref, lng_ref, lnb_ref, y_ref,
                     *, lam_init):
    gate = gate_ref[...]
    o_mla = jnp.dot(olat_ref[...], wuv_ref[...], preferred_element_type=F32)
    parts = [(o_mla * gate[:, :MLA_W].astype(F32)).astype(BF16)]
    od = odiff_ref[...]
    for h in range(DIFF_HEADS):
        sl = slice(h * LANES, (h + 1) * LANES)
        n = _rms(od[:, sl], sg_ref[...]) * (1.0 - lam_init)
        parts.append((n * gate[:, MLA_W + h * LANES:MLA_W + (h + 1) * LANES].astype(F32)).astype(BF16))
    o = jnp.dot(jnp.concatenate(parts, axis=1), wout_ref[...], preferred_element_type=F32)
    y_ref[...] = _layernorm(ALPHA * x_ref[...] + o, lng_ref[...], lnb_ref[...])


def _even_out(olat, odiff, gate, x2d, wuv_bd, sg, wout, lng, lnb, lam_init, tm):
    t = x2d.shape[0]
    tm = min(tm, t)
    row = lambda w: pl.BlockSpec((tm, w), lambda i: (i, 0))
    return pl.pallas_call(
        functools.partial(_even_out_kernel, lam_init=lam_init),
        grid=(t // tm,),
        in_specs=[row(MLA_HEADS * KV_LORA), row(DIFF_W), row(MLA_W + DIFF_W), row(D_MODEL),
                  _const_spec(wuv_bd.shape), _const_spec(sg.shape), _const_spec(wout.shape),
                  _const_spec(lng.shape), _const_spec(lnb.shape)],
        out_specs=row(D_MODEL),
        out_shape=jax.ShapeDtypeStruct((t, D_MODEL), F32),
        compiler_params=_params(("parallel",)),
        name="even_out",
    )(olat, odiff, gate, x2d, wuv_bd, sg, wout, lng, lnb)


def _odd_in_kernel(x_ref, w_ref, q_ref, kf_ref, kb_ref, vf_ref, vb_ref, g_ref):
    xb = x_ref[...].astype(BF16)

    def proj(i):
        return jnp.dot(xb, w_ref[:, i * SB_W:(i + 1) * SB_W], preferred_element_type=F32)

    q_ref[...] = (proj(0) * (SB_HD ** -0.5 * LOG2E)).astype(BF16)
    k = proj(1)
    kf_ref[...] = k
    kb_ref[...] = k.astype(BF16)
    v = proj(2)
    vf_ref[...] = v
    vb_ref[...] = v.astype(BF16)
    g_ref[...] = _silu(proj(3)).astype(BF16)


def _odd_in(x2d, w_in, tm):
    t = x2d.shape[0]
    tm = min(tm, t)
    row = pl.BlockSpec((tm, SB_W), lambda i: (i, 0))
    dts = [BF16, F32, BF16, F32, BF16, BF16]
    return pl.pallas_call(
        _odd_in_kernel,
        grid=(t // tm,),
        in_specs=[pl.BlockSpec((tm, D_MODEL), lambda i: (i, 0)), _const_spec(w_in.shape)],
        out_specs=[row] * len(dts),
        out_shape=[jax.ShapeDtypeStruct((t, SB_W), d) for d in dts],
        compiler_params=_params(("parallel",)),
        name="odd_in",
    )(x2d, w_in)


SB_EXIT_LOG2 = 160.0


def _log2(x):
    return jnp.log(x) * LOG2E


def _later_matrix(n):
    return (lax.broadcasted_iota(jnp.int32, (n, n), 0) > lax.broadcasted_iota(jnp.int32, (n, n), 1)).astype(BF16)


def _softplus2(z):
    return jnp.maximum(z, 0.0) + _log2(1.0 + jnp.exp2(-jnp.abs(z)))


def _sb_prompt_kernel(q_ref, k_ref, vt_ref, o_ref, c_ref, acc_ref, x_ref, sp0_ref, *, tq, pairs):
    i = pl.program_id(2)
    ut = (lax.broadcasted_iota(jnp.int32, (tq, tq), 1) > lax.broadcasted_iota(jnp.int32, (tq, tq), 0)).astype(BF16)
    keys = lax.broadcasted_iota(jnp.int32, (tq, 2 * tq), 0)
    queries = lax.broadcasted_iota(jnp.int32, (tq, 2 * tq), 1) & (tq - 1)
    diag_valid = keys < queries
    half = LANES // 2
    sls = [slice(p * LANES, (p + 1) * LANES) for p in range(pairs)]
    qss = [_lane_halves(q_ref[0, :, sl]) for sl in sls]

    def all_pairs(j, valid, first):
        k0 = pl.multiple_of(j * tq, tq)
        zs = [lax.dot_general(k_ref[0, pl.ds(k0, tq), sl], qss[p], _NT, preferred_element_type=F32)
              for p, sl in enumerate(sls)]
        laters = []
        for p, z in enumerate(zs):
            sp = _softplus2(z)
            x_ref[p] = z - sp
            if valid is not None:
                sp = jnp.where(valid, sp, 0.0)
            spb = sp.astype(BF16)
            sp0_ref[p] = spb[:1, :].astype(F32)
            laters.append(jnp.dot(ut, spb, preferred_element_type=F32))
        cmin = None
        for p, sl in enumerate(sls):
            a = jnp.exp2(x_ref[p] - laters[p])
            if valid is not None:
                a = jnp.where(valid, a, 0.0)
            pv = jnp.dot(vt_ref[0, j, sl, :], a.astype(BF16), preferred_element_type=F32)
            tot = laters[p][:1, :] + sp0_ref[p]
            if first:
                acc_ref[p] = pv
                c = tot
            else:
                c0 = c_ref[p]
                acc_ref[p] += pv * jnp.exp2(-c0)
                c = c0 + tot
            c_ref[p] = c
            cmin = c if cmin is None else jnp.minimum(cmin, c)
        return jnp.min(cmin)

    cmin = all_pairs(i, diag_valid, True)
    lax.while_loop(lambda jc: (jc[0] >= 0) & (jc[1] < SB_EXIT_LOG2),
                   lambda jc: (jc[0] - 1, all_pairs(jc[0], None, False)),
                   (i - 1, cmin))

    for p, sl in enumerate(sls):
        acc = acc_ref[p]
        o_t = jnp.concatenate([acc[:half, :tq], acc[half:, tq:]], axis=0)
        o_ref[0, :, sl] = o_t.T.astype(o_ref.dtype)


def _sb_prompt(q, k, v, tq, pairs):
    b, s, w = q.shape
    tq = min(tq, s)
    gw = pairs * LANES
    qspec = pl.BlockSpec((1, tq, gw), lambda bi, g, i: (bi, i, g))
    return pl.pallas_call(
        functools.partial(_sb_prompt_kernel, tq=tq, pairs=pairs),
        grid=(b, w // gw, s // tq),
        in_specs=[qspec, pl.BlockSpec((1, s, gw), lambda bi, g, i: (bi, 0, g)),
                  pl.BlockSpec((1, s // tq, gw, tq), lambda bi, g, i: (bi, 0, g, 0))],
        out_specs=qspec,
        out_shape=jax.ShapeDtypeStruct((b, s, w), BF16),
        scratch_shapes=[pltpu.VMEM((pairs, 1, 2 * tq), F32), pltpu.VMEM((pairs, LANES, 2 * tq), F32),
                        pltpu.VMEM((pairs, tq, 2 * tq), F32), pltpu.VMEM((pairs, 1, 2 * tq), F32)],
        compiler_params=_params(("parallel", "parallel", "parallel")),
        name="sb_prompt",
    )(q, k, _key_major(v, tq))


def _sb_rows_block(qss, kbs, vbs, u, cs, valid):
    zs = [lax.dot_general(qs, kb, _NT, preferred_element_type=F32) for qs, kb in zip(qss, kbs)]
    xs, sp0s, laters = [], [], []
    for z in zs:
        sp = _softplus2(z)
        xs.append(z - sp)
        if valid is not None:
            sp = jnp.where(valid, sp, 0.0)
        spb = sp.astype(BF16)
        sp0s.append(spb[:, :1].astype(F32))
        laters.append(jnp.dot(spb, u, preferred_element_type=F32))
    out = []
    for x, sp0, later, c, vb in zip(xs, sp0s, laters, cs, vbs):
        a = jnp.exp2(x - later - c)
        if valid is not None:
            a = jnp.where(valid, a, 0.0)
        out.append((jnp.dot(a.astype(BF16), vb, preferred_element_type=F32), c + later[:, :1] + sp0))
    return out


def _sb_decode_recent_kernel(q_ref, kw_ref, vw_ref, kn_ref, vn_ref, acc_ref, c_ref, *, n_new, tk, pairs):
    sls = [slice(p * LANES, (p + 1) * LANES) for p in range(pairs)]
    qss = [_lane_halves(q_ref[0, :, sl]) for sl in sls]
    rows = lax.broadcasted_iota(jnp.int32, (2 * n_new, LANES), 0) & (n_new - 1)
    cols = lax.broadcasted_iota(jnp.int32, (2 * n_new, LANES), 1)
    zero = jnp.zeros((2 * n_new, 1), F32)
    state = _sb_rows_block(qss, [_pad_rows(kn_ref[0, :, sl], LANES) for sl in sls],
                           [_pad_rows(vn_ref[0, :, sl], LANES) for sl in sls], _later_matrix(LANES),
                           [zero] * pairs, cols < rows)
    u = _later_matrix(tk)
    for j in reversed(range(kw_ref.shape[1] // tk)):
        blk = _sb_rows_block(qss, [kw_ref[0, j * tk:(j + 1) * tk, sl].astype(BF16) for sl in sls],
                             [vw_ref[0, j * tk:(j + 1) * tk, sl].astype(BF16) for sl in sls], u,
                             [c for _, c in state], None)
        state = [(acc + pv, c) for (acc, _), (pv, c) in zip(state, blk)]
    for p, (acc, c) in enumerate(state):
        acc_ref[0, p] = acc
        c_ref[0, p] = jnp.broadcast_to(c, (2 * n_new, LANES))


def _sb_decode_rest_kernel(q_ref, kc_ref, vc_ref, acc_in_ref, c_in_ref, acc_ref, *, tk):
    qs = _lane_halves(q_ref[0])
    acc = acc_in_ref[0, 0]
    c = c_in_ref[0, 0][:, :1]
    u = _later_matrix(tk)
    for j in reversed(range(kc_ref.shape[1] // tk)):
        def visit(acc, c, j=j):
            (pv, c), = _sb_rows_block([qs], [kc_ref[0, j * tk:(j + 1) * tk, :].astype(BF16)],
                                      [vc_ref[0, j * tk:(j + 1) * tk, :].astype(BF16)], u, [c], None)
            return acc + pv, c

        acc, c = lax.cond(jnp.min(c) < SB_EXIT_LOG2, visit, lambda acc, c: (acc, c), acc, c)
    acc_ref[0, 0] = acc


def _sb_decode_finish_kernel(acc_ref, o_ref, *, n_new, pairs):
    lane = lax.broadcasted_iota(jnp.int32, (n_new, LANES), 1)
    for p in range(pairs):
        acc = acc_ref[0, p]
        o_ref[0, :, p * LANES:(p + 1) * LANES] = jnp.where(lane < LANES // 2, acc[:n_new], acc[n_new:]).astype(o_ref.dtype)


def _sb_decode(q, kc4, vc4, k_new, v_new, tk, window):
    b, n, w = q.shape
    past = kc4.shape[1]
    window = min(window, past)
    tk = min(tk, window)
    pairs = w // LANES
    rest = past - window
    state_shape = jax.ShapeDtypeStruct((b, pairs, 2 * n, LANES), F32)
    new = pl.BlockSpec((1, n, w), lambda i: (i, 0, 0))
    win = pl.BlockSpec((1, window, w), lambda i: (i, 0, 0))
    state = pl.BlockSpec((1, pairs, 2 * n, LANES), lambda i: (i, 0, 0, 0))
    acc, c = pl.pallas_call(
        functools.partial(_sb_decode_recent_kernel, n_new=n, tk=tk, pairs=pairs),
        grid=(b,),
        in_specs=[new, win, win, new, new],
        out_specs=[state, state],
        out_shape=[state_shape, state_shape],
        compiler_params=_params(("parallel",)),
        name="sb_decode_recent",
    )(q, kc4[:, rest:].reshape(b, window, w), vc4[:, rest:].reshape(b, window, w), k_new, v_new)

    if rest:
        def older(acc, c):
            new1 = pl.BlockSpec((1, n, LANES), lambda i, g: (i, 0, g))
            cache = pl.BlockSpec((1, rest, LANES), lambda i, g: (i, 0, g))
            state1 = pl.BlockSpec((1, 1, 2 * n, LANES), lambda i, g: (i, g, 0, 0))
            return pl.pallas_call(
                functools.partial(_sb_decode_rest_kernel, tk=math.gcd(tk, rest)),
                grid=(b, pairs),
                in_specs=[new1, cache, cache, state1, state1],
                out_specs=state1,
                out_shape=state_shape,
                compiler_params=_params(("parallel", "parallel")),
                name="sb_decode_rest",
            )(q, kc4[:, :rest].reshape(b, rest, w), vc4[:, :rest].reshape(b, rest, w), acc, c)

        acc = lax.cond(jnp.min(c) < SB_EXIT_LOG2, older, lambda acc, c: acc, acc, c)

    return pl.pallas_call(
        functools.partial(_sb_decode_finish_kernel, n_new=n, pairs=pairs),
        grid=(b,),
        in_specs=[state],
        out_specs=new,
        out_shape=jax.ShapeDtypeStruct((b, n, w), BF16),
        compiler_params=_params(("parallel",)),
        name="sb_decode_finish",
    )(acc)


def _odd_out_kernel(o_ref, g_ref, x_ref, wout_ref, lng_ref, lnb_ref, y_ref):
    a = (o_ref[...].astype(F32) * g_ref[...].astype(F32)).astype(BF16)
    o = jnp.dot(a, wout_ref[...], preferred_element_type=F32)
    y_ref[...] = _layernorm(ALPHA * x_ref[...] + o, lng_ref[...], lnb_ref[...])


def _odd_out(o, gate, x2d, wout, lng, lnb, tm):
    t = x2d.shape[0]
    tm = min(tm, t)
    row = lambda w: pl.BlockSpec((tm, w), lambda i: (i, 0))
    return pl.pallas_call(
        _odd_out_kernel,
        grid=(t // tm,),
        in_specs=[row(SB_W), row(SB_W), row(D_MODEL), _const_spec(wout.shape),
                  _const_spec(lng.shape), _const_spec(lnb.shape)],
        out_specs=row(D_MODEL),
        out_shape=jax.ShapeDtypeStruct((t, D_MODEL), F32),
        compiler_params=_params(("parallel",)),
        name="odd_out",
    )(o, gate, x2d, wout, lng, lnb)


def _even_weights(w_in, q_g, w_uq, kv_g, w_uk):
    pad = jnp.zeros((D_MODEL, LANES - MLA_ROPE_DIM), w_in.dtype)
    o = np.cumsum([0, Q_LORA, KV_LORA, MLA_ROPE_DIM, MLA_W, DIFF_W, DIFF_W, DIFF_W, DIFF_W])
    w_re = jnp.concatenate([w_in[:, o[0]:o[3]], pad, w_in[:, o[3]:]], axis=1).astype(BF16)
    w_uq3 = w_uq.reshape(Q_LORA, MLA_HEADS, MLA_QK_DIM)
    wqn = w_uq3[:, :, :MLA_NOPE_DIM].reshape(Q_LORA, MLA_HEADS * MLA_NOPE_DIM).astype(BF16)
    wqr = jnp.pad(w_uq3[:, :, MLA_NOPE_DIM:], ((0, 0), (0, 0), (0, LANES - MLA_ROPE_DIM)))
    wqr = wqr.reshape(Q_LORA, MLA_HEADS * LANES).astype(BF16)
    eye = jnp.eye(MLA_HEADS, dtype=w_uk.dtype)
    wuk_bd = jnp.einsum('chn,hg->hngc', w_uk, eye).reshape(MLA_HEADS * MLA_NOPE_DIM, MLA_HEADS * KV_LORA).astype(BF16)
    return w_re, wqn, wqr, wuk_bd, q_g.reshape(1, Q_LORA), kv_g.reshape(1, KV_LORA)


def _uv_block_diag(w_uv):
    eye = jnp.eye(MLA_HEADS, dtype=w_uv.dtype)
    return jnp.einsum('chv,hg->hcgv', w_uv, eye).reshape(MLA_HEADS * KV_LORA, MLA_HEADS * MLA_V_DIM).astype(BF16)


TM_PROJ = 512
TQ_SOFTMAX = 128
TK_SOFTMAX = 512
TQ_SB = 256
SB_PAIRS = 8
TK_SB_DECODE = 256
SB_DECODE_WINDOW = 256


def kernel(x_prompt, x_sample, cache_mla_ckv, cache_mla_krope, cache_diff_k, cache_diff_v, cache_sb_k, cache_sb_v,
           w_in_even, q_norm_g, w_uq, kv_norm_g, w_uk, w_uv, diff_lambda_q, diff_lambda_k, diff_subln_g,
           w_out_even, ln_even_g, ln_even_b, w_in_odd, w_out_odd, ln_odd_g, ln_odd_b):
    bp, sp, _ = x_prompt.shape
    bs, ss, _ = x_sample.shape
    past = cache_sb_k.shape[2]
    if past % CHUNK or ss > CHUNK or (ss & (ss - 1)) or sp % TQ_SB:
        raise NotImplementedError("decode frames must share one chunk after a chunk-aligned cache")
    tp, ts = bp * sp, bs * ss
    pos_p = jnp.arange(sp, dtype=jnp.int32)
    pos_s = past + jnp.arange(ss, dtype=jnp.int32)
    pos_s_rows = jnp.tile(pos_s, bs)

    def tables(pos):
        mc, msa, msb, _ = _rope_tables(pos, MLA_ROPE_DIM, MLA_ROPE_THETA, LANES)
        dc, dsa, dsb, _ = _rope_tables(pos, DIFF_ROT, ROPE_THETA, DIFF_HD)
        return mc, msa, msb, dc, dsa, dsb

    yp = x_prompt.reshape(tp, D_MODEL)
    ys = x_sample.reshape(ts, D_MODEL)
    even_p, even_s, odd_p, odd_s = [], [], [], []
    for layer in range(DEPTH):
        i = layer // 2
        if layer % 2 == 0:
            lam_init = 0.8 - 0.6 * math.exp(-0.3 * layer)
            wts = _even_weights(w_in_even[i], q_norm_g[i], w_uq[i], kv_norm_g[i], w_uk[i])
            wuv_bd = _uv_block_diag(w_uv[i])
            sg = diff_subln_g[i].reshape(1, 2 * DIFF_HD)
            wout = w_out_even[i].astype(BF16)
            lng, lnb = ln_even_g[i].reshape(1, D_MODEL), ln_even_b[i].reshape(1, D_MODEL)
            lq, lk = diff_lambda_q[i], diff_lambda_k[i]
            qcat, kcat, ckv, krope, gate, dq, dkf, dkb, dvf, dvb = _even_in(yp, wts, tables(pos_p), sp, TM_PROJ)
            olat = _mla_prompt(qcat.reshape(bp, sp, -1), kcat.reshape(bp, sp, -1), TQ_SOFTMAX, TK_SOFTMAX)
            odiff = _diff_prompt(dq.reshape(bp, sp, -1), dkb.reshape(bp, sp, -1), dvb.reshape(bp, sp, -1),
                                 lq, lk, lam_init, 2 * TQ_SOFTMAX, TK_SOFTMAX)
            yp_new = _even_out(olat.reshape(tp, -1), odiff.reshape(tp, -1), gate, yp, wuv_bd, sg, wout, lng, lnb,
                               lam_init, TM_PROJ)
            even_p.append((ckv.reshape(bp, sp, KV_LORA), krope.reshape(bp, sp, MLA_ROPE_DIM),
                           dkf.reshape(bp, sp, DIFF_HEADS, 2, DIFF_HD), dvf.reshape(bp, sp, DIFF_HEADS, 2 * DIFF_HD)))
            qcat, kcat, ckv, krope, gate, dq, dkf, dkb, dvf, dvb = _even_in(ys, wts, tables(pos_s_rows), ts, TM_PROJ)
            olat = _mla_decode(qcat.reshape(bs, ss, -1), cache_mla_ckv[i], cache_mla_krope[i],
                               kcat.reshape(bs, ss, -1))
            odiff = _diff_decode(dq.reshape(bs, ss, -1), cache_diff_k[i].reshape(bs, past, DIFF_W),
                                 cache_diff_v[i].reshape(bs, past, DIFF_W), dkb.reshape(bs, ss, -1),
                                 dvb.reshape(bs, ss, -1), lq, lk, lam_init)
            ys_new = _even_out(olat.reshape(ts, -1), odiff.reshape(ts, -1), gate, ys, wuv_bd, sg, wout, lng, lnb,
                               lam_init, TM_PROJ)
            even_s.append((ckv.reshape(bs, ss, KV_LORA), krope.reshape(bs, ss, MLA_ROPE_DIM),
                           dkf.reshape(bs, ss, DIFF_HEADS, 2, DIFF_HD), dvf.reshape(bs, ss, DIFF_HEADS, 2 * DIFF_HD)))
            yp, ys = yp_new, ys_new
        else:
            w_in = w_in_odd[i].astype(BF16)
            wout = w_out_odd[i].astype(BF16)
            lng, lnb = ln_odd_g[i].reshape(1, D_MODEL), ln_odd_b[i].reshape(1, D_MODEL)
            q, kf, kb, vf, vb, gate = _odd_in(yp, w_in, TM_PROJ)
            o = _sb_prompt(q.reshape(bp, sp, -1), kb.reshape(bp, sp, -1), vb.reshape(bp, sp, -1), TQ_SB, SB_PAIRS)
            yp_new = _odd_out(o.reshape(tp, -1), gate, yp, wout, lng, lnb, TM_PROJ)
            odd_p.append((kf.reshape(bp, sp, SB_HEADS, SB_HD), vf.reshape(bp, sp, SB_HEADS, SB_HD)))
            q, kf, kb, vf, vb, gate = _odd_in(ys, w_in, TM_PROJ)
            o = _sb_decode(q.reshape(bs, ss, -1), cache_sb_k[i], cache_sb_v[i], kb.reshape(bs, ss, -1),
                           vb.reshape(bs, ss, -1), TK_SB_DECODE, SB_DECODE_WINDOW)
            ys_new = _odd_out(o.reshape(ts, -1), gate, ys, wout, lng, lnb, TM_PROJ)
            odd_s.append((kf.reshape(bs, ss, SB_HEADS, SB_HD), vf.reshape(bs, ss, SB_HEADS, SB_HD)))
            yp, ys = yp_new, ys_new
    stack = lambda lst, k: jnp.stack([t[k] for t in lst])
    return (yp.reshape(bp, sp, D_MODEL), ys.reshape(bs, ss, D_MODEL),
            stack(even_p, 0), stack(even_p, 1), stack(even_p, 2), stack(even_p, 3),
            stack(odd_p, 0), stack(odd_p, 1),
            stack(even_s, 0), stack(even_s, 1), stack(even_s, 2), stack(even_s, 3),
            stack(odd_s, 0), stack(odd_s, 1))
```

```python
import functools
import math

import numpy as np
import jax
import jax.numpy as jnp
from jax import lax
from jax.experimental import pallas as pl
from jax.experimental.pallas import tpu as pltpu

F32 = jnp.float32
BF16 = jnp.bfloat16

D_MODEL = 1024
DEPTH = 2
CHUNK = 64
ALPHA = (2 * DEPTH) ** 0.25
EPS = 1e-5
NEG_INF = -1e30
MLA_HEADS = 8
Q_LORA = 256
KV_LORA = 128
MLA_NOPE_DIM = 64
MLA_ROPE_DIM = 32
MLA_QK_DIM = MLA_NOPE_DIM + MLA_ROPE_DIM
MLA_V_DIM = 64
MLA_ROPE_THETA = 10000.0
MLA_W = MLA_HEADS * MLA_V_DIM
DIFF_HEADS = 4
DIFF_HD = 64
DIFF_ROT = DIFF_HD // 4
ROPE_THETA = 500000.0
DIFF_W = DIFF_HEADS * 2 * DIFF_HD
SB_HEADS = 16
SB_HD = 64
SB_W = SB_HEADS * SB_HD

LANES = 128
MLA_QCAT = 2 * LANES
VMEM_LIMIT = 56 * 1024 * 1024

_NT = (((1,), (1,)), ((), ()))
LOG2E = math.log2(math.e)


def _params(sem):
    return pltpu.CompilerParams(dimension_semantics=sem, vmem_limit_bytes=VMEM_LIMIT)


def _const_spec(shape):
    nd = len(shape)
    return pl.BlockSpec(shape, lambda *_: (0,) * nd, pipeline_mode=pl.Buffered(1))


def _rope_tables(pos, rot_dim, theta, period):
    half = rot_dim // 2
    inv = jnp.exp(-math.log(theta) * jnp.arange(half, dtype=F32) * (2.0 / rot_dim))
    ang = pos.astype(F32)[:, None] * inv[None, :]
    cos, sin = jnp.cos(ang), jnp.sin(ang)
    lane = np.arange(LANES) % period
    idx = lane % half
    in_rot = lane < rot_dim
    first = lane < half
    second = in_rot & ~first
    cos_l, sin_l = cos[:, idx], sin[:, idx]
    c = jnp.where(in_rot[None, :], cos_l, 1.0)
    sa = jnp.where(first[None, :], -sin_l, 0.0)
    sb = jnp.where(second[None, :], sin_l, 0.0)
    return c, sa, sb, half


def _apply_rope(x, c, sa, sb, half):
    return x * c + pltpu.roll(x, LANES - half, 1) * sa + pltpu.roll(x, half, 1) * sb


def _silu(x):
    return x / (1.0 + jnp.exp(-x))


def _rms(x, g):
    ms = jnp.mean(x * x, axis=-1, keepdims=True)
    return x * lax.rsqrt(ms + EPS) * g


def _layernorm(r, g, b):
    mu = jnp.mean(r, axis=-1, keepdims=True)
    rc = r - mu
    var = jnp.mean(rc * rc, axis=-1, keepdims=True)
    return rc * lax.rsqrt(var + EPS) * g + b


_C_QLAT = 0
_C_CKV = Q_LORA
_C_KROPE = _C_CKV + KV_LORA
_C_GMLA = _C_KROPE + LANES
_C_DQ = _C_GMLA + MLA_W
_C_DK = _C_DQ + DIFF_W
_C_DV = _C_DK + DIFF_W
_C_GDIFF = _C_DV + DIFF_W
_EVEN_COLS = _C_GDIFF + DIFF_W


def _even_in_kernel(x_ref, w_ref, wqn_ref, wqr_ref, wuk_ref, qg_ref, kvg_ref,
                    mc_ref, msa_ref, msb_ref, dc_ref, dsa_ref, dsb_ref,
                    qcat_ref, kcat_ref, ckv_ref, krope_ref, gate_ref, dq_ref,
                    dkf_ref, dkb_ref, dvf_ref, dvb_ref):
    xb = x_ref[...].astype(BF16)

    def proj(c0, c1):
        return jnp.dot(xb, w_ref[:, c0:c1], preferred_element_type=F32)

    mla_half = MLA_ROPE_DIM // 2
    diff_half = DIFF_ROT // 2
    mc, msa, msb = mc_ref[...], msa_ref[...], msb_ref[...]
    dc, dsa, dsb = dc_ref[...], dsa_ref[...], dsb_ref[...]

    qn = _rms(proj(_C_QLAT, _C_QLAT + Q_LORA), qg_ref[...]).astype(BF16)
    q_nope = jnp.dot(qn, wqn_ref[...], preferred_element_type=F32).astype(BF16)
    q_abs = jnp.dot(q_nope, wuk_ref[...], preferred_element_type=F32)
    q_rp = jnp.dot(qn, wqr_ref[...], preferred_element_type=F32)
    scale = MLA_QK_DIM ** -0.5 * LOG2E
    for h in range(MLA_HEADS):
        rot = _apply_rope(q_rp[:, h * LANES:(h + 1) * LANES], mc, msa, msb, mla_half)
        qcat_ref[:, h * MLA_QCAT:h * MLA_QCAT + LANES] = (q_abs[:, h * LANES:(h + 1) * LANES] * scale).astype(BF16)
        qcat_ref[:, h * MLA_QCAT + LANES:(h + 1) * MLA_QCAT] = (rot * scale).astype(BF16)

    kv = proj(_C_CKV, _C_GMLA)
    ckv = _rms(kv[:, :KV_LORA], kvg_ref[...])
    krope = _apply_rope(kv[:, KV_LORA:], mc, msa, msb, mla_half)
    ckv_ref[...] = ckv
    krope_ref[...] = krope[:, :MLA_ROPE_DIM]
    kcat_ref[:, :LANES] = ckv.astype(BF16)
    kcat_ref[:, LANES:] = krope.astype(BF16)

    gate_ref[:, :MLA_W] = _silu(proj(_C_GMLA, _C_DQ)).astype(BF16)
    gate_ref[:, MLA_W:] = _silu(proj(_C_GDIFF, _EVEN_COLS)).astype(BF16)

    dq = proj(_C_DQ, _C_DK)
    dk = proj(_C_DK, _C_DV)
    dv = proj(_C_DV, _C_GDIFF)
    dscale = DIFF_HD ** -0.5 * LOG2E
    tm = x_ref.shape[0]
    for h in range(DIFF_HEADS):
        sl = slice(h * LANES, (h + 1) * LANES)
        dq_ref[:, sl] = (_apply_rope(dq[:, sl], dc, dsa, dsb, diff_half) * dscale).astype(BF16)
        kr = _apply_rope(dk[:, sl], dc, dsa, dsb, diff_half)
        dkf_ref[pl.ds(2 * h, tm, stride=2 * DIFF_HEADS), :] = kr[:, :DIFF_HD]
        dkf_ref[pl.ds(2 * h + 1, tm, stride=2 * DIFF_HEADS), :] = kr[:, DIFF_HD:]
        dkb_ref[:, sl] = kr.astype(BF16)
        dvf_ref[pl.ds(h, tm, stride=DIFF_HEADS), :] = dv[:, sl]
    dvb_ref[...] = dv.astype(BF16)


def _even_in(x2d, wts, tabs, tab_rows_per_rep, tm):
    t = x2d.shape[0]
    tm = min(tm, t)
    nrep = tab_rows_per_rep // tm
    w_in, wqn, wqr, wuk, qg, kvg = wts
    row = lambda w: pl.BlockSpec((tm, w), lambda i: (i, 0))
    tab = pl.BlockSpec((tm, LANES), lambda i: (i % nrep, 0))
    outs = [
        (1, MLA_HEADS * MLA_QCAT, BF16),
        (1, MLA_QCAT, BF16),
        (1, KV_LORA, F32),
        (1, MLA_ROPE_DIM, F32),
        (1, MLA_W + DIFF_W, BF16),
        (1, DIFF_W, BF16),
        (2 * DIFF_HEADS, DIFF_HD, F32),
        (1, DIFF_W, BF16),
        (DIFF_HEADS, 2 * DIFF_HD, F32),
        (1, DIFF_W, BF16),
    ]
    return pl.pallas_call(
        _even_in_kernel,
        grid=(t // tm,),
        in_specs=[row(D_MODEL), _const_spec(w_in.shape), _const_spec(wqn.shape), _const_spec(wqr.shape),
                  _const_spec(wuk.shape), _const_spec(qg.shape), _const_spec(kvg.shape)] + [tab] * 6,
        out_specs=[pl.BlockSpec((tm * r, w), lambda i: (i, 0)) for r, w, _ in outs],
        out_shape=[jax.ShapeDtypeStruct((t * r, w), d) for r, w, d in outs],
        compiler_params=_params(("parallel",)),
        name="even_in",
    )(x2d, w_in, wqn, wqr, wuk, qg, kvg, *tabs)


def _softmax_step(s, vt, m_ref, l_ref, acc_ref, valid):
    if valid is not None:
        s = jnp.where(valid, s, NEG_INF)
    m_old = m_ref[...]
    m_new = jnp.maximum(m_old, jnp.max(s, axis=0, keepdims=True))
    alpha = jnp.exp2(m_old - m_new)
    p = jnp.exp2(s - m_new)
    l_ref[...] = alpha * l_ref[...] + jnp.sum(p, axis=0, keepdims=True)
    acc_ref[...] = alpha * acc_ref[...] + jnp.dot(vt, p.astype(BF16), preferred_element_type=F32)
    m_ref[...] = m_new


def _chunk_causal_flash(qs, get_k, get_vt, q0, tq, tk, s_ref, m_ref, l_ref, acc_ref):
    m_rows = qs.shape[0]
    m_ref[...] = jnp.full(m_ref.shape, NEG_INF, F32)
    l_ref[...] = jnp.zeros(l_ref.shape, F32)
    acc_ref[...] = jnp.zeros(acc_ref.shape, F32)
    n_full = q0 // tk

    def scores(j):
        return lax.dot_general(get_k(pl.multiple_of(j * tk, tk)), qs, _NT, preferred_element_type=F32)

    sa_ref, sb_ref = s_ref
    sa_ref[...] = scores(0)

    def step(j, cur_ref, nxt_ref):
        nxt_ref[...] = scores(j + 1)
        _softmax_step(cur_ref[...], get_vt(j), m_ref, l_ref, acc_ref, None)

    def body(jj, carry):
        step(2 * jj, sa_ref, sb_ref)
        step(2 * jj + 1, sb_ref, sa_ref)
        return carry

    lax.fori_loop(0, n_full // 2, body, 0)
    kpos = n_full * tk + lax.broadcasted_iota(jnp.int32, (tk, m_rows), 0)
    qpos = q0 + (lax.broadcasted_iota(jnp.int32, (tk, m_rows), 1) & (tq - 1))
    shift = CHUNK.bit_length() - 1
    valid = (kpos >> shift) <= (qpos >> shift)

    @pl.when(n_full % 2 == 1)
    def _():
        step(n_full - 1, sa_ref, sb_ref)
        _softmax_step(sb_ref[...], get_vt(n_full), m_ref, l_ref, acc_ref, valid)

    @pl.when(n_full % 2 == 0)
    def _():
        _softmax_step(sa_ref[...], get_vt(n_full), m_ref, l_ref, acc_ref, valid)


def _mla_prompt_kernel(q_ref, k_ref, vt_ref, o_ref, sa_ref, sb_ref, m_ref, l_ref, acc_ref, *, tq, tk):
    s_ref = (sa_ref, sb_ref)
    q0 = pl.program_id(1) * tq
    qs = jnp.concatenate([q_ref[0, :, h * MLA_QCAT:(h + 1) * MLA_QCAT] for h in range(MLA_HEADS)], axis=0)
    get_k = lambda k0: k_ref[0, pl.ds(k0, tk), :]
    get_vt = lambda j: vt_ref[0, j]
    _chunk_causal_flash(qs, get_k, get_vt, q0, tq, tk, s_ref, m_ref, l_ref, acc_ref)
    out = acc_ref[...] / l_ref[...]
    for h in range(MLA_HEADS):
        o_ref[0, :, h * KV_LORA:(h + 1) * KV_LORA] = out[:, h * tq:(h + 1) * tq].T.astype(o_ref.dtype)


def _key_major(x, tk):
    b, s, w = x.shape
    return x.reshape(b, s // tk, tk, w).transpose(0, 1, 3, 2)


def _mla_prompt(qcat, kcat, tq, tk):
    b, s, _ = qcat.shape
    tq, tk = min(tq, s), min(tk, s)
    m_rows = MLA_HEADS * tq
    vt = _key_major(kcat[:, :, :KV_LORA], tk)
    return pl.pallas_call(
        functools.partial(_mla_prompt_kernel, tq=tq, tk=tk),
        grid=(b, s // tq),
        in_specs=[pl.BlockSpec((1, tq, MLA_HEADS * MLA_QCAT), lambda bi, i: (bi, i, 0)),
                  pl.BlockSpec((1, s, MLA_QCAT), lambda bi, i: (bi, 0, 0)),
                  pl.BlockSpec((1, s // tk, KV_LORA, tk), lambda bi, i: (bi, 0, 0, 0))],
        out_specs=pl.BlockSpec((1, tq, MLA_HEADS * KV_LORA), lambda bi, i: (bi, i, 0)),
        out_shape=jax.ShapeDtypeStruct((b, s, MLA_HEADS * KV_LORA), BF16),
        scratch_shapes=[pltpu.VMEM((tk, m_rows), F32), pltpu.VMEM((tk, m_rows), F32), pltpu.VMEM((1, m_rows), F32),
                        pltpu.VMEM((1, m_rows), F32), pltpu.VMEM((KV_LORA, m_rows), F32)],
        compiler_params=_params(("parallel", "parallel")),
        name="mla_prompt",
    )(qcat, kcat, vt)


def _lane_halves(x):
    lane = lax.broadcasted_iota(jnp.int32, x.shape, 1)
    zero = jnp.zeros_like(x)
    return jnp.concatenate([jnp.where(lane < LANES // 2, x, zero), jnp.where(lane < LANES // 2, zero, x)], axis=0)


def _diff_lambda(lq_ref, lk_ref, lam_init):
    e = jnp.exp(jnp.sum(lq_ref[...] * lk_ref[...], axis=-1, keepdims=True))
    return e[0:1] - e[1:2] + lam_init


def _diff_prompt_kernel(q_ref, k_ref, vt_ref, lq_ref, lk_ref, o_ref, sa_ref, sb_ref, m_ref, l_ref, acc_ref,
                        *, tq, tk, lam_init):
    s_ref = (sa_ref, sb_ref)
    q0 = pl.program_id(1) * tq
    lam = _diff_lambda(lq_ref, lk_ref, lam_init)
    for h in range(DIFF_HEADS):
        sl = slice(h * LANES, (h + 1) * LANES)
        qs = _lane_halves(q_ref[0, :, sl])
        get_k = lambda k0, sl=sl: k_ref[0, pl.ds(k0, tk), sl]
        get_vt = lambda j, sl=sl: vt_ref[0, j, sl, :]
        _chunk_causal_flash(qs, get_k, get_vt, q0, tq, tk, s_ref, m_ref, l_ref, acc_ref)
        o = acc_ref[...] / l_ref[...]
        o_ref[0, :, sl] = (o[:, :tq] - lam * o[:, tq:]).T


def _diff_prompt(dq, dk, dv, lq, lk, lam_init, tq, tk):
    b, s, _ = dq.shape
    tq, tk = min(tq, s), min(tk, s)
    qspec = pl.BlockSpec((1, tq, DIFF_W), lambda bi, i: (bi, i, 0))
    return pl.pallas_call(
        functools.partial(_diff_prompt_kernel, tq=tq, tk=tk, lam_init=lam_init),
        grid=(b, s // tq),
        in_specs=[qspec, pl.BlockSpec((1, s, DIFF_W), lambda bi, i: (bi, 0, 0)),
                  pl.BlockSpec((1, s // tk, DIFF_W, tk), lambda bi, i: (bi, 0, 0, 0)),
                  _const_spec(lq.shape), _const_spec(lk.shape)],
        out_specs=qspec,
        out_shape=jax.ShapeDtypeStruct((b, s, DIFF_W), F32),
        scratch_shapes=[pltpu.VMEM((tk, 2 * tq), F32), pltpu.VMEM((tk, 2 * tq), F32), pltpu.VMEM((1, 2 * tq), F32),
                        pltpu.VMEM((1, 2 * tq), F32), pltpu.VMEM((LANES, 2 * tq), F32)],
        compiler_params=_params(("parallel", "parallel")),
        name="diff_prompt",
    )(dq, dk, _key_major(dv, tk), lq, lk)


def _pad_rows(x, rows):
    return jnp.concatenate([x, jnp.zeros((rows - x.shape[0], x.shape[1]), x.dtype)], axis=0)


def _decode_softmax(s_c, s_n, v_cache, v_new, n_new):
    col = lax.broadcasted_iota(jnp.int32, s_n.shape, 1)
    s_n = jnp.where(col < n_new, s_n, NEG_INF)
    m = jnp.maximum(jnp.max(s_c, axis=-1, keepdims=True), jnp.max(s_n, axis=-1, keepdims=True))
    p_c = jnp.exp2(s_c - m)
    p_n = jnp.exp2(s_n - m)
    l = jnp.sum(p_c, axis=-1, keepdims=True) + jnp.sum(p_n, axis=-1, keepdims=True)
    acc = (jnp.dot(p_c.astype(BF16), v_cache, preferred_element_type=F32)
           + jnp.dot(p_n.astype(BF16), v_new, preferred_element_type=F32))
    return acc / l


def _mla_decode_kernel(q_ref, ckv_ref, kr_ref, kn_ref, o_ref, *, n_new):
    qs = jnp.concatenate([q_ref[0, :, h * MLA_QCAT:(h + 1) * MLA_QCAT] for h in range(MLA_HEADS)], axis=0)
    ckv = ckv_ref[0].astype(BF16)
    kr = kr_ref[0].astype(BF16)
    k_new = _pad_rows(kn_ref[0], LANES)
    s_c = (lax.dot_general(qs[:, :KV_LORA], ckv, _NT, preferred_element_type=F32)
           + lax.dot_general(qs[:, KV_LORA:KV_LORA + MLA_ROPE_DIM], kr, _NT, preferred_element_type=F32))
    s_n = lax.dot_general(qs, k_new, _NT, preferred_element_type=F32)
    out = _decode_softmax(s_c, s_n, ckv, k_new[:, :KV_LORA], n_new)
    for h in range(MLA_HEADS):
        o_ref[0, :, h * KV_LORA:(h + 1) * KV_LORA] = out[h * n_new:(h + 1) * n_new].astype(o_ref.dtype)


def _mla_decode(qcat, ckv_c, kr_c, kcat_new):
    b, n, _ = qcat.shape
    past = ckv_c.shape[1]
    return pl.pallas_call(
        functools.partial(_mla_decode_kernel, n_new=n),
        grid=(b,),
        in_specs=[pl.BlockSpec((1, n, MLA_HEADS * MLA_QCAT), lambda i: (i, 0, 0)),
                  pl.BlockSpec((1, past, KV_LORA), lambda i: (i, 0, 0)),
                  pl.BlockSpec((1, past, MLA_ROPE_DIM), lambda i: (i, 0, 0)),
                  pl.BlockSpec((1, n, MLA_QCAT), lambda i: (i, 0, 0))],
        out_specs=pl.BlockSpec((1, n, MLA_HEADS * KV_LORA), lambda i: (i, 0, 0)),
        out_shape=jax.ShapeDtypeStruct((b, n, MLA_HEADS * KV_LORA), BF16),
        compiler_params=_params(("parallel",)),
        name="mla_decode",
    )(qcat, ckv_c, kr_c, kcat_new)


def _diff_decode_kernel(q_ref, kc_ref, vc_ref, kn_ref, vn_ref, lq_ref, lk_ref, o_ref, *, n_new, past, lam_init):
    lam = _diff_lambda(lq_ref, lk_ref, lam_init)
    for h in range(DIFF_HEADS):
        sl = slice(h * LANES, (h + 1) * LANES)
        q = q_ref[0, :, sl]
        k1 = kc_ref[0, pl.ds(2 * h, past, stride=2 * DIFF_HEADS), :].astype(BF16)
        k2 = kc_ref[0, pl.ds(2 * h + 1, past, stride=2 * DIFF_HEADS), :].astype(BF16)
        v = vc_ref[0, pl.ds(h, past, stride=DIFF_HEADS), :].astype(BF16)
        s_c = jnp.concatenate([lax.dot_general(q[:, :DIFF_HD], k1, _NT, preferred_element_type=F32),
                               lax.dot_general(q[:, DIFF_HD:], k2, _NT, preferred_element_type=F32)], axis=0)
        s_n = lax.dot_general(_lane_halves(q), _pad_rows(kn_ref[0, :, sl], LANES), _NT, preferred_element_type=F32)
        o = _decode_softmax(s_c, s_n, v, _pad_rows(vn_ref[0, :, sl], LANES), n_new)
        o_ref[0, :, sl] = o[:n_new] - lam * o[n_new:]


def _diff_decode(dq, kc, vc, dk_new, dv_new, lq, lk, lam_init):
    b, n, _ = dq.shape
    past = kc.shape[1]
    new = pl.BlockSpec((1, n, DIFF_W), lambda i: (i, 0, 0))
    return pl.pallas_call(
        functools.partial(_diff_decode_kernel, n_new=n, past=past, lam_init=lam_init),
        grid=(b,),
        in_specs=[new, pl.BlockSpec((1, past * DIFF_HEADS * 2, DIFF_HD), lambda i: (i, 0, 0)),
                  pl.BlockSpec((1, past * DIFF_HEADS, 2 * DIFF_HD), lambda i: (i, 0, 0)),
                  new, new, _const_spec(lq.shape), _const_spec(lk.shape)],
        out_specs=new,
        out_shape=jax.ShapeDtypeStruct((b, n, DIFF_W), F32),
        compiler_params=_params(("parallel",)),
        name="diff_decode",
    )(dq, kc.reshape(b, past * DIFF_HEADS * 2, DIFF_HD), vc.reshape(b, past * DIFF_HEADS, 2 * DIFF_HD),
      dk_new, dv_new, lq, lk)


def _even_out_kernel(olat_ref, odiff_ref, gate_ref, x_ref, wuv_ref, sg_ref, wout_ref, lng_ref, lnb_ref, y_ref,
                     *, lam_init):
    gate = gate_ref[...]
    o_mla = jnp.dot(olat_ref[...], wuv_ref[...], preferred_element_type=F32)
    parts = [(o_mla * gate[:, :MLA_W].astype(F32)).astype(BF16)]
    od = odiff_ref[...]
    for h in range(DIFF_HEADS):
        sl = slice(h * LANES, (h + 1) * LANES)
        n = _rms(od[:, sl], sg_ref[...]) * (1.0 - lam_init)
        parts.append((n * gate[:, MLA_W + h * LANES:MLA_W + (h + 1) * LANES].astype(F32)).astype(BF16))
    o = jnp.dot(jnp.concatenate(parts, axis=1), wout_ref[...], preferred_element_type=F32)
    y_ref[...] = _layernorm(ALPHA * x_ref[...] + o, lng_ref[...], lnb_ref[...])


def _even_out(olat, odiff, gate, x2d, wuv_bd, sg, wout, lng, lnb, lam_init, tm):
    t = x2d.shape[0]
    tm = min(tm, t)
    row = lambda w: pl.BlockSpec((tm, w), lambda i: (i, 0))
    return pl.pallas_call(
        functools.partial(_even_out_kernel, lam_init=lam_init),
        grid=(t // tm,),
        in_specs=[row(MLA_HEADS * KV_LORA), row(DIFF_W), row(MLA_W + DIFF_W), row(D_MODEL),
                  _const_spec(wuv_bd.shape), _const_spec(sg.shape), _const_spec(wout.shape),
                  _const_spec(lng.shape), _const_spec(lnb.shape)],
        out_specs=row(D_MODEL),
        out_shape=jax.ShapeDtypeStruct((t, D_MODEL), F32),
        compiler_params=_params(("parallel",)),
        name="even_out",
    )(olat, odiff, gate, x2d, wuv_bd, sg, wout, lng, lnb)


def _odd_in_kernel(x_ref, w_ref, q_ref, kf_ref, kb_ref, vf_ref, vb_ref, g_ref):
    xb = x_ref[...].astype(BF16)

    def proj(i):
        return jnp.dot(xb, w_ref[:, i * SB_W:(i + 1) * SB_W], preferred_element_type=F32)

    tm = x_ref.shape[0]

    def store_heads(ref, val):
        for h in range(SB_HEADS):
            ref[pl.ds(h, tm, stride=SB_HEADS), :] = val[:, h * SB_HD:(h + 1) * SB_HD]

    q_ref[...] = (proj(0) * (SB_HD ** -0.5 * LOG2E)).astype(BF16)
    k = proj(1)
    store_heads(kf_ref, k)
    kb_ref[...] = k.astype(BF16)
    v = proj(2)
    store_heads(vf_ref, v)
    vb_ref[...] = v.astype(BF16)
    g_ref[...] = _silu(proj(3)).astype(BF16)


def _odd_in(x2d, w_in, tm):
    t = x2d.shape[0]
    tm = min(tm, t)
    row = pl.BlockSpec((tm, SB_W), lambda i: (i, 0))
    heads = pl.BlockSpec((tm * SB_HEADS, SB_HD), lambda i: (i, 0))
    wide = lambda d: jax.ShapeDtypeStruct((t, SB_W), d)
    tall = jax.ShapeDtypeStruct((t * SB_HEADS, SB_HD), F32)
    return pl.pallas_call(
        _odd_in_kernel,
        grid=(t // tm,),
        in_specs=[pl.BlockSpec((tm, D_MODEL), lambda i: (i, 0)), _const_spec(w_in.shape)],
        out_specs=[row, heads, row, heads, row, row],
        out_shape=[wide(BF16), tall, wide(BF16), tall, wide(BF16), wide(BF16)],
        compiler_params=_params(("parallel",)),
        name="odd_in",
    )(x2d, w_in)


SB_EXIT_LOG2 = 160.0


def _log2(x):
    return jnp.log(x) * LOG2E


def _later_matrix(n):
    return (lax.broadcasted_iota(jnp.int32, (n, n), 0) > lax.broadcasted_iota(jnp.int32, (n, n), 1)).astype(BF16)


def _softplus2(z):
    return jnp.maximum(z, 0.0) + _log2(1.0 + jnp.exp2(-jnp.abs(z)))


def _sb_prompt_kernel(q_ref, k_ref, vt_ref, o_ref, c_ref, acc_ref, x_ref, sp0_ref, *, tq, pairs):
    i = pl.program_id(2)
    ut = (lax.broadcasted_iota(jnp.int32, (tq, tq), 1) > lax.broadcasted_iota(jnp.int32, (tq, tq), 0)).astype(BF16)
    keys = lax.broadcasted_iota(jnp.int32, (tq, 2 * tq), 0)
    queries = lax.broadcasted_iota(jnp.int32, (tq, 2 * tq), 1) & (tq - 1)
    diag_valid = keys < queries
    half = LANES // 2
    sls = [slice(p * LANES, (p + 1) * LANES) for p in range(pairs)]
    qss = [_lane_halves(q_ref[0, :, sl]) for sl in sls]

    def all_pairs(j, valid, first):
        k0 = pl.multiple_of(j * tq, tq)
        zs = [lax.dot_general(k_ref[0, pl.ds(k0, tq), sl], qss[p], _NT, preferred_element_type=F32)
              for p, sl in enumerate(sls)]
        laters = []
        for p, z in enumerate(zs):
            sp = _softplus2(z)
            x_ref[p] = z - sp
            if valid is not None:
                sp = jnp.where(valid, sp, 0.0)
            spb = sp.astype(BF16)
            sp0_ref[p] = spb[:1, :].astype(F32)
            laters.append(jnp.dot(ut, spb, preferred_element_type=F32))
        cmin = None
        for p, sl in enumerate(sls):
            a = jnp.exp2(x_ref[p] - laters[p])
            if valid is not None:
                a = jnp.where(valid, a, 0.0)
            pv = jnp.dot(vt_ref[0, j, sl, :], a.astype(BF16), preferred_element_type=F32)
            tot = laters[p][:1, :] + sp0_ref[p]
            if first:
                acc_ref[p] = pv
                c = tot
            else:
                c0 = c_ref[p]
                acc_ref[p] += pv * jnp.exp2(-c0)
                c = c0 + tot
            c_ref[p] = c
            cmin = c if cmin is None else jnp.minimum(cmin, c)
        return jnp.min(cmin)

    cmin = all_pairs(i, diag_valid, True)
    lax.while_loop(lambda jc: (jc[0] >= 0) & (jc[1] < SB_EXIT_LOG2),
                   lambda jc: (jc[0] - 1, all_pairs(jc[0], None, False)),
                   (i - 1, cmin))

    for p, sl in enumerate(sls):
        acc = acc_ref[p]
        o_t = jnp.concatenate([acc[:half, :tq], acc[half:, tq:]], axis=0)
        o_ref[0, :, sl] = o_t.T.astype(o_ref.dtype)


def _sb_prompt(q, k, v, tq, pairs):
    b, s, w = q.shape
    tq = min(tq, s)
    gw = pairs * LANES
    qspec = pl.BlockSpec((1, tq, gw), lambda bi, g, i: (bi, i, g))
    return pl.pallas_call(
        functools.partial(_sb_prompt_kernel, tq=tq, pairs=pairs),
        grid=(b, w // gw, s // tq),
        in_specs=[qspec, pl.BlockSpec((1, s, gw), lambda bi, g, i: (bi, 0, g)),
                  pl.BlockSpec((1, s // tq, gw, tq), lambda bi, g, i: (bi, 0, g, 0))],
        out_specs=qspec,
        out_shape=jax.ShapeDtypeStruct((b, s, w), BF16),
        scratch_shapes=[pltpu.VMEM((pairs, 1, 2 * tq), F32), pltpu.VMEM((pairs, LANES, 2 * tq), F32),
                        pltpu.VMEM((pairs, tq, 2 * tq), F32), pltpu.VMEM((pairs, 1, 2 * tq), F32)],
        compiler_params=_params(("parallel", "parallel", "parallel")),
        name="sb_prompt",
    )(q, k, _key_major(v, tq))


def _sb_rows_block(qss, kbs, vbs, u, cs, valid):
    zs = [lax.dot_general(qs, kb, _NT, preferred_element_type=F32) for qs, kb in zip(qss, kbs)]
    xs, sp0s, laters = [], [], []
    for z in zs:
        sp = _softplus2(z)
        xs.append(z - sp)
        if valid is not None:
            sp = jnp.where(valid, sp, 0.0)
        spb = sp.astype(BF16)
        sp0s.append(spb[:, :1].astype(F32))
        laters.append(jnp.dot(spb, u, preferred_element_type=F32))
    out = []
    for x, sp0, later, c, vb in zip(xs, sp0s, laters, cs, vbs):
        a = jnp.exp2(x - later - c)
        if valid is not None:
            a = jnp.where(valid, a, 0.0)
        out.append((jnp.dot(a.astype(BF16), vb, preferred_element_type=F32), c + later[:, :1] + sp0))
    return out


def _sb_decode_recent_kernel(q_ref, kw_ref, vw_ref, kn_ref, vn_ref, acc_ref, c_ref, *, n_new, tk, pairs):
    sls = [slice(p * LANES, (p + 1) * LANES) for p in range(pairs)]
    qss = [_lane_halves(q_ref[0, :, sl]) for sl in sls]
    rows = lax.broadcasted_iota(jnp.int32, (2 * n_new, LANES), 0) & (n_new - 1)
    cols = lax.broadcasted_iota(jnp.int32, (2 * n_new, LANES), 1)
    zero = jnp.zeros((2 * n_new, 1), F32)
    state = _sb_rows_block(qss, [_pad_rows(kn_ref[0, :, sl], LANES) for sl in sls],
                           [_pad_rows(vn_ref[0, :, sl], LANES) for sl in sls], _later_matrix(LANES),
                           [zero] * pairs, cols < rows)
    u = _later_matrix(tk)
    for j in reversed(range(kw_ref.shape[1] // tk)):
        blk = _sb_rows_block(qss, [kw_ref[0, j * tk:(j + 1) * tk, sl].astype(BF16) for sl in sls],
                             [vw_ref[0, j * tk:(j + 1) * tk, sl].astype(BF16) for sl in sls], u,
                             [c for _, c in state], None)
        state = [(acc + pv, c) for (acc, _), (pv, c) in zip(state, blk)]
    for p, (acc, c) in enumerate(state):
        acc_ref[0, p] = acc
        c_ref[0, p] = jnp.broadcast_to(c, (2 * n_new, LANES))


def _sb_decode_rest_kernel(q_ref, kc_ref, vc_ref, acc_in_ref, c_in_ref, acc_ref, *, tk):
    qs = _lane_halves(q_ref[0])
    acc = acc_in_ref[0, 0]
    c = c_in_ref[0, 0][:, :1]
    u = _later_matrix(tk)
    for j in reversed(range(kc_ref.shape[1] // tk)):
        def visit(acc, c, j=j):
            (pv, c), = _sb_rows_block([qs], [kc_ref[0, j * tk:(j + 1) * tk, :].astype(BF16)],
                                      [vc_ref[0, j * tk:(j + 1) * tk, :].astype(BF16)], u, [c], None)
            return acc + pv, c

        acc, c = lax.cond(jnp.min(c) < SB_EXIT_LOG2, visit, lambda acc, c: (acc, c), acc, c)
    acc_ref[0, 0] = acc


def _sb_decode_finish_kernel(acc_ref, o_ref, *, n_new, pairs):
    lane = lax.broadcasted_iota(jnp.int32, (n_new, LANES), 1)
    for p in range(pairs):
        acc = acc_ref[0, p]
        o_ref[0, :, p * LANES:(p + 1) * LANES] = jnp.where(lane < LANES // 2, acc[:n_new], acc[n_new:]).astype(o_ref.dtype)


def _sb_decode(q, kc4, vc4, k_new, v_new, tk, window):
    b, n, w = q.shape
    past = kc4.shape[1]
    window = min(window, past)
    tk = min(tk, window)
    pairs = w // LANES
    rest = past - window
    state_shape = jax.ShapeDtypeStruct((b, pairs, 2 * n, LANES), F32)
    new = pl.BlockSpec((1, n, w), lambda i: (i, 0, 0))
    win = pl.BlockSpec((1, window, w), lambda i: (i, 0, 0))
    state = pl.BlockSpec((1, pairs, 2 * n, LANES), lambda i: (i, 0, 0, 0))
    acc, c = pl.pallas_call(
        functools.partial(_sb_decode_recent_kernel, n_new=n, tk=tk, pairs=pairs),
        grid=(b,),
        in_specs=[new, win, win, new, new],
        out_specs=[state, state],
        out_shape=[state_shape, state_shape],
        compiler_params=_params(("parallel",)),
        name="sb_decode_recent",
    )(q, kc4[:, rest:].reshape(b, window, w), vc4[:, rest:].reshape(b, window, w), k_new, v_new)

    if rest:
        def older(acc, c):
            new1 = pl.BlockSpec((1, n, LANES), lambda i, g: (i, 0, g))
            cache = pl.BlockSpec((1, rest, LANES), lambda i, g: (i, 0, g))
            state1 = pl.BlockSpec((1, 1, 2 * n, LANES), lambda i, g: (i, g, 0, 0))
            return pl.pallas_call(
                functools.partial(_sb_decode_rest_kernel, tk=math.gcd(tk, rest)),
                grid=(b, pairs),
                in_specs=[new1, cache, cache, state1, state1],
                out_specs=state1,
                out_shape=state_shape,
                compiler_params=_params(("parallel", "parallel")),
                name="sb_decode_rest",
            )(q, kc4[:, :rest].reshape(b, rest, w), vc4[:, :rest].reshape(b, rest, w), acc, c)

        acc = lax.cond(jnp.min(c) < SB_EXIT_LOG2, older, lambda acc, c: acc, acc, c)

    return pl.pallas_call(
        functools.partial(_sb_decode_finish_kernel, n_new=n, pairs=pairs),
        grid=(b,),
        in_specs=[state],
        out_specs=new,
        out_shape=jax.ShapeDtypeStruct((b, n, w), BF16),
        compiler_params=_params(("parallel",)),
        name="sb_decode_finish",
    )(acc)


def _odd_out_kernel(o_ref, g_ref, x_ref, wout_ref, lng_ref, lnb_ref, y_ref):
    a = (o_ref[...].astype(F32) * g_ref[...].astype(F32)).astype(BF16)
    o = jnp.dot(a, wout_ref[...], preferred_element_type=F32)
    y_ref[...] = _layernorm(ALPHA * x_ref[...] + o, lng_ref[...], lnb_ref[...])


def _odd_out(o, gate, x2d, wout, lng, lnb, tm):
    t = x2d.shape[0]
    tm = min(tm, t)
    row = lambda w: pl.BlockSpec((tm, w), lambda i: (i, 0))
    return pl.pallas_call(
        _odd_out_kernel,
        grid=(t // tm,),
        in_specs=[row(SB_W), row(SB_W), row(D_MODEL), _const_spec(wout.shape),
                  _const_spec(lng.shape), _const_spec(lnb.shape)],
        out_specs=row(D_MODEL),
        out_shape=jax.ShapeDtypeStruct((t, D_MODEL), F32),
        compiler_params=_params(("parallel",)),
        name="odd_out",
    )(o, gate, x2d, wout, lng, lnb)


def _even_weights(w_in, q_g, w_uq, kv_g, w_uk):
    pad = jnp.zeros((D_MODEL, LANES - MLA_ROPE_DIM), w_in.dtype)
    o = np.cumsum([0, Q_LORA, KV_LORA, MLA_ROPE_DIM, MLA_W, DIFF_W, DIFF_W, DIFF_W, DIFF_W])
    w_re = jnp.concatenate([w_in[:, o[0]:o[3]], pad, w_in[:, o[3]:]], axis=1).astype(BF16)
    w_uq3 = w_uq.reshape(Q_LORA, MLA_HEADS, MLA_QK_DIM)
    wqn = w_uq3[:, :, :MLA_NOPE_DIM].reshape(Q_LORA, MLA_HEADS * MLA_NOPE_DIM).astype(BF16)
    wqr = jnp.pad(w_uq3[:, :, MLA_NOPE_DIM:], ((0, 0), (0, 0), (0, LANES - MLA_ROPE_DIM)))
    wqr = wqr.reshape(Q_LORA, MLA_HEADS * LANES).astype(BF16)
    eye = jnp.eye(MLA_HEADS, dtype=w_uk.dtype)
    wuk_bd = jnp.einsum('chn,hg->hngc', w_uk, eye).reshape(MLA_HEADS * MLA_NOPE_DIM, MLA_HEADS * KV_LORA).astype(BF16)
    return w_re, wqn, wqr, wuk_bd, q_g.reshape(1, Q_LORA), kv_g.reshape(1, KV_LORA)


def _uv_block_diag(w_uv):
    eye = jnp.eye(MLA_HEADS, dtype=w_uv.dtype)
    return jnp.einsum('chv,hg->hcgv', w_uv, eye).reshape(MLA_HEADS * KV_LORA, MLA_HEADS * MLA_V_DIM).astype(BF16)


TM_PROJ = 512
TQ_SOFTMAX = 128
TK_SOFTMAX = 512
TQ_SB = 256
SB_PAIRS = 8
TK_SB_DECODE = 256
SB_DECODE_WINDOW = 256


def kernel(x_prompt, x_sample, cache_mla_ckv, cache_mla_krope, cache_diff_k, cache_diff_v, cache_sb_k, cache_sb_v,
           w_in_even, q_norm_g, w_uq, kv_norm_g, w_uk, w_uv, diff_lambda_q, diff_lambda_k, diff_subln_g,
           w_out_even, ln_even_g, ln_even_b, w_in_odd, w_out_odd, ln_odd_g, ln_odd_b):
    bp, sp, _ = x_prompt.shape
    bs, ss, _ = x_sample.shape
    past = cache_sb_k.shape[2]
    if past % CHUNK or ss > CHUNK or (ss & (ss - 1)) or sp % TQ_SB:
        raise NotImplementedError("decode frames must share one chunk after a chunk-aligned cache")
    tp, ts = bp * sp, bs * ss
    pos_p = jnp.arange(sp, dtype=jnp.int32)
    pos_s = past + jnp.arange(ss, dtype=jnp.int32)
    pos_s_rows = jnp.tile(pos_s, bs)

    def tables(pos):
        mc, msa, msb, _ = _rope_tables(pos, MLA_ROPE_DIM, MLA_ROPE_THETA, LANES)
        dc, dsa, dsb, _ = _rope_tables(pos, DIFF_ROT, ROPE_THETA, DIFF_HD)
        return mc, msa, msb, dc, dsa, dsb

    yp = x_prompt.reshape(tp, D_MODEL)
    ys = x_sample.reshape(ts, D_MODEL)
    even_p, even_s, odd_p, odd_s = [], [], [], []
    for layer in range(DEPTH):
        i = layer // 2
        if layer % 2 == 0:
            lam_init = 0.8 - 0.6 * math.exp(-0.3 * layer)
            wts = _even_weights(w_in_even[i], q_norm_g[i], w_uq[i], kv_norm_g[i], w_uk[i])
            wuv_bd = _uv_block_diag(w_uv[i])
            sg = diff_subln_g[i].reshape(1, 2 * DIFF_HD)
            wout = w_out_even[i].astype(BF16)
            lng, lnb = ln_even_g[i].reshape(1, D_MODEL), ln_even_b[i].reshape(1, D_MODEL)
            lq, lk = diff_lambda_q[i], diff_lambda_k[i]
            qcat, kcat, ckv, krope, gate, dq, dkf, dkb, dvf, dvb = _even_in(yp, wts, tables(pos_p), sp, TM_PROJ)
            olat = _mla_prompt(qcat.reshape(bp, sp, -1), kcat.reshape(bp, sp, -1), TQ_SOFTMAX, TK_SOFTMAX)
            odiff = _diff_prompt(dq.reshape(bp, sp, -1), dkb.reshape(bp, sp, -1), dvb.reshape(bp, sp, -1),
                                 lq, lk, lam_init, 2 * TQ_SOFTMAX, TK_SOFTMAX)
            yp_new = _even_out(olat.reshape(tp, -1), odiff.reshape(tp, -1), gate, yp, wuv_bd, sg, wout, lng, lnb,
                               lam_init, TM_PROJ)
            even_p.append((ckv.reshape(bp, sp, KV_LORA), krope.reshape(bp, sp, MLA_ROPE_DIM),
                           dkf.reshape(bp, sp, DIFF_HEADS, 2, DIFF_HD), dvf.reshape(bp, sp, DIFF_HEADS, 2 * DIFF_HD)))
            qcat, kcat, ckv, krope, gate, dq, dkf, dkb, dvf, dvb = _even_in(ys, wts, tables(pos_s_rows), ts, TM_PROJ)
            olat = _mla_decode(qcat.reshape(bs, ss, -1), cache_mla_ckv[i], cache_mla_krope[i],
                               kcat.reshape(bs, ss, -1))
            odiff = _diff_decode(dq.reshape(bs, ss, -1), cache_diff_k[i], cache_diff_v[i], dkb.reshape(bs, ss, -1),
                                 dvb.reshape(bs, ss, -1), lq, lk, lam_init)
            ys_new = _even_out(olat.reshape(ts, -1), odiff.reshape(ts, -1), gate, ys, wuv_bd, sg, wout, lng, lnb,
                               lam_init, TM_PROJ)
            even_s.append((ckv.reshape(bs, ss, KV_LORA), krope.reshape(bs, ss, MLA_ROPE_DIM),
                           dkf.reshape(bs, ss, DIFF_HEADS, 2, DIFF_HD), dvf.reshape(bs, ss, DIFF_HEADS, 2 * DIFF_HD)))
            yp, ys = yp_new, ys_new
        else:
            w_in = w_in_odd[i].astype(BF16)
            wout = w_out_odd[i].astype(BF16)
            lng, lnb = ln_odd_g[i].reshape(1, D_MODEL), ln_odd_b[i].reshape(1, D_MODEL)
            q, kf, kb, vf, vb, gate = _odd_in(yp, w_in, TM_PROJ)
            o = _sb_prompt(q.reshape(bp, sp, -1), kb.reshape(bp, sp, -1), vb.reshape(bp, sp, -1), TQ_SB, SB_PAIRS)
            yp_new = _odd_out(o.reshape(tp, -1), gate, yp, wout, lng, lnb, TM_PROJ)
            odd_p.append((kf.reshape(bp, sp, SB_HEADS, SB_HD), vf.reshape(bp, sp, SB_HEADS, SB_HD)))
            q, kf, kb, vf, vb, gate = _odd_in(ys, w_in, TM_PROJ)
            o = _sb_decode(q.reshape(bs, ss, -1), cache_sb_k[i], cache_sb_v[i], kb.reshape(bs, ss, -1),
                           vb.reshape(bs, ss, -1), TK_SB_DECODE, SB_DECODE_WINDOW)
            ys_new = _odd_out(o.reshape(ts, -1), gate, ys, wout, lng, lnb, TM_PROJ)
            odd_s.append((kf.reshape(bs, ss, SB_HEADS, SB_HD), vf.reshape(bs, ss, SB_HEADS, SB_HD)))
            yp, ys = yp_new, ys_new
    stack = lambda lst, k: jnp.stack([t[k] for t in lst])
    return (yp.reshape(bp, sp, D_MODEL), ys.reshape(bs, ss, D_MODEL),
            stack(even_p, 0), stack(even_p, 1), stack(even_p, 2), stack(even_p, 3),
            stack(odd_p, 0), stack(odd_p, 1),
            stack(even_s, 0), stack(even_s, 1), stack(even_s, 2), stack(even_s, 3),
            stack(odd_s, 0), stack(odd_s, 1))
```

```python
import functools
import math

import numpy as np
import jax
import jax.numpy as jnp
from jax import lax
from jax.experimental import pallas as pl
from jax.experimental.pallas import tpu as pltpu

F32 = jnp.float32
BF16 = jnp.bfloat16

D_MODEL = 1024
DEPTH = 2
CHUNK = 64
ALPHA = (2 * DEPTH) ** 0.25
EPS = 1e-5
NEG_INF = -1e30
MLA_HEADS = 8
Q_LORA = 256
KV_LORA = 128
MLA_NOPE_DIM = 64
MLA_ROPE_DIM = 32
MLA_QK_DIM = MLA_NOPE_DIM + MLA_ROPE_DIM
MLA_V_DIM = 64
MLA_ROPE_THETA = 10000.0
MLA_W = MLA_HEADS * MLA_V_DIM
DIFF_HEADS = 4
DIFF_HD = 64
DIFF_ROT = DIFF_HD // 4
ROPE_THETA = 500000.0
DIFF_W = DIFF_HEADS * 2 * DIFF_HD
SB_HEADS = 16
SB_HD = 64
SB_W = SB_HEADS * SB_HD

LANES = 128
MLA_QCAT = 2 * LANES
VMEM_LIMIT = 56 * 1024 * 1024

_NT = (((1,), (1,)), ((), ()))
LOG2E = math.log2(math.e)


def _params(sem):
    return pltpu.CompilerParams(dimension_semantics=sem, vmem_limit_bytes=VMEM_LIMIT)


def _const_spec(shape):
    nd = len(shape)
    return pl.BlockSpec(shape, lambda *_: (0,) * nd, pipeline_mode=pl.Buffered(1))


def _rope_tables(pos, rot_dim, theta, period):
    half = rot_dim // 2
    inv = jnp.exp(-math.log(theta) * jnp.arange(half, dtype=F32) * (2.0 / rot_dim))
    ang = pos.astype(F32)[:, None] * inv[None, :]
    cos, sin = jnp.cos(ang), jnp.sin(ang)
    lane = np.arange(LANES) % period
    idx = lane % half
    in_rot = lane < rot_dim
    first = lane < half
    second = in_rot & ~first
    cos_l, sin_l = cos[:, idx], sin[:, idx]
    c = jnp.where(in_rot[None, :], cos_l, 1.0)
    sa = jnp.where(first[None, :], -sin_l, 0.0)
    sb = jnp.where(second[None, :], sin_l, 0.0)
    return c, sa, sb, half


def _apply_rope(x, c, sa, sb, half):
    return x * c + pltpu.roll(x, LANES - half, 1) * sa + pltpu.roll(x, half, 1) * sb


def _silu(x):
    return x / (1.0 + jnp.exp(-x))


def _rms(x, g):
    ms = jnp.mean(x * x, axis=-1, keepdims=True)
    return x * lax.rsqrt(ms + EPS) * g


def _layernorm(r, g, b):
    mu = jnp.mean(r, axis=-1, keepdims=True)
    rc = r - mu
    var = jnp.mean(rc * rc, axis=-1, keepdims=True)
    return rc * lax.rsqrt(var + EPS) * g + b


_C_QLAT = 0
_C_CKV = Q_LORA
_C_KROPE = _C_CKV + KV_LORA
_C_GMLA = _C_KROPE + LANES
_C_DQ = _C_GMLA + MLA_W
_C_DK = _C_DQ + DIFF_W
_C_DV = _C_DK + DIFF_W
_C_GDIFF = _C_DV + DIFF_W
_EVEN_COLS = _C_GDIFF + DIFF_W


def _even_in_kernel(x_ref, w_ref, wqn_ref, wqr_ref, wuk_ref, qg_ref, kvg_ref,
                    mc_ref, msa_ref, msb_ref, dc_ref, dsa_ref, dsb_ref,
                    qcat_ref, kcat_ref, ckv_ref, krope_ref, gate_ref, dq_ref,
                    dkf_ref, dkb_ref, dvf_ref, dvb_ref):
    xb = x_ref[...].astype(BF16)

    def proj(c0, c1):
        return jnp.dot(xb, w_ref[:, c0:c1], preferred_element_type=F32)

    mla_half = MLA_ROPE_DIM // 2
    diff_half = DIFF_ROT // 2
    mc, msa, msb = mc_ref[...], msa_ref[...], msb_ref[...]
    dc, dsa, dsb = dc_ref[...], dsa_ref[...], dsb_ref[...]

    qn = _rms(proj(_C_QLAT, _C_QLAT + Q_LORA), qg_ref[...]).astype(BF16)
    q_nope = jnp.dot(qn, wqn_ref[...], preferred_element_type=F32).astype(BF16)
    q_abs = jnp.dot(q_nope, wuk_ref[...], preferred_element_type=F32)
    q_rp = jnp.dot(qn, wqr_ref[...], preferred_element_type=F32)
    scale = MLA_QK_DIM ** -0.5 * LOG2E
    for h in range(MLA_HEADS):
        rot = _apply_rope(q_rp[:, h * LANES:(h + 1) * LANES], mc, msa, msb, mla_half)
        qcat_ref[:, h * MLA_QCAT:h * MLA_QCAT + LANES] = (q_abs[:, h * LANES:(h + 1) * LANES] * scale).astype(BF16)
        qcat_ref[:, h * MLA_QCAT + LANES:(h + 1) * MLA_QCAT] = (rot * scale).astype(BF16)

    kv = proj(_C_CKV, _C_GMLA)
    ckv = _rms(kv[:, :KV_LORA], kvg_ref[...])
    krope = _apply_rope(kv[:, KV_LORA:], mc, msa, msb, mla_half)
    ckv_ref[...] = ckv
    krope_ref[...] = krope[:, :MLA_ROPE_DIM]
    kcat_ref[:, :LANES] = ckv.astype(BF16)
    kcat_ref[:, LANES:] = krope.astype(BF16)

    gate_ref[:, :MLA_W] = _silu(proj(_C_GMLA, _C_DQ)).astype(BF16)
    gate_ref[:, MLA_W:] = _silu(proj(_C_GDIFF, _EVEN_COLS)).astype(BF16)

    dq = proj(_C_DQ, _C_DK)
    dk = proj(_C_DK, _C_DV)
    dv = proj(_C_DV, _C_GDIFF)
    dscale = DIFF_HD ** -0.5 * LOG2E
    tm = x_ref.shape[0]
    for h in range(DIFF_HEADS):
        sl = slice(h * LANES, (h + 1) * LANES)
        dq_ref[:, sl] = (_apply_rope(dq[:, sl], dc, dsa, dsb, diff_half) * dscale).astype(BF16)
        kr = _apply_rope(dk[:, sl], dc, dsa, dsb, diff_half)
        dkf_ref[pl.ds(2 * h, tm, stride=2 * DIFF_HEADS), :] = kr[:, :DIFF_HD]
        dkf_ref[pl.ds(2 * h + 1, tm, stride=2 * DIFF_HEADS), :] = kr[:, DIFF_HD:]
        dkb_ref[:, sl] = kr.astype(BF16)
        dvf_ref[pl.ds(h, tm, stride=DIFF_HEADS), :] = dv[:, sl]
    dvb_ref[...] = dv.astype(BF16)


def _even_in(x2d, wts, tabs, tab_rows_per_rep, tm):
    t = x2d.shape[0]
    tm = min(tm, t)
    nrep = tab_rows_per_rep // tm
    w_in, wqn, wqr, wuk, qg, kvg = wts
    row = lambda w: pl.BlockSpec((tm, w), lambda i: (i, 0))
    tab = pl.BlockSpec((tm, LANES), lambda i: (i % nrep, 0))
    outs = [
        (1, MLA_HEADS * MLA_QCAT, BF16),
        (1, MLA_QCAT, BF16),
        (1, KV_LORA, F32),
        (1, MLA_ROPE_DIM, F32),
        (1, MLA_W + DIFF_W, BF16),
        (1, DIFF_W, BF16),
        (2 * DIFF_HEADS, DIFF_HD, F32),
        (1, DIFF_W, BF16),
        (DIFF_HEADS, 2 * DIFF_HD, F32),
        (1, DIFF_W, BF16),
    ]
    return pl.pallas_call(
        _even_in_kernel,
        grid=(t // tm,),
        in_specs=[row(D_MODEL), _const_spec(w_in.shape), _const_spec(wqn.shape), _const_spec(wqr.shape),
                  _const_spec(wuk.shape), _const_spec(qg.shape), _const_spec(kvg.shape)] + [tab] * 6,
        out_specs=[pl.BlockSpec((tm * r, w), lambda i: (i, 0)) for r, w, _ in outs],
        out_shape=[jax.ShapeDtypeStruct((t * r, w), d) for r, w, d in outs],
        compiler_params=_params(("parallel",)),
        name="even_in",
    )(x2d, w_in, wqn, wqr, wuk, qg, kvg, *tabs)


def _softmax_step(s, vt, m_ref, l_ref, acc_ref, valid):
    if valid is not None:
        s = jnp.where(valid, s, NEG_INF)
    m_old = m_ref[...]
    m_new = jnp.maximum(m_old, jnp.max(s, axis=0, keepdims=True))
    alpha = jnp.exp2(m_old - m_new)
    p = jnp.exp2(s - m_new)
    l_ref[...] = alpha * l_ref[...] + jnp.sum(p, axis=0, keepdims=True)
    acc_ref[...] = alpha * acc_ref[...] + jnp.dot(vt, p.astype(BF16), preferred_element_type=F32)
    m_ref[...] = m_new


def _flash_scratch(n, tk, m_rows, dv):
    one = [pltpu.VMEM((tk, m_rows), F32), pltpu.VMEM((tk, m_rows), F32), pltpu.VMEM((1, m_rows), F32),
           pltpu.VMEM((1, m_rows), F32), pltpu.VMEM((dv, m_rows), F32)]
    return one * n


def _chunk_causal_flash(problems, q0, tq, tk):
    m_rows = problems[0][0].shape[0]
    for _, _, _, (_, _, m_ref, l_ref, acc_ref) in problems:
        m_ref[...] = jnp.full(m_ref.shape, NEG_INF, F32)
        l_ref[...] = jnp.zeros(l_ref.shape, F32)
        acc_ref[...] = jnp.zeros(acc_ref.shape, F32)
    n_full = q0 // tk

    def scores(prob, j):
        qs, get_k = prob[0], prob[1]
        return lax.dot_general(get_k(pl.multiple_of(j * tk, tk)), qs, _NT, preferred_element_type=F32)

    def consume(j, slot, valid):
        for _, _, get_vt, scr in problems:
            _softmax_step(scr[slot][...], get_vt(j), scr[2], scr[3], scr[4], valid)

    def step(j, slot):
        for prob in problems:
            prob[3][1 - slot][...] = scores(prob, j + 1)
        consume(j, slot, None)

    for prob in problems:
        prob[3][0][...] = scores(prob, 0)

    def body(jj, carry):
        step(2 * jj, 0)
        step(2 * jj + 1, 1)
        return carry

    lax.fori_loop(0, n_full // 2, body, 0)
    kpos = n_full * tk + lax.broadcasted_iota(jnp.int32, (tk, m_rows), 0)
    qpos = q0 + (lax.broadcasted_iota(jnp.int32, (tk, m_rows), 1) & (tq - 1))
    shift = CHUNK.bit_length() - 1
    valid = (kpos >> shift) <= (qpos >> shift)

    @pl.when(n_full % 2 == 1)
    def _():
        step(n_full - 1, 0)
        consume(n_full, 1, valid)

    @pl.when(n_full % 2 == 0)
    def _():
        consume(n_full, 0, valid)


MLA_GROUPS = 1


def _mla_prompt_kernel(q_ref, k_ref, vt_ref, o_ref, *scratch, tq, tk):
    q0 = pl.program_id(1) * tq
    get_k = lambda k0: k_ref[0, pl.ds(k0, tk), :]
    get_vt = lambda j: vt_ref[0, j]
    per = MLA_HEADS // MLA_GROUPS
    problems = []
    for g in range(MLA_GROUPS):
        qs = jnp.concatenate([q_ref[0, :, h * MLA_QCAT:(h + 1) * MLA_QCAT] for h in range(g * per, (g + 1) * per)],
                             axis=0)
        problems.append((qs, get_k, get_vt, scratch[5 * g:5 * g + 5]))
    _chunk_causal_flash(problems, q0, tq, tk)
    for g, (_, _, _, (_, _, _, l_ref, acc_ref)) in enumerate(problems):
        out = acc_ref[...] / l_ref[...]
        for hh in range(per):
            h = g * per + hh
            o_ref[0, :, h * KV_LORA:(h + 1) * KV_LORA] = out[:, hh * tq:(hh + 1) * tq].T.astype(o_ref.dtype)


def _key_major(x, tk):
    b, s, w = x.shape
    return x.reshape(b, s // tk, tk, w).transpose(0, 1, 3, 2)


def _mla_prompt(qcat, kcat, tq, tk):
    b, s, _ = qcat.shape
    tq, tk = min(tq, s), min(tk, s)
    m_rows = MLA_HEADS * tq
    vt = _key_major(kcat[:, :, :KV_LORA], tk)
    return pl.pallas_call(
        functools.partial(_mla_prompt_kernel, tq=tq, tk=tk),
        grid=(b, s // tq),
        in_specs=[pl.BlockSpec((1, tq, MLA_HEADS * MLA_QCAT), lambda bi, i: (bi, i, 0)),
                  pl.BlockSpec((1, s, MLA_QCAT), lambda bi, i: (bi, 0, 0)),
                  pl.BlockSpec((1, s // tk, KV_LORA, tk), lambda bi, i: (bi, 0, 0, 0))],
        out_specs=pl.BlockSpec((1, tq, MLA_HEADS * KV_LORA), lambda bi, i: (bi, i, 0)),
        out_shape=jax.ShapeDtypeStruct((b, s, MLA_HEADS * KV_LORA), BF16),
        scratch_shapes=_flash_scratch(MLA_GROUPS, tk, m_rows // MLA_GROUPS, KV_LORA),
        compiler_params=_params(("parallel", "parallel")),
        name="mla_prompt",
    )(qcat, kcat, vt)


def _lane_halves(x):
    lane = lax.broadcasted_iota(jnp.int32, x.shape, 1)
    zero = jnp.zeros_like(x)
    return jnp.concatenate([jnp.where(lane < LANES // 2, x, zero), jnp.where(lane < LANES // 2, zero, x)], axis=0)


def _diff_lambda(lq_ref, lk_ref, lam_init):
    e = jnp.exp(jnp.sum(lq_ref[...] * lk_ref[...], axis=-1, keepdims=True))
    return e[0:1] - e[1:2] + lam_init


def _diff_prompt_kernel(q_ref, k_ref, vt_ref, lq_ref, lk_ref, o_ref, *scratch, tq, tk, lam_init):
    q0 = pl.program_id(1) * tq
    lam = _diff_lambda(lq_ref, lk_ref, lam_init)
    problems = []
    for h in range(DIFF_HEADS):
        sl = slice(h * LANES, (h + 1) * LANES)
        problems.append((_lane_halves(q_ref[0, :, sl]),
                         lambda k0, sl=sl: k_ref[0, pl.ds(k0, tk), sl],
                         lambda j, sl=sl: vt_ref[0, j, sl, :],
                         scratch[5 * h:5 * h + 5]))
    _chunk_causal_flash(problems, q0, tq, tk)
    for h, (_, _, _, (_, _, _, l_ref, acc_ref)) in enumerate(problems):
        o = acc_ref[...] / l_ref[...]
        o_ref[0, :, h * LANES:(h + 1) * LANES] = (o[:, :tq] - lam * o[:, tq:]).T


def _diff_prompt(dq, dk, dv, lq, lk, lam_init, tq, tk):
    b, s, _ = dq.shape
    tq, tk = min(tq, s), min(tk, s)
    qspec = pl.BlockSpec((1, tq, DIFF_W), lambda bi, i: (bi, i, 0))
    return pl.pallas_call(
        functools.partial(_diff_prompt_kernel, tq=tq, tk=tk, lam_init=lam_init),
        grid=(b, s // tq),
        in_specs=[qspec, pl.BlockSpec((1, s, DIFF_W), lambda bi, i: (bi, 0, 0)),
                  pl.BlockSpec((1, s // tk, DIFF_W, tk), lambda bi, i: (bi, 0, 0, 0)),
                  _const_spec(lq.shape), _const_spec(lk.shape)],
        out_specs=qspec,
        out_shape=jax.ShapeDtypeStruct((b, s, DIFF_W), F32),
        scratch_shapes=_flash_scratch(DIFF_HEADS, tk, 2 * tq, LANES),
        compiler_params=_params(("parallel", "parallel")),
        name="diff_prompt",
    )(dq, dk, _key_major(dv, tk), lq, lk)


def _pad_rows(x, rows):
    return jnp.concatenate([x, jnp.zeros((rows - x.shape[0], x.shape[1]), x.dtype)], axis=0)


def _decode_softmax(s_c, s_n, v_cache, v_new, n_new):
    col = lax.broadcasted_iota(jnp.int32, s_n.shape, 1)
    s_n = jnp.where(col < n_new, s_n, NEG_INF)
    m = jnp.maximum(jnp.max(s_c, axis=-1, keepdims=True), jnp.max(s_n, axis=-1, keepdims=True))
    p_c = jnp.exp2(s_c - m)
    p_n = jnp.exp2(s_n - m)
    l = jnp.sum(p_c, axis=-1, keepdims=True) + jnp.sum(p_n, axis=-1, keepdims=True)
    acc = (jnp.dot(p_c.astype(BF16), v_cache, preferred_element_type=F32)
           + jnp.dot(p_n.astype(BF16), v_new, preferred_element_type=F32))
    return acc / l


def _mla_decode_kernel(q_ref, ckv_ref, kr_ref, kn_ref, o_ref, *, n_new):
    qs = jnp.concatenate([q_ref[0, :, h * MLA_QCAT:(h + 1) * MLA_QCAT] for h in range(MLA_HEADS)], axis=0)
    ckv = ckv_ref[0].astype(BF16)
    kr = kr_ref[0].astype(BF16)
    k_new = _pad_rows(kn_ref[0], LANES)
    s_c = (lax.dot_general(qs[:, :KV_LORA], ckv, _NT, preferred_element_type=F32)
           + lax.dot_general(qs[:, KV_LORA:KV_LORA + MLA_ROPE_DIM], kr, _NT, preferred_element_type=F32))
    s_n = lax.dot_general(qs, k_new, _NT, preferred_element_type=F32)
    out = _decode_softmax(s_c, s_n, ckv, k_new[:, :KV_LORA], n_new)
    for h in range(MLA_HEADS):
        o_ref[0, :, h * KV_LORA:(h + 1) * KV_LORA] = out[h * n_new:(h + 1) * n_new].astype(o_ref.dtype)


def _mla_decode(qcat, ckv_c, kr_c, kcat_new):
    b, n, _ = qcat.shape
    past = ckv_c.shape[1]
    return pl.pallas_call(
        functools.partial(_mla_decode_kernel, n_new=n),
        grid=(b,),
        in_specs=[pl.BlockSpec((1, n, MLA_HEADS * MLA_QCAT), lambda i: (i, 0, 0)),
                  pl.BlockSpec((1, past, KV_LORA), lambda i: (i, 0, 0)),
                  pl.BlockSpec((1, past, MLA_ROPE_DIM), lambda i: (i, 0, 0)),
                  pl.BlockSpec((1, n, MLA_QCAT), lambda i: (i, 0, 0))],
        out_specs=pl.BlockSpec((1, n, MLA_HEADS * KV_LORA), lambda i: (i, 0, 0)),
        out_shape=jax.ShapeDtypeStruct((b, n, MLA_HEADS * KV_LORA), BF16),
        compiler_params=_params(("parallel",)),
        name="mla_decode",
    )(qcat, ckv_c, kr_c, kcat_new)


def _diff_decode_kernel(q_ref, kc_ref, vc_ref, kn_ref, vn_ref, lq_ref, lk_ref, o_ref, *, n_new, past, lam_init):
    lam = _diff_lambda(lq_ref, lk_ref, lam_init)
    for h in range(DIFF_HEADS):
        sl = slice(h * LANES, (h + 1) * LANES)
        qs = _lane_halves(q_ref[0, :, sl])
        v = vc_ref[0, pl.ds(h, past, stride=DIFF_HEADS), :].astype(BF16)
        s_c = lax.dot_general(qs, kc_ref[0, :, sl].astype(BF16), _NT, preferred_element_type=F32)
        s_n = lax.dot_general(qs, _pad_rows(kn_ref[0, :, sl], LANES), _NT, preferred_element_type=F32)
        o = _decode_softmax(s_c, s_n, v, _pad_rows(vn_ref[0, :, sl], LANES), n_new)
        o_ref[0, :, sl] = o[:n_new] - lam * o[n_new:]


def _diff_decode(dq, kc, vc, dk_new, dv_new, lq, lk, lam_init):
    b, n, _ = dq.shape
    past = kc.shape[1]
    new = pl.BlockSpec((1, n, DIFF_W), lambda i: (i, 0, 0))
    return pl.pallas_call(
        functools.partial(_diff_decode_kernel, n_new=n, past=past, lam_init=lam_init),
        grid=(b,),
        in_specs=[new, pl.BlockSpec((1, past, DIFF_W), lambda i: (i, 0, 0)),
                  pl.BlockSpec((1, past * DIFF_HEADS, 2 * DIFF_HD), lambda i: (i, 0, 0)),
                  new, new, _const_spec(lq.shape), _const_spec(lk.shape)],
        out_specs=new,
        out_shape=jax.ShapeDtypeStruct((b, n, DIFF_W), F32),
        compiler_params=_params(("parallel",)),
        name="diff_decode",
    )(dq, kc, vc.reshape(b, past * DIFF_HEADS, 2 * DIFF_HD), dk_new, dv_new, lq, lk)


def _even_out_kernel(olat_ref, odiff_ref, gate_ref, x_ref, wuv_ref, sg_ref, wout_ref, lng_ref, lnb_ref, y_ref,
                     *, lam_init):
    gate = gate_ref[...]
    o_mla = jnp.dot(olat_ref[...], wuv_ref[...], preferred_element_type=F32)
    parts = [(o_mla * gate[:, :MLA_W].astype(F32)).astype(BF16)]
    od = odiff_ref[...]
    for h in range(DIFF_HEADS):
        sl = slice(h * LANES, (h + 1) * LANES)
        n = _rms(od[:, sl], sg_ref[...]) * (1.0 - lam_init)
        parts.append((n * gate[:, MLA_W + h * LANES:MLA_W + (h + 1) * LANES].astype(F32)).astype(BF16))
    o = jnp.dot(jnp.concatenate(parts, axis=1), wout_ref[...], preferred_element_type=F32)
    y_ref[...] = _layernorm(ALPHA * x_ref[...] + o, lng_ref[...], lnb_ref[...])


def _even_out(olat, odiff, gate, x2d, wuv_bd, sg, wout, lng, lnb, lam_init, tm):
    t = x2d.shape[0]
    tm = min(tm, t)
    row = lambda w: pl.BlockSpec((tm, w), lambda i: (i, 0))
    return pl.pallas_call(
        functools.partial(_even_out_kernel, lam_init=lam_init),
        grid=(t // tm,),
        in_specs=[row(MLA_HEADS * KV_LORA), row(DIFF_W), row(MLA_W + DIFF_W), row(D_MODEL),
                  _const_spec(wuv_bd.shape), _const_spec(sg.shape), _const_spec(wout.shape),
                  _const_spec(lng.shape), _const_spec(lnb.shape)],
        out_specs=row(D_MODEL),
        out_shape=jax.ShapeDtypeStruct((t, D_MODEL), F32),
        compiler_params=_params(("parallel",)),
        name="even_out",
    )(olat, odiff, gate, x2d, wuv_bd, sg, wout, lng, lnb)


def _odd_in_kernel(x_ref, w_ref, q_ref, kf_ref, kb_ref, vf_ref, vb_ref, g_ref):
    xb = x_ref[...].astype(BF16)

    def proj(i):
        return jnp.dot(xb, w_ref[:, i * SB_W:(i + 1) * SB_W], preferred_element_type=F32)

    tm = x_ref.shape[0]

    def store_heads(ref, val):
        for h in range(SB_HEADS):
            ref[pl.ds(h, tm, stride=SB_HEADS), :] = val[:, h * SB_HD:(h + 1) * SB_HD]

    q_ref[...] = (proj(0) * (SB_HD ** -0.5 * LOG2E)).astype(BF16)
    k = proj(1)
    store_heads(kf_ref, k)
    kb_ref[...] = k.astype(BF16)
    v = proj(2)
    store_heads(vf_ref, v)
    vb_ref[...] = v.astype(BF16)
    g_ref[...] = _silu(proj(3)).astype(BF16)


def _odd_in(x2d, w_in, tm):
    t = x2d.shape[0]
    tm = min(tm, t)
    row = pl.BlockSpec((tm, SB_W), lambda i: (i, 0))
    heads = pl.BlockSpec((tm * SB_HEADS, SB_HD), lambda i: (i, 0))
    wide = lambda d: jax.ShapeDtypeStruct((t, SB_W), d)
    tall = jax.ShapeDtypeStruct((t * SB_HEADS, SB_HD), F32)
    return pl.pallas_call(
        _odd_in_kernel,
        grid=(t // tm,),
        in_specs=[pl.BlockSpec((tm, D_MODEL), lambda i: (i, 0)), _const_spec(w_in.shape)],
        out_specs=[row, heads, row, heads, row, row],
        out_shape=[wide(BF16), tall, wide(BF16), tall, wide(BF16), wide(BF16)],
        compiler_params=_params(("parallel",)),
        name="odd_in",
    )(x2d, w_in)


SB_EXIT_LOG2 = 160.0


def _log2(x):
    return jnp.log(x) * LOG2E


def _later_matrix(n):
    return (lax.broadcasted_iota(jnp.int32, (n, n), 0) > lax.broadcasted_iota(jnp.int32, (n, n), 1)).astype(BF16)


def _softplus2(z):
    return jnp.maximum(z, 0.0) + _log2(1.0 + jnp.exp2(-jnp.abs(z)))


def _sb_prompt_kernel(q_ref, k_ref, vt_ref, o_ref, c_ref, acc_ref, x_ref, sp0_ref, *, tq, pairs):
    i = pl.program_id(2)
    ut = (lax.broadcasted_iota(jnp.int32, (tq, tq), 1) > lax.broadcasted_iota(jnp.int32, (tq, tq), 0)).astype(BF16)
    keys = lax.broadcasted_iota(jnp.int32, (tq, 2 * tq), 0)
    queries = lax.broadcasted_iota(jnp.int32, (tq, 2 * tq), 1) & (tq - 1)
    diag_valid = keys < queries
    half = LANES // 2
    sls = [slice(p * LANES, (p + 1) * LANES) for p in range(pairs)]
    qss = [_lane_halves(q_ref[0, :, sl]) for sl in sls]

    def all_pairs(j, valid, first):
        k0 = pl.multiple_of(j * tq, tq)
        zs = [lax.dot_general(k_ref[0, pl.ds(k0, tq), sl], qss[p], _NT, preferred_element_type=F32)
              for p, sl in enumerate(sls)]
        laters = []
        for p, z in enumerate(zs):
            sp = _softplus2(z)
            x_ref[p] = z - sp
            if valid is not None:
                sp = jnp.where(valid, sp, 0.0)
            spb = sp.astype(BF16)
            sp0_ref[p] = spb[:1, :].astype(F32)
            laters.append(jnp.dot(ut, spb, preferred_element_type=F32))
        cmin = None
        for p, sl in enumerate(sls):
            a = jnp.exp2(x_ref[p] - laters[p])
            if valid is not None:
                a = jnp.where(valid, a, 0.0)
            pv = jnp.dot(vt_ref[0, j, sl, :], a.astype(BF16), preferred_element_type=F32)
            tot = laters[p][:1, :] + sp0_ref[p]
            if first:
                acc_ref[p] = pv
                c = tot
            else:
                c0 = c_ref[p]
                acc_ref[p] += pv * jnp.exp2(-c0)
                c = c0 + tot
            c_ref[p] = c
            cmin = c if cmin is None else jnp.minimum(cmin, c)
        return jnp.min(cmin)

    cmin = all_pairs(i, diag_valid, True)
    lax.while_loop(lambda jc: (jc[0] >= 0) & (jc[1] < SB_EXIT_LOG2),
                   lambda jc: (jc[0] - 1, all_pairs(jc[0], None, False)),
                   (i - 1, cmin))

    for p, sl in enumerate(sls):
        acc = acc_ref[p]
        o_t = jnp.concatenate([acc[:half, :tq], acc[half:, tq:]], axis=0)
        o_ref[0, :, sl] = o_t.T.astype(o_ref.dtype)


def _sb_prompt(q, k, v, tq, pairs):
    b, s, w = q.shape
    tq = min(tq, s)
    gw = pairs * LANES
    qspec = pl.BlockSpec((1, tq, gw), lambda bi, g, i: (bi, i, g))
    return pl.pallas_call(
        functools.partial(_sb_prompt_kernel, tq=tq, pairs=pairs),
        grid=(b, w // gw, s // tq),
        in_specs=[qspec, pl.BlockSpec((1, s, gw), lambda bi, g, i: (bi, 0, g)),
                  pl.BlockSpec((1, s // tq, gw, tq), lambda bi, g, i: (bi, 0, g, 0))],
        out_specs=qspec,
        out_shape=jax.ShapeDtypeStruct((b, s, w), BF16),
        scratch_shapes=[pltpu.VMEM((pairs, 1, 2 * tq), F32), pltpu.VMEM((pairs, LANES, 2 * tq), F32),
                        pltpu.VMEM((pairs, tq, 2 * tq), F32), pltpu.VMEM((pairs, 1, 2 * tq), F32)],
        compiler_params=_params(("parallel", "parallel", "parallel")),
        name="sb_prompt",
    )(q, k, _key_major(v, tq))


def _sb_rows_block(qss, kbs, vbs, u, cs, valid):
    zs = [lax.dot_general(qs, kb, _NT, preferred_element_type=F32) for qs, kb in zip(qss, kbs)]
    xs, sp0s, laters = [], [], []
    for z in zs:
        sp = _softplus2(z)
        xs.append(z - sp)
        if valid is not None:
            sp = jnp.where(valid, sp, 0.0)
        spb = sp.astype(BF16)
        sp0s.append(spb[:, :1].astype(F32))
        laters.append(jnp.dot(spb, u, preferred_element_type=F32))
    out = []
    for x, sp0, later, c, vb in zip(xs, sp0s, laters, cs, vbs):
        a = jnp.exp2(x - later - c)
        if valid is not None:
            a = jnp.where(valid, a, 0.0)
        out.append((jnp.dot(a.astype(BF16), vb, preferred_element_type=F32), c + later[:, :1] + sp0))
    return out


def _sb_decode_recent_kernel(q_ref, kw_ref, vw_ref, kn_ref, vn_ref, acc_ref, c_ref, *, n_new, tk, pairs):
    sls = [slice(p * LANES, (p + 1) * LANES) for p in range(pairs)]
    qss = [_lane_halves(q_ref[0, :, sl]) for sl in sls]
    rows = lax.broadcasted_iota(jnp.int32, (2 * n_new, LANES), 0) & (n_new - 1)
    cols = lax.broadcasted_iota(jnp.int32, (2 * n_new, LANES), 1)
    zero = jnp.zeros((2 * n_new, 1), F32)
    state = _sb_rows_block(qss, [_pad_rows(kn_ref[0, :, sl], LANES) for sl in sls],
                           [_pad_rows(vn_ref[0, :, sl], LANES) for sl in sls], _later_matrix(LANES),
                           [zero] * pairs, cols < rows)
    u = _later_matrix(tk)
    for j in reversed(range(kw_ref.shape[1] // tk)):
        blk = _sb_rows_block(qss, [kw_ref[0, j * tk:(j + 1) * tk, sl].astype(BF16) for sl in sls],
                             [vw_ref[0, j * tk:(j + 1) * tk, sl].astype(BF16) for sl in sls], u,
                             [c for _, c in state], None)
        state = [(acc + pv, c) for (acc, _), (pv, c) in zip(state, blk)]
    for p, (acc, c) in enumerate(state):
        acc_ref[0, p] = acc
        c_ref[0, p] = jnp.broadcast_to(c, (2 * n_new, LANES))


def _sb_decode_rest_kernel(q_ref, kc_ref, vc_ref, acc_in_ref, c_in_ref, acc_ref, *, tk):
    qs = _lane_halves(q_ref[0])
    acc = acc_in_ref[0, 0]
    c = c_in_ref[0, 0][:, :1]
    u = _later_matrix(tk)
    for j in reversed(range(kc_ref.shape[1] // tk)):
        def visit(acc, c, j=j):
            (pv, c), = _sb_rows_block([qs], [kc_ref[0, j * tk:(j + 1) * tk, :].astype(BF16)],
                                      [vc_ref[0, j * tk:(j + 1) * tk, :].astype(BF16)], u, [c], None)
            return acc + pv, c

        acc, c = lax.cond(jnp.min(c) < SB_EXIT_LOG2, visit, lambda acc, c: (acc, c), acc, c)
    acc_ref[0, 0] = acc


def _sb_decode_finish_kernel(acc_ref, o_ref, *, n_new, pairs):
    lane = lax.broadcasted_iota(jnp.int32, (n_new, LANES), 1)
    for p in range(pairs):
        acc = acc_ref[0, p]
        o_ref[0, :, p * LANES:(p + 1) * LANES] = jnp.where(lane < LANES // 2, acc[:n_new], acc[n_new:]).astype(o_ref.dtype)


def _sb_decode(q, kc4, vc4, k_new, v_new, tk, window):
    b, n, w = q.shape
    past = kc4.shape[1]
    window = min(window, past)
    tk = min(tk, window)
    pairs = w // LANES
    rest = past - window
    state_shape = jax.ShapeDtypeStruct((b, pairs, 2 * n, LANES), F32)
    new = pl.BlockSpec((1, n, w), lambda i: (i, 0, 0))
    win = pl.BlockSpec((1, window, w), lambda i: (i, 0, 0))
    state = pl.BlockSpec((1, pairs, 2 * n, LANES), lambda i: (i, 0, 0, 0))
    acc, c = pl.pallas_call(
        functools.partial(_sb_decode_recent_kernel, n_new=n, tk=tk, pairs=pairs),
        grid=(b,),
        in_specs=[new, win, win, new, new],
        out_specs=[state, state],
        out_shape=[state_shape, state_shape],
        compiler_params=_params(("parallel",)),
        name="sb_decode_recent",
    )(q, kc4[:, rest:].reshape(b, window, w), vc4[:, rest:].reshape(b, window, w), k_new, v_new)

    if rest:
        def older(acc, c):
            new1 = pl.BlockSpec((1, n, LANES), lambda i, g: (i, 0, g))
            cache = pl.BlockSpec((1, rest, LANES), lambda i, g: (i, 0, g))
            state1 = pl.BlockSpec((1, 1, 2 * n, LANES), lambda i, g: (i, g, 0, 0))
            return pl.pallas_call(
                functools.partial(_sb_decode_rest_kernel, tk=math.gcd(tk, rest)),
                grid=(b, pairs),
                in_specs=[new1, cache, cache, state1, state1],
                out_specs=state1,
                out_shape=state_shape,
                compiler_params=_params(("parallel", "parallel")),
                name="sb_decode_rest",
            )(q, kc4[:, :rest].reshape(b, rest, w), vc4[:, :rest].reshape(b, rest, w), acc, c)

        acc = lax.cond(jnp.min(c) < SB_EXIT_LOG2, older, lambda acc, c: acc, acc, c)

    return pl.pallas_call(
        functools.partial(_sb_decode_finish_kernel, n_new=n, pairs=pairs),
        grid=(b,),
        in_specs=[state],
        out_specs=new,
        out_shape=jax.ShapeDtypeStruct((b, n, w), BF16),
        compiler_params=_params(("parallel",)),
        name="sb_decode_finish",
    )(acc)


def _odd_out_kernel(o_ref, g_ref, x_ref, wout_ref, lng_ref, lnb_ref, y_ref):
    a = (o_ref[...].astype(F32) * g_ref[...].astype(F32)).astype(BF16)
    o = jnp.dot(a, wout_ref[...], preferred_element_type=F32)
    y_ref[...] = _layernorm(ALPHA * x_ref[...] + o, lng_ref[...], lnb_ref[...])


def _odd_out(o, gate, x2d, wout, lng, lnb, tm):
    t = x2d.shape[0]
    tm = min(tm, t)
    row = lambda w: pl.BlockSpec((tm, w), lambda i: (i, 0))
    return pl.pallas_call(
        _odd_out_kernel,
        grid=(t // tm,),
        in_specs=[row(SB_W), row(SB_W), row(D_MODEL), _const_spec(wout.shape),
                  _const_spec(lng.shape), _const_spec(lnb.shape)],
        out_specs=row(D_MODEL),
        out_shape=jax.ShapeDtypeStruct((t, D_MODEL), F32),
        compiler_params=_params(("parallel",)),
        name="odd_out",
    )(o, gate, x2d, wout, lng, lnb)


def _even_weights(w_in, q_g, w_uq, kv_g, w_uk):
    pad = jnp.zeros((D_MODEL, LANES - MLA_ROPE_DIM), w_in.dtype)
    o = np.cumsum([0, Q_LORA, KV_LORA, MLA_ROPE_DIM, MLA_W, DIFF_W, DIFF_W, DIFF_W, DIFF_W])
    w_re = jnp.concatenate([w_in[:, o[0]:o[3]], pad, w_in[:, o[3]:]], axis=1).astype(BF16)
    w_uq3 = w_uq.reshape(Q_LORA, MLA_HEADS, MLA_QK_DIM)
    wqn = w_uq3[:, :, :MLA_NOPE_DIM].reshape(Q_LORA, MLA_HEADS * MLA_NOPE_DIM).astype(BF16)
    wqr = jnp.pad(w_uq3[:, :, MLA_NOPE_DIM:], ((0, 0), (0, 0), (0, LANES - MLA_ROPE_DIM)))
    wqr = wqr.reshape(Q_LORA, MLA_HEADS * LANES).astype(BF16)
    eye = jnp.eye(MLA_HEADS, dtype=w_uk.dtype)
    wuk_bd = jnp.einsum('chn,hg->hngc', w_uk, eye).reshape(MLA_HEADS * MLA_NOPE_DIM, MLA_HEADS * KV_LORA).astype(BF16)
    return w_re, wqn, wqr, wuk_bd, q_g.reshape(1, Q_LORA), kv_g.reshape(1, KV_LORA)


def _uv_block_diag(w_uv):
    eye = jnp.eye(MLA_HEADS, dtype=w_uv.dtype)
    return jnp.einsum('chv,hg->hcgv', w_uv, eye).reshape(MLA_HEADS * KV_LORA, MLA_HEADS * MLA_V_DIM).astype(BF16)


TM_PROJ = 512
TQ_MLA = 256
TQ_DIFF = 256
TK_SOFTMAX = 512
TQ_SB = 256
SB_PAIRS = 8
TK_SB_DECODE = 256
SB_DECODE_WINDOW = 256


def kernel(x_prompt, x_sample, cache_mla_ckv, cache_mla_krope, cache_diff_k, cache_diff_v, cache_sb_k, cache_sb_v,
           w_in_even, q_norm_g, w_uq, kv_norm_g, w_uk, w_uv, diff_lambda_q, diff_lambda_k, diff_subln_g,
           w_out_even, ln_even_g, ln_even_b, w_in_odd, w_out_odd, ln_odd_g, ln_odd_b):
    bp, sp, _ = x_prompt.shape
    bs, ss, _ = x_sample.shape
    past = cache_sb_k.shape[2]
    if past % CHUNK or ss > CHUNK or (ss & (ss - 1)) or sp % TQ_SB:
        raise NotImplementedError("decode frames must share one chunk after a chunk-aligned cache")
    tp, ts = bp * sp, bs * ss
    pos_p = jnp.arange(sp, dtype=jnp.int32)
    pos_s = past + jnp.arange(ss, dtype=jnp.int32)
    pos_s_rows = jnp.tile(pos_s, bs)

    def tables(pos):
        mc, msa, msb, _ = _rope_tables(pos, MLA_ROPE_DIM, MLA_ROPE_THETA, LANES)
        dc, dsa, dsb, _ = _rope_tables(pos, DIFF_ROT, ROPE_THETA, DIFF_HD)
        return mc, msa, msb, dc, dsa, dsb

    yp = x_prompt.reshape(tp, D_MODEL)
    ys = x_sample.reshape(ts, D_MODEL)
    even_p, even_s, odd_p, odd_s = [], [], [], []
    for layer in range(DEPTH):
        i = layer // 2
        if layer % 2 == 0:
            lam_init = 0.8 - 0.6 * math.exp(-0.3 * layer)
            wts = _even_weights(w_in_even[i], q_norm_g[i], w_uq[i], kv_norm_g[i], w_uk[i])
            wuv_bd = _uv_block_diag(w_uv[i])
            sg = diff_subln_g[i].reshape(1, 2 * DIFF_HD)
            wout = w_out_even[i].astype(BF16)
            lng, lnb = ln_even_g[i].reshape(1, D_MODEL), ln_even_b[i].reshape(1, D_MODEL)
            lq, lk = diff_lambda_q[i], diff_lambda_k[i]
            qcat, kcat, ckv, krope, gate, dq, dkf, dkb, dvf, dvb = _even_in(yp, wts, tables(pos_p), sp, TM_PROJ)
            olat = _mla_prompt(qcat.reshape(bp, sp, -1), kcat.reshape(bp, sp, -1), TQ_MLA, TK_SOFTMAX)
            odiff = _diff_prompt(dq.reshape(bp, sp, -1), dkb.reshape(bp, sp, -1), dvb.reshape(bp, sp, -1),
                                 lq, lk, lam_init, TQ_DIFF, TK_SOFTMAX)
            yp_new = _even_out(olat.reshape(tp, -1), odiff.reshape(tp, -1), gate, yp, wuv_bd, sg, wout, lng, lnb,
                               lam_init, TM_PROJ)
            even_p.append((ckv.reshape(bp, sp, KV_LORA), krope.reshape(bp, sp, MLA_ROPE_DIM),
                           dkf.reshape(bp, sp, DIFF_HEADS, 2, DIFF_HD), dvf.reshape(bp, sp, DIFF_HEADS, 2 * DIFF_HD)))
            qcat, kcat, ckv, krope, gate, dq, dkf, dkb, dvf, dvb = _even_in(ys, wts, tables(pos_s_rows), ts, TM_PROJ)
            olat = _mla_decode(qcat.reshape(bs, ss, -1), cache_mla_ckv[i], cache_mla_krope[i],
                               kcat.reshape(bs, ss, -1))
            odiff = _diff_decode(dq.reshape(bs, ss, -1), cache_diff_k[i].reshape(bs, past, DIFF_W), cache_diff_v[i],
                                 dkb.reshape(bs, ss, -1),
                                 dvb.reshape(bs, ss, -1), lq, lk, lam_init)
            ys_new = _even_out(olat.reshape(ts, -1), odiff.reshape(ts, -1), gate, ys, wuv_bd, sg, wout, lng, lnb,
                               lam_init, TM_PROJ)
            even_s.append((ckv.reshape(bs, ss, KV_LORA), krope.reshape(bs, ss, MLA_ROPE_DIM),
                           dkf.reshape(bs, ss, DIFF_HEADS, 2, DIFF_HD), dvf.reshape(bs, ss, DIFF_HEADS, 2 * DIFF_HD)))
            yp, ys = yp_new, ys_new
        else:
            w_in = w_in_odd[i].astype(BF16)
            wout = w_out_odd[i].astype(BF16)
            lng, lnb = ln_odd_g[i].reshape(1, D_MODEL), ln_odd_b[i].reshape(1, D_MODEL)
            q, kf, kb, vf, vb, gate = _odd_in(yp, w_in, TM_PROJ)
            o = _sb_prompt(q.reshape(bp, sp, -1), kb.reshape(bp, sp, -1), vb.reshape(bp, sp, -1), TQ_SB, SB_PAIRS)
            yp_new = _odd_out(o.reshape(tp, -1), gate, yp, wout, lng, lnb, TM_PROJ)
            odd_p.append((kf.reshape(bp, sp, SB_HEADS, SB_HD), vf.reshape(bp, sp, SB_HEADS, SB_HD)))
            q, kf, kb, vf, vb, gate = _odd_in(ys, w_in, TM_PROJ)
            o = _sb_decode(q.reshape(bs, ss, -1), cache_sb_k[i], cache_sb_v[i], kb.reshape(bs, ss, -1),
                           vb.reshape(bs, ss, -1), TK_SB_DECODE, SB_DECODE_WINDOW)
            ys_new = _odd_out(o.reshape(ts, -1), gate, ys, wout, lng, lnb, TM_PROJ)
            odd_s.append((kf.reshape(bs, ss, SB_HEADS, SB_HD), vf.reshape(bs, ss, SB_HEADS, SB_HD)))
            yp, ys = yp_new, ys_new
    stack = lambda lst, k: jnp.stack([t[k] for t in lst])
    return (yp.reshape(bp, sp, D_MODEL), ys.reshape(bs, ss, D_MODEL),
            stack(even_p, 0), stack(even_p, 1), stack(even_p, 2), stack(even_p, 3),
            stack(odd_p, 0), stack(odd_p, 1),
            stack(even_s, 0), stack(even_s, 1), stack(even_s, 2), stack(even_s, 3),
            stack(odd_s, 0), stack(odd_s, 1))
```

```python
import functools
import math

import numpy as np
import jax
import jax.numpy as jnp
from jax import lax
from jax.experimental import pallas as pl
from jax.experimental.pallas import tpu as pltpu

F32 = jnp.float32
BF16 = jnp.bfloat16

D_MODEL = 1024
DEPTH = 2
CHUNK = 64
ALPHA = (2 * DEPTH) ** 0.25
EPS = 1e-5
NEG_INF = -1e30
MLA_HEADS = 8
Q_LORA = 256
KV_LORA = 128
MLA_NOPE_DIM = 64
MLA_ROPE_DIM = 32
MLA_QK_DIM = MLA_NOPE_DIM + MLA_ROPE_DIM
MLA_V_DIM = 64
MLA_ROPE_THETA = 10000.0
MLA_W = MLA_HEADS * MLA_V_DIM
DIFF_HEADS = 4
DIFF_HD = 64
DIFF_ROT = DIFF_HD // 4
ROPE_THETA = 500000.0
DIFF_W = DIFF_HEADS * 2 * DIFF_HD
SB_HEADS = 16
SB_HD = 64
SB_W = SB_HEADS * SB_HD

LANES = 128
MLA_QCAT = 2 * LANES
VMEM_LIMIT = 56 * 1024 * 1024

_NT = (((1,), (1,)), ((), ()))
LOG2E = math.log2(math.e)


def _params(sem):
    return pltpu.CompilerParams(dimension_semantics=sem, vmem_limit_bytes=VMEM_LIMIT)


def _const_spec(shape):
    nd = len(shape)
    return pl.BlockSpec(shape, lambda *_: (0,) * nd, pipeline_mode=pl.Buffered(1))


def _rope_tables(pos, rot_dim, theta, period):
    half = rot_dim // 2
    inv = jnp.exp(-math.log(theta) * jnp.arange(half, dtype=F32) * (2.0 / rot_dim))
    ang = pos.astype(F32)[:, None] * inv[None, :]
    cos, sin = jnp.cos(ang), jnp.sin(ang)
    lane = np.arange(LANES) % period
    idx = lane % half
    in_rot = lane < rot_dim
    first = lane < half
    second = in_rot & ~first
    cos_l, sin_l = cos[:, idx], sin[:, idx]
    c = jnp.where(in_rot[None, :], cos_l, 1.0)
    sa = jnp.where(first[None, :], -sin_l, 0.0)
    sb = jnp.where(second[None, :], sin_l, 0.0)
    return c, sa, sb, half


def _apply_rope(x, c, sa, sb, half):
    return x * c + pltpu.roll(x, LANES - half, 1) * sa + pltpu.roll(x, half, 1) * sb


def _silu(x):
    return x / (1.0 + jnp.exp(-x))


def _rms(x, g):
    ms = jnp.mean(x * x, axis=-1, keepdims=True)
    return x * lax.rsqrt(ms + EPS) * g


def _layernorm(r, g, b):
    mu = jnp.mean(r, axis=-1, keepdims=True)
    rc = r - mu
    var = jnp.mean(rc * rc, axis=-1, keepdims=True)
    return rc * lax.rsqrt(var + EPS) * g + b


_C_QLAT = 0
_C_CKV = Q_LORA
_C_KROPE = _C_CKV + KV_LORA
_C_GMLA = _C_KROPE + LANES
_C_DQ = _C_GMLA + MLA_W
_C_DK = _C_DQ + DIFF_W
_C_DV = _C_DK + DIFF_W
_C_GDIFF = _C_DV + DIFF_W
_EVEN_COLS = _C_GDIFF + DIFF_W


def _even_in_kernel(x_ref, w_ref, wqn_ref, wqr_ref, wuk_ref, qg_ref, kvg_ref,
                    mc_ref, msa_ref, msb_ref, dc_ref, dsa_ref, dsb_ref,
                    qcat_ref, kcat_ref, ckv_ref, krope_ref, gate_ref, dq_ref,
                    dkf_ref, dkb_ref, dvf_ref, dvb_ref):
    xb = x_ref[...].astype(BF16)

    def proj(c0, c1):
        return jnp.dot(xb, w_ref[:, c0:c1], preferred_element_type=F32)

    mla_half = MLA_ROPE_DIM // 2
    diff_half = DIFF_ROT // 2
    mc, msa, msb = mc_ref[...], msa_ref[...], msb_ref[...]
    dc, dsa, dsb = dc_ref[...], dsa_ref[...], dsb_ref[...]

    qn = _rms(proj(_C_QLAT, _C_QLAT + Q_LORA), qg_ref[...]).astype(BF16)
    q_nope = jnp.dot(qn, wqn_ref[...], preferred_element_type=F32).astype(BF16)
    q_abs = jnp.dot(q_nope, wuk_ref[...], preferred_element_type=F32)
    q_rp = jnp.dot(qn, wqr_ref[...], preferred_element_type=F32)
    scale = MLA_QK_DIM ** -0.5 * LOG2E
    for h in range(MLA_HEADS):
        rot = _apply_rope(q_rp[:, h * LANES:(h + 1) * LANES], mc, msa, msb, mla_half)
        qcat_ref[:, h * MLA_QCAT:h * MLA_QCAT + LANES] = (q_abs[:, h * LANES:(h + 1) * LANES] * scale).astype(BF16)
        qcat_ref[:, h * MLA_QCAT + LANES:(h + 1) * MLA_QCAT] = (rot * scale).astype(BF16)

    kv = proj(_C_CKV, _C_GMLA)
    ckv = _rms(kv[:, :KV_LORA], kvg_ref[...])
    krope = _apply_rope(kv[:, KV_LORA:], mc, msa, msb, mla_half)
    ckv_ref[...] = ckv
    krope_ref[...] = krope[:, :MLA_ROPE_DIM]
    kcat_ref[:, :LANES] = ckv.astype(BF16)
    kcat_ref[:, LANES:] = krope.astype(BF16)

    gate_ref[:, :MLA_W] = _silu(proj(_C_GMLA, _C_DQ)).astype(BF16)
    gate_ref[:, MLA_W:] = _silu(proj(_C_GDIFF, _EVEN_COLS)).astype(BF16)

    dq = proj(_C_DQ, _C_DK)
    dk = proj(_C_DK, _C_DV)
    dv = proj(_C_DV, _C_GDIFF)
    dscale = DIFF_HD ** -0.5 * LOG2E
    tm = x_ref.shape[0]
    for h in range(DIFF_HEADS):
        sl = slice(h * LANES, (h + 1) * LANES)
        dq_ref[:, sl] = (_apply_rope(dq[:, sl], dc, dsa, dsb, diff_half) * dscale).astype(BF16)
        kr = _apply_rope(dk[:, sl], dc, dsa, dsb, diff_half)
        dkf_ref[pl.ds(2 * h, tm, stride=2 * DIFF_HEADS), :] = kr[:, :DIFF_HD]
        dkf_ref[pl.ds(2 * h + 1, tm, stride=2 * DIFF_HEADS), :] = kr[:, DIFF_HD:]
        dkb_ref[:, sl] = kr.astype(BF16)
        dvf_ref[pl.ds(h, tm, stride=DIFF_HEADS), :] = dv[:, sl]
    dvb_ref[...] = dv.astype(BF16)


def _even_in(x2d, wts, tabs, tab_rows_per_rep, tm):
    t = x2d.shape[0]
    tm = min(tm, t)
    nrep = tab_rows_per_rep // tm
    w_in, wqn, wqr, wuk, qg, kvg = wts
    row = lambda w: pl.BlockSpec((tm, w), lambda i: (i, 0))
    tab = pl.BlockSpec((tm, LANES), lambda i: (i % nrep, 0))
    outs = [
        (1, MLA_HEADS * MLA_QCAT, BF16),
        (1, MLA_QCAT, BF16),
        (1, KV_LORA, F32),
        (1, MLA_ROPE_DIM, F32),
        (1, MLA_W + DIFF_W, BF16),
        (1, DIFF_W, BF16),
        (2 * DIFF_HEADS, DIFF_HD, F32),
        (1, DIFF_W, BF16),
        (DIFF_HEADS, 2 * DIFF_HD, F32),
        (1, DIFF_W, BF16),
    ]
    return pl.pallas_call(
        _even_in_kernel,
        grid=(t // tm,),
        in_specs=[row(D_MODEL), _const_spec(w_in.shape), _const_spec(wqn.shape), _const_spec(wqr.shape),
                  _const_spec(wuk.shape), _const_spec(qg.shape), _const_spec(kvg.shape)] + [tab] * 6,
        out_specs=[pl.BlockSpec((tm * r, w), lambda i: (i, 0)) for r, w, _ in outs],
        out_shape=[jax.ShapeDtypeStruct((t * r, w), d) for r, w, d in outs],
        compiler_params=_params(("parallel",)),
        name="even_in",
    )(x2d, w_in, wqn, wqr, wuk, qg, kvg, *tabs)


def _softmax_step(s, vt, m_ref, l_ref, acc_ref, valid):
    if valid is not None:
        s = jnp.where(valid, s, NEG_INF)
    m_old = m_ref[...]
    m_new = jnp.maximum(m_old, jnp.max(s, axis=0, keepdims=True))
    alpha = jnp.exp2(m_old - m_new)
    p = jnp.exp2(s - m_new)
    l_ref[...] = alpha * l_ref[...] + jnp.sum(p, axis=0, keepdims=True)
    acc_ref[...] = alpha * acc_ref[...] + jnp.dot(vt, p.astype(BF16), preferred_element_type=F32)
    m_ref[...] = m_new


def _flash_scratch(n, tk, m_rows, dv):
    one = [pltpu.VMEM((tk, m_rows), F32), pltpu.VMEM((tk, m_rows), F32), pltpu.VMEM((1, m_rows), F32),
           pltpu.VMEM((1, m_rows), F32), pltpu.VMEM((dv, m_rows), F32)]
    return one * n


def _chunk_causal_flash(problems, q0, tq, tk):
    m_rows = problems[0][0].shape[0]
    for _, _, _, (_, _, m_ref, l_ref, acc_ref) in problems:
        m_ref[...] = jnp.full(m_ref.shape, NEG_INF, F32)
        l_ref[...] = jnp.zeros(l_ref.shape, F32)
        acc_ref[...] = jnp.zeros(acc_ref.shape, F32)
    n_full = q0 // tk

    def scores(prob, j):
        qs, get_k = prob[0], prob[1]
        return lax.dot_general(get_k(pl.multiple_of(j * tk, tk)), qs, _NT, preferred_element_type=F32)

    def consume(j, slot, valid):
        for _, _, get_vt, scr in problems:
            _softmax_step(scr[slot][...], get_vt(j), scr[2], scr[3], scr[4], valid)

    def step(j, slot):
        for prob in problems:
            prob[3][1 - slot][...] = scores(prob, j + 1)
        consume(j, slot, None)

    for prob in problems:
        prob[3][0][...] = scores(prob, 0)

    def body(jj, carry):
        step(2 * jj, 0)
        step(2 * jj + 1, 1)
        return carry

    lax.fori_loop(0, n_full // 2, body, 0)
    kpos = n_full * tk + lax.broadcasted_iota(jnp.int32, (tk, m_rows), 0)
    qpos = q0 + (lax.broadcasted_iota(jnp.int32, (tk, m_rows), 1) & (tq - 1))
    shift = CHUNK.bit_length() - 1
    valid = (kpos >> shift) <= (qpos >> shift)

    @pl.when(n_full % 2 == 1)
    def _():
        step(n_full - 1, 0)
        consume(n_full, 1, valid)

    @pl.when(n_full % 2 == 0)
    def _():
        consume(n_full, 0, valid)


MLA_GROUPS = 1


def _mla_prompt_kernel(q_ref, k_ref, vt_ref, o_ref, *scratch, tq, tk):
    q0 = pl.program_id(1) * tq
    get_k = lambda k0: k_ref[0, pl.ds(k0, tk), :]
    get_vt = lambda j: vt_ref[0, j]
    per = MLA_HEADS // MLA_GROUPS
    problems = []
    for g in range(MLA_GROUPS):
        qs = jnp.concatenate([q_ref[0, :, h * MLA_QCAT:(h + 1) * MLA_QCAT] for h in range(g * per, (g + 1) * per)],
                             axis=0)
        problems.append((qs, get_k, get_vt, scratch[5 * g:5 * g + 5]))
    _chunk_causal_flash(problems, q0, tq, tk)
    for g, (_, _, _, (_, _, _, l_ref, acc_ref)) in enumerate(problems):
        out = acc_ref[...] / l_ref[...]
        for hh in range(per):
            h = g * per + hh
            o_ref[0, :, h * KV_LORA:(h + 1) * KV_LORA] = out[:, hh * tq:(hh + 1) * tq].T.astype(o_ref.dtype)


def _key_major(x, tk):
    b, s, w = x.shape
    return x.reshape(b, s // tk, tk, w).transpose(0, 1, 3, 2)


def _mla_prompt(qcat, kcat, tq, tk):
    b, s, _ = qcat.shape
    tq, tk = min(tq, s), min(tk, s)
    m_rows = MLA_HEADS * tq
    vt = _key_major(kcat[:, :, :KV_LORA], tk)
    return pl.pallas_call(
        functools.partial(_mla_prompt_kernel, tq=tq, tk=tk),
        grid=(b, s // tq),
        in_specs=[pl.BlockSpec((1, tq, MLA_HEADS * MLA_QCAT), lambda bi, i: (bi, i, 0)),
                  pl.BlockSpec((1, s, MLA_QCAT), lambda bi, i: (bi, 0, 0)),
                  pl.BlockSpec((1, s // tk, KV_LORA, tk), lambda bi, i: (bi, 0, 0, 0))],
        out_specs=pl.BlockSpec((1, tq, MLA_HEADS * KV_LORA), lambda bi, i: (bi, i, 0)),
        out_shape=jax.ShapeDtypeStruct((b, s, MLA_HEADS * KV_LORA), BF16),
        scratch_shapes=_flash_scratch(MLA_GROUPS, tk, m_rows // MLA_GROUPS, KV_LORA),
        compiler_params=_params(("parallel", "parallel")),
        name="mla_prompt",
    )(qcat, kcat, vt)


def _lane_halves(x):
    lane = lax.broadcasted_iota(jnp.int32, x.shape, 1)
    zero = jnp.zeros_like(x)
    return jnp.concatenate([jnp.where(lane < LANES // 2, x, zero), jnp.where(lane < LANES // 2, zero, x)], axis=0)


def _diff_lambda(lq_ref, lk_ref, lam_init):
    e = jnp.exp(jnp.sum(lq_ref[...] * lk_ref[...], axis=-1, keepdims=True))
    return e[0:1] - e[1:2] + lam_init


def _diff_prompt_kernel(q_ref, k_ref, vt_ref, lq_ref, lk_ref, o_ref, *scratch, tq, tk, lam_init):
    q0 = pl.program_id(1) * tq
    lam = _diff_lambda(lq_ref, lk_ref, lam_init)
    problems = []
    for h in range(DIFF_HEADS):
        sl = slice(h * LANES, (h + 1) * LANES)
        problems.append((_lane_halves(q_ref[0, :, sl]),
                         lambda k0, sl=sl: k_ref[0, pl.ds(k0, tk), sl],
                         lambda j, sl=sl: vt_ref[0, j, sl, :],
                         scratch[5 * h:5 * h + 5]))
    _chunk_causal_flash(problems, q0, tq, tk)
    for h, (_, _, _, (_, _, _, l_ref, acc_ref)) in enumerate(problems):
        o = acc_ref[...] / l_ref[...]
        o_ref[0, :, h * LANES:(h + 1) * LANES] = (o[:, :tq] - lam * o[:, tq:]).T


def _diff_prompt(dq, dk, dv, lq, lk, lam_init, tq, tk):
    b, s, _ = dq.shape
    tq, tk = min(tq, s), min(tk, s)
    qspec = pl.BlockSpec((1, tq, DIFF_W), lambda bi, i: (bi, i, 0))
    return pl.pallas_call(
        functools.partial(_diff_prompt_kernel, tq=tq, tk=tk, lam_init=lam_init),
        grid=(b, s // tq),
        in_specs=[qspec, pl.BlockSpec((1, s, DIFF_W), lambda bi, i: (bi, 0, 0)),
                  pl.BlockSpec((1, s // tk, DIFF_W, tk), lambda bi, i: (bi, 0, 0, 0)),
                  _const_spec(lq.shape), _const_spec(lk.shape)],
        out_specs=qspec,
        out_shape=jax.ShapeDtypeStruct((b, s, DIFF_W), F32),
        scratch_shapes=_flash_scratch(DIFF_HEADS, tk, 2 * tq, LANES),
        compiler_params=_params(("parallel", "parallel")),
        name="diff_prompt",
    )(dq, dk, _key_major(dv, tk), lq, lk)


def _pad_rows(x, rows):
    return jnp.concatenate([x, jnp.zeros((rows - x.shape[0], x.shape[1]), x.dtype)], axis=0)


def _decode_softmax(s_c, s_n, v_cache, v_new, n_new):
    col = lax.broadcasted_iota(jnp.int32, s_n.shape, 1)
    s_n = jnp.where(col < n_new, s_n, NEG_INF)
    m = jnp.maximum(jnp.max(s_c, axis=-1, keepdims=True), jnp.max(s_n, axis=-1, keepdims=True))
    p_c = jnp.exp2(s_c - m)
    p_n = jnp.exp2(s_n - m)
    l = jnp.sum(p_c, axis=-1, keepdims=True) + jnp.sum(p_n, axis=-1, keepdims=True)
    acc = (jnp.dot(p_c.astype(BF16), v_cache, preferred_element_type=F32)
           + jnp.dot(p_n.astype(BF16), v_new, preferred_element_type=F32))
    return acc / l


def _mla_decode_kernel(q_ref, ckv_ref, kr_ref, kn_ref, o_ref, *, n_new):
    qs = jnp.concatenate([q_ref[0, :, h * MLA_QCAT:(h + 1) * MLA_QCAT] for h in range(MLA_HEADS)], axis=0)
    ckv = ckv_ref[0].astype(BF16)
    kr = kr_ref[0].astype(BF16)
    k_new = _pad_rows(kn_ref[0], LANES)
    s_c = (lax.dot_general(qs[:, :KV_LORA], ckv, _NT, preferred_element_type=F32)
           + lax.dot_general(qs[:, KV_LORA:KV_LORA + MLA_ROPE_DIM], kr, _NT, preferred_element_type=F32))
    s_n = lax.dot_general(qs, k_new, _NT, preferred_element_type=F32)
    out = _decode_softmax(s_c, s_n, ckv, k_new[:, :KV_LORA], n_new)
    for h in range(MLA_HEADS):
        o_ref[0, :, h * KV_LORA:(h + 1) * KV_LORA] = out[h * n_new:(h + 1) * n_new].astype(o_ref.dtype)


def _mla_decode(qcat, ckv_c, kr_c, kcat_new):
    b, n, _ = qcat.shape
    past = ckv_c.shape[1]
    return pl.pallas_call(
        functools.partial(_mla_decode_kernel, n_new=n),
        grid=(b,),
        in_specs=[pl.BlockSpec((1, n, MLA_HEADS * MLA_QCAT), lambda i: (i, 0, 0)),
                  pl.BlockSpec((1, past, KV_LORA), lambda i: (i, 0, 0)),
                  pl.BlockSpec((1, past, MLA_ROPE_DIM), lambda i: (i, 0, 0)),
                  pl.BlockSpec((1, n, MLA_QCAT), lambda i: (i, 0, 0))],
        out_specs=pl.BlockSpec((1, n, MLA_HEADS * KV_LORA), lambda i: (i, 0, 0)),
        out_shape=jax.ShapeDtypeStruct((b, n, MLA_HEADS * KV_LORA), BF16),
        compiler_params=_params(("parallel",)),
        name="mla_decode",
    )(qcat, ckv_c, kr_c, kcat_new)


def _diff_decode_kernel(q_ref, kc_ref, vc_ref, kn_ref, vn_ref, lq_ref, lk_ref, o_ref, *, n_new, past, lam_init):
    lam = _diff_lambda(lq_ref, lk_ref, lam_init)
    for h in range(DIFF_HEADS):
        sl = slice(h * LANES, (h + 1) * LANES)
        qs = _lane_halves(q_ref[0, :, sl])
        v = vc_ref[0, pl.ds(h, past, stride=DIFF_HEADS), :].astype(BF16)
        s_c = lax.dot_general(qs, kc_ref[0, :, sl].astype(BF16), _NT, preferred_element_type=F32)
        s_n = lax.dot_general(qs, _pad_rows(kn_ref[0, :, sl], LANES), _NT, preferred_element_type=F32)
        o = _decode_softmax(s_c, s_n, v, _pad_rows(vn_ref[0, :, sl], LANES), n_new)
        o_ref[0, :, sl] = o[:n_new] - lam * o[n_new:]


def _diff_decode(dq, kc, vc, dk_new, dv_new, lq, lk, lam_init):
    b, n, _ = dq.shape
    past = kc.shape[1]
    new = pl.BlockSpec((1, n, DIFF_W), lambda i: (i, 0, 0))
    return pl.pallas_call(
        functools.partial(_diff_decode_kernel, n_new=n, past=past, lam_init=lam_init),
        grid=(b,),
        in_specs=[new, pl.BlockSpec((1, past, DIFF_W), lambda i: (i, 0, 0)),
                  pl.BlockSpec((1, past * DIFF_HEADS, 2 * DIFF_HD), lambda i: (i, 0, 0)),
                  new, new, _const_spec(lq.shape), _const_spec(lk.shape)],
        out_specs=new,
        out_shape=jax.ShapeDtypeStruct((b, n, DIFF_W), F32),
        compiler_params=_params(("parallel",)),
        name="diff_decode",
    )(dq, kc, vc.reshape(b, past * DIFF_HEADS, 2 * DIFF_HD), dk_new, dv_new, lq, lk)


def _even_out_kernel(olat_ref, odiff_ref, gate_ref, x_ref, wuv_ref, sg_ref, wout_ref, lng_ref, lnb_ref, y_ref,
                     *, lam_init):
    gate = gate_ref[...]
    o_mla = jnp.dot(olat_ref[...], wuv_ref[...], preferred_element_type=F32)
    parts = [(o_mla * gate[:, :MLA_W].astype(F32)).astype(BF16)]
    od = odiff_ref[...]
    for h in range(DIFF_HEADS):
        sl = slice(h * LANES, (h + 1) * LANES)
        n = _rms(od[:, sl], sg_ref[...]) * (1.0 - lam_init)
        parts.append((n * gate[:, MLA_W + h * LANES:MLA_W + (h + 1) * LANES].astype(F32)).astype(BF16))
    o = jnp.dot(jnp.concatenate(parts, axis=1), wout_ref[...], preferred_element_type=F32)
    y_ref[...] = _layernorm(ALPHA * x_ref[...] + o, lng_ref[...], lnb_ref[...])


def _even_out(olat, odiff, gate, x2d, wuv_bd, sg, wout, lng, lnb, lam_init, tm):
    t = x2d.shape[0]
    tm = min(tm, t)
    row = lambda w: pl.BlockSpec((tm, w), lambda i: (i, 0))
    return pl.pallas_call(
        functools.partial(_even_out_kernel, lam_init=lam_init),
        grid=(t // tm,),
        in_specs=[row(MLA_HEADS * KV_LORA), row(DIFF_W), row(MLA_W + DIFF_W), row(D_MODEL),
                  _const_spec(wuv_bd.shape), _const_spec(sg.shape), _const_spec(wout.shape),
                  _const_spec(lng.shape), _const_spec(lnb.shape)],
        out_specs=row(D_MODEL),
        out_shape=jax.ShapeDtypeStruct((t, D_MODEL), F32),
        compiler_params=_params(("parallel",)),
        name="even_out",
    )(olat, odiff, gate, x2d, wuv_bd, sg, wout, lng, lnb)


def _odd_in_kernel(x_ref, w_ref, q_ref, kf_ref, kb_ref, vf_ref, vb_ref, g_ref):
    xb = x_ref[...].astype(BF16)

    def proj(i):
        return jnp.dot(xb, w_ref[:, i * SB_W:(i + 1) * SB_W], preferred_element_type=F32)

    tm = x_ref.shape[0]

    def store_heads(ref, val):
        for h in range(SB_HEADS):
            ref[pl.ds(h, tm, stride=SB_HEADS), :] = val[:, h * SB_HD:(h + 1) * SB_HD]

    q_ref[...] = (proj(0) * (SB_HD ** -0.5 * LOG2E)).astype(BF16)
    k = proj(1)
    store_heads(kf_ref, k)
    kb_ref[...] = k.astype(BF16)
    v = proj(2)
    store_heads(vf_ref, v)
    vb_ref[...] = v.astype(BF16)
    g_ref[...] = _silu(proj(3)).astype(BF16)


def _odd_in(x2d, w_in, tm):
    t = x2d.shape[0]
    tm = min(tm, t)
    row = pl.BlockSpec((tm, SB_W), lambda i: (i, 0))
    heads = pl.BlockSpec((tm * SB_HEADS, SB_HD), lambda i: (i, 0))
    wide = lambda d: jax.ShapeDtypeStruct((t, SB_W), d)
    tall = jax.ShapeDtypeStruct((t * SB_HEADS, SB_HD), F32)
    return pl.pallas_call(
        _odd_in_kernel,
        grid=(t // tm,),
        in_specs=[pl.BlockSpec((tm, D_MODEL), lambda i: (i, 0)), _const_spec(w_in.shape)],
        out_specs=[row, heads, row, heads, row, row],
        out_shape=[wide(BF16), tall, wide(BF16), tall, wide(BF16), wide(BF16)],
        compiler_params=_params(("parallel",)),
        name="odd_in",
    )(x2d, w_in)


SB_EXIT_LOG2 = 160.0


def _log2(x):
    return jnp.log(x) * LOG2E


def _later_matrix(n):
    return (lax.broadcasted_iota(jnp.int32, (n, n), 0) > lax.broadcasted_iota(jnp.int32, (n, n), 1)).astype(BF16)


def _softplus2(z):
    return jnp.maximum(z, 0.0) + _log2(1.0 + jnp.exp2(-jnp.abs(z)))


def _sb_prompt_kernel(q_ref, k_ref, vt_ref, o_ref, c_ref, acc_ref, x_ref, sp0_ref, *, tq, pairs):
    i = pl.program_id(2)
    ut = (lax.broadcasted_iota(jnp.int32, (tq, tq), 1) > lax.broadcasted_iota(jnp.int32, (tq, tq), 0)).astype(BF16)
    keys = lax.broadcasted_iota(jnp.int32, (tq, 2 * tq), 0)
    queries = lax.broadcasted_iota(jnp.int32, (tq, 2 * tq), 1) & (tq - 1)
    diag_valid = keys < queries
    half = LANES // 2
    sls = [slice(p * LANES, (p + 1) * LANES) for p in range(pairs)]
    qss = [_lane_halves(q_ref[0, :, sl]) for sl in sls]

    def all_pairs(j, valid, first):
        k0 = pl.multiple_of(j * tq, tq)
        zs = [lax.dot_general(k_ref[0, pl.ds(k0, tq), sl], qss[p], _NT, preferred_element_type=F32)
              for p, sl in enumerate(sls)]
        laters = []
        for p, z in enumerate(zs):
            sp = _softplus2(z)
            x_ref[p] = z - sp
            if valid is not None:
                sp = jnp.where(valid, sp, 0.0)
            spb = sp.astype(BF16)
            sp0_ref[p] = spb[:1, :].astype(F32)
            laters.append(jnp.dot(ut, spb, preferred_element_type=F32))
        cmin = None
        for p, sl in enumerate(sls):
            a = jnp.exp2(x_ref[p] - laters[p])
            if valid is not None:
                a = jnp.where(valid, a, 0.0)
            pv = jnp.dot(vt_ref[0, j, sl, :], a.astype(BF16), preferred_element_type=F32)
            tot = laters[p][:1, :] + sp0_ref[p]
            if first:
                acc_ref[p] = pv
                c = tot
            else:
                c0 = c_ref[p]
                acc_ref[p] += pv * jnp.exp2(-c0)
                c = c0 + tot
            c_ref[p] = c
            cmin = c if cmin is None else jnp.minimum(cmin, c)
        return jnp.min(cmin)

    cmin = all_pairs(i, diag_valid, True)
    lax.while_loop(lambda jc: (jc[0] >= 0) & (jc[1] < SB_EXIT_LOG2),
                   lambda jc: (jc[0] - 1, all_pairs(jc[0], None, False)),
                   (i - 1, cmin))

    for p, sl in enumerate(sls):
        acc = acc_ref[p]
        o_t = jnp.concatenate([acc[:half, :tq], acc[half:, tq:]], axis=0)
        o_ref[0, :, sl] = o_t.T.astype(o_ref.dtype)


def _sb_prompt(q, k, v, tq, pairs):
    b, s, w = q.shape
    tq = min(tq, s)
    gw = pairs * LANES
    qspec = pl.BlockSpec((1, tq, gw), lambda bi, g, i: (bi, i, g))
    return pl.pallas_call(
        functools.partial(_sb_prompt_kernel, tq=tq, pairs=pairs),
        grid=(b, w // gw, s // tq),
        in_specs=[qspec, pl.BlockSpec((1, s, gw), lambda bi, g, i: (bi, 0, g)),
                  pl.BlockSpec((1, s // tq, gw, tq), lambda bi, g, i: (bi, 0, g, 0))],
        out_specs=qspec,
        out_shape=jax.ShapeDtypeStruct((b, s, w), BF16),
        scratch_shapes=[pltpu.VMEM((pairs, 1, 2 * tq), F32), pltpu.VMEM((pairs, LANES, 2 * tq), F32),
                        pltpu.VMEM((pairs, tq, 2 * tq), F32), pltpu.VMEM((pairs, 1, 2 * tq), F32)],
        compiler_params=_params(("parallel", "parallel", "parallel")),
        name="sb_prompt",
    )(q, k, _key_major(v, tq))


def _sb_rows_block(score_fns, value_fns, u, cs, valid):
    zs = [f() for f in score_fns]
    xs, sp0s, laters = [], [], []
    for z in zs:
        sp = _softplus2(z)
        xs.append(z - sp)
        if valid is not None:
            sp = jnp.where(valid, sp, 0.0)
        spb = sp.astype(BF16)
        sp0s.append(spb[:, :1].astype(F32))
        laters.append(jnp.dot(spb, u, preferred_element_type=F32))
    out = []
    for x, sp0, later, c, value in zip(xs, sp0s, laters, cs, value_fns):
        a = jnp.exp2(x - later - c)
        if valid is not None:
            a = jnp.where(valid, a, 0.0)
        out.append((value(a.astype(BF16)), c + later[:, :1] + sp0))
    return out


def _pair_fns(qs, kb, vb):
    return (lambda: lax.dot_general(qs, kb, _NT, preferred_element_type=F32),
            lambda a: jnp.dot(a, vb, preferred_element_type=F32))


def _sb_decode_recent_kernel(q_ref, kw_ref, vw_ref, kn_ref, vn_ref, acc_ref, c_ref, *, n_new, tk, window, pairs):
    sls = [slice(p * LANES, (p + 1) * LANES) for p in range(pairs)]
    qss = [_lane_halves(q_ref[0, :, sl]) for sl in sls]
    rows = lax.broadcasted_iota(jnp.int32, (2 * n_new, LANES), 0) & (n_new - 1)
    cols = lax.broadcasted_iota(jnp.int32, (2 * n_new, LANES), 1)
    zero = jnp.zeros((2 * n_new, 1), F32)
    fns = [_pair_fns(qs, _pad_rows(kn_ref[0, :, sl], LANES), _pad_rows(vn_ref[0, :, sl], LANES))
           for qs, sl in zip(qss, sls)]
    state = _sb_rows_block([f for f, _ in fns], [g for _, g in fns], _later_matrix(LANES), [zero] * pairs,
                           cols < rows)
    u = _later_matrix(tk)
    heads = 2 * pairs
    hd = LANES // 2
    pad = jnp.zeros((n_new, hd), F32)

    def head_rows(ref, j, h):
        return ref[0, pl.ds(j * tk * heads + h, tk, stride=heads), :].astype(BF16)

    for j in reversed(range(window // tk)):
        def fns_of(p, qs, j=j):
            q0, q1 = qs[:n_new, :hd], qs[n_new:, hd:]

            def score():
                return jnp.concatenate(
                    [lax.dot_general(q0, head_rows(kw_ref, j, 2 * p), _NT, preferred_element_type=F32),
                     lax.dot_general(q1, head_rows(kw_ref, j, 2 * p + 1), _NT, preferred_element_type=F32)], axis=0)

            def value(a):
                pv0 = jnp.dot(a[:n_new], head_rows(vw_ref, j, 2 * p), preferred_element_type=F32)
                pv1 = jnp.dot(a[n_new:], head_rows(vw_ref, j, 2 * p + 1), preferred_element_type=F32)
                return jnp.concatenate([jnp.concatenate([pv0, pad], axis=1),
                                        jnp.concatenate([pad, pv1], axis=1)], axis=0)

            return score, value

        fns = [fns_of(p, qs) for p, qs in enumerate(qss)]
        blk = _sb_rows_block([f for f, _ in fns], [g for _, g in fns], u, [c for _, c in state], None)
        state = [(acc + pv, c) for (acc, _), (pv, c) in zip(state, blk)]
    for p, (acc, c) in enumerate(state):
        acc_ref[0, p] = acc
        c_ref[0, p] = jnp.broadcast_to(c, (2 * n_new, LANES))


def _sb_decode_rest_kernel(q_ref, kc_ref, vc_ref, acc_in_ref, c_in_ref, acc_ref, *, tk):
    qs = _lane_halves(q_ref[0])
    acc = acc_in_ref[0, 0]
    c = c_in_ref[0, 0][:, :1]
    u = _later_matrix(tk)
    for j in reversed(range(kc_ref.shape[1] // tk)):
        def visit(acc, c, j=j):
            score, value = _pair_fns(qs, kc_ref[0, j * tk:(j + 1) * tk, :].astype(BF16),
                                     vc_ref[0, j * tk:(j + 1) * tk, :].astype(BF16))
            (pv, c), = _sb_rows_block([score], [value], u, [c], None)
            return acc + pv, c

        acc, c = lax.cond(jnp.min(c) < SB_EXIT_LOG2, visit, lambda acc, c: (acc, c), acc, c)
    acc_ref[0, 0] = acc


def _sb_decode_finish_kernel(acc_ref, o_ref, *, n_new, pairs):
    lane = lax.broadcasted_iota(jnp.int32, (n_new, LANES), 1)
    for p in range(pairs):
        acc = acc_ref[0, p]
        o_ref[0, :, p * LANES:(p + 1) * LANES] = jnp.where(lane < LANES // 2, acc[:n_new], acc[n_new:]).astype(o_ref.dtype)


def _sb_decode(q, kc4, vc4, k_new, v_new, tk, window):
    b, n, w = q.shape
    past, heads, hd = kc4.shape[1:]
    window = min(window, past)
    if past % window:
        raise NotImplementedError("the decode window must divide the cache length")
    tk = min(tk, window)
    pairs = w // LANES
    rest = past - window
    state_shape = jax.ShapeDtypeStruct((b, pairs, 2 * n, LANES), F32)
    new = pl.BlockSpec((1, n, w), lambda i: (i, 0, 0))
    win = pl.BlockSpec((1, window * heads, hd), lambda i: (i, past // window - 1, 0))
    state = pl.BlockSpec((1, pairs, 2 * n, LANES), lambda i: (i, 0, 0, 0))
    acc, c = pl.pallas_call(
        functools.partial(_sb_decode_recent_kernel, n_new=n, tk=tk, window=window, pairs=pairs),
        grid=(b,),
        in_specs=[new, win, win, new, new],
        out_specs=[state, state],
        out_shape=[state_shape, state_shape],
        compiler_params=_params(("parallel",)),
        name="sb_decode_recent",
    )(q, kc4.reshape(b, past * heads, hd), vc4.reshape(b, past * heads, hd), k_new, v_new)

    if rest:
        def older(acc, c):
            new1 = pl.BlockSpec((1, n, LANES), lambda i, g: (i, 0, g))
            cache = pl.BlockSpec((1, rest, LANES), lambda i, g: (i, 0, g))
            state1 = pl.BlockSpec((1, 1, 2 * n, LANES), lambda i, g: (i, g, 0, 0))
            return pl.pallas_call(
                functools.partial(_sb_decode_rest_kernel, tk=math.gcd(tk, rest)),
                grid=(b, pairs),
                in_specs=[new1, cache, cache, state1, state1],
                out_specs=state1,
                out_shape=state_shape,
                compiler_params=_params(("parallel", "parallel")),
                name="sb_decode_rest",
            )(q, kc4[:, :rest].reshape(b, rest, w), vc4[:, :rest].reshape(b, rest, w), acc, c)

        acc = lax.cond(jnp.min(c) < SB_EXIT_LOG2, older, lambda acc, c: acc, acc, c)

    return pl.pallas_call(
        functools.partial(_sb_decode_finish_kernel, n_new=n, pairs=pairs),
        grid=(b,),
        in_specs=[state],
        out_specs=new,
        out_shape=jax.ShapeDtypeStruct((b, n, w), BF16),
        compiler_params=_params(("parallel",)),
        name="sb_decode_finish",
    )(acc)


def _odd_out_kernel(o_ref, g_ref, x_ref, wout_ref, lng_ref, lnb_ref, y_ref):
    a = (o_ref[...].astype(F32) * g_ref[...].astype(F32)).astype(BF16)
    o = jnp.dot(a, wout_ref[...], preferred_element_type=F32)
    y_ref[...] = _layernorm(ALPHA * x_ref[...] + o, lng_ref[...], lnb_ref[...])


def _odd_out(o, gate, x2d, wout, lng, lnb, tm):
    t = x2d.shape[0]
    tm = min(tm, t)
    row = lambda w: pl.BlockSpec((tm, w), lambda i: (i, 0))
    return pl.pallas_call(
        _odd_out_kernel,
        grid=(t // tm,),
        in_specs=[row(SB_W), row(SB_W), row(D_MODEL), _const_spec(wout.shape),
                  _const_spec(lng.shape), _const_spec(lnb.shape)],
        out_specs=row(D_MODEL),
        out_shape=jax.ShapeDtypeStruct((t, D_MODEL), F32),
        compiler_params=_params(("parallel",)),
        name="odd_out",
    )(o, gate, x2d, wout, lng, lnb)


def _even_weights(w_in, q_g, w_uq, kv_g, w_uk):
    pad = jnp.zeros((D_MODEL, LANES - MLA_ROPE_DIM), w_in.dtype)
    o = np.cumsum([0, Q_LORA, KV_LORA, MLA_ROPE_DIM, MLA_W, DIFF_W, DIFF_W, DIFF_W, DIFF_W])
    w_re = jnp.concatenate([w_in[:, o[0]:o[3]], pad, w_in[:, o[3]:]], axis=1).astype(BF16)
    w_uq3 = w_uq.reshape(Q_LORA, MLA_HEADS, MLA_QK_DIM)
    wqn = w_uq3[:, :, :MLA_NOPE_DIM].reshape(Q_LORA, MLA_HEADS * MLA_NOPE_DIM).astype(BF16)
    wqr = jnp.pad(w_uq3[:, :, MLA_NOPE_DIM:], ((0, 0), (0, 0), (0, LANES - MLA_ROPE_DIM)))
    wqr = wqr.reshape(Q_LORA, MLA_HEADS * LANES).astype(BF16)
    eye = jnp.eye(MLA_HEADS, dtype=w_uk.dtype)
    wuk_bd = jnp.einsum('chn,hg->hngc', w_uk, eye).reshape(MLA_HEADS * MLA_NOPE_DIM, MLA_HEADS * KV_LORA).astype(BF16)
    return w_re, wqn, wqr, wuk_bd, q_g.reshape(1, Q_LORA), kv_g.reshape(1, KV_LORA)


def _uv_block_diag(w_uv):
    eye = jnp.eye(MLA_HEADS, dtype=w_uv.dtype)
    return jnp.einsum('chv,hg->hcgv', w_uv, eye).reshape(MLA_HEADS * KV_LORA, MLA_HEADS * MLA_V_DIM).astype(BF16)


TM_PROJ = 512
TQ_MLA = 256
TQ_DIFF = 256
TK_SOFTMAX = 512
TQ_SB = 256
SB_PAIRS = 8
TK_SB_DECODE = 256
SB_DECODE_WINDOW = 256


def kernel(x_prompt, x_sample, cache_mla_ckv, cache_mla_krope, cache_diff_k, cache_diff_v, cache_sb_k, cache_sb_v,
           w_in_even, q_norm_g, w_uq, kv_norm_g, w_uk, w_uv, diff_lambda_q, diff_lambda_k, diff_subln_g,
           w_out_even, ln_even_g, ln_even_b, w_in_odd, w_out_odd, ln_odd_g, ln_odd_b):
    bp, sp, _ = x_prompt.shape
    bs, ss, _ = x_sample.shape
    past = cache_sb_k.shape[2]
    if past % CHUNK or ss > CHUNK or (ss & (ss - 1)) or sp % TQ_SB:
        raise NotImplementedError("decode frames must share one chunk after a chunk-aligned cache")
    tp, ts = bp * sp, bs * ss
    pos_p = jnp.arange(sp, dtype=jnp.int32)
    pos_s = past + jnp.arange(ss, dtype=jnp.int32)
    pos_s_rows = jnp.tile(pos_s, bs)

    def tables(pos):
        mc, msa, msb, _ = _rope_tables(pos, MLA_ROPE_DIM, MLA_ROPE_THETA, LANES)
        dc, dsa, dsb, _ = _rope_tables(pos, DIFF_ROT, ROPE_THETA, DIFF_HD)
        return mc, msa, msb, dc, dsa, dsb

    yp = x_prompt.reshape(tp, D_MODEL)
    ys = x_sample.reshape(ts, D_MODEL)
    even_p, even_s, odd_p, odd_s = [], [], [], []
    for layer in range(DEPTH):
        i = layer // 2
        if layer % 2 == 0:
            lam_init = 0.8 - 0.6 * math.exp(-0.3 * layer)
            wts = _even_weights(w_in_even[i], q_norm_g[i], w_uq[i], kv_norm_g[i], w_uk[i])
            wuv_bd = _uv_block_diag(w_uv[i])
            sg = diff_subln_g[i].reshape(1, 2 * DIFF_HD)
            wout = w_out_even[i].astype(BF16)
            lng, lnb = ln_even_g[i].reshape(1, D_MODEL), ln_even_b[i].reshape(1, D_MODEL)
            lq, lk = diff_lambda_q[i], diff_lambda_k[i]
            qcat, kcat, ckv, krope, gate, dq, dkf, dkb, dvf, dvb = _even_in(yp, wts, tables(pos_p), sp, TM_PROJ)
            olat = _mla_prompt(qcat.reshape(bp, sp, -1), kcat.reshape(bp, sp, -1), TQ_MLA, TK_SOFTMAX)
            odiff = _diff_prompt(dq.reshape(bp, sp, -1), dkb.reshape(bp, sp, -1), dvb.reshape(bp, sp, -1),
                                 lq, lk, lam_init, TQ_DIFF, TK_SOFTMAX)
            yp_new = _even_out(olat.reshape(tp, -1), odiff.reshape(tp, -1), gate, yp, wuv_bd, sg, wout, lng, lnb,
                               lam_init, TM_PROJ)
            even_p.append((ckv.reshape(bp, sp, KV_LORA), krope.reshape(bp, sp, MLA_ROPE_DIM),
                           dkf.reshape(bp, sp, DIFF_HEADS, 2, DIFF_HD), dvf.reshape(bp, sp, DIFF_HEADS, 2 * DIFF_HD)))
            qcat, kcat, ckv, krope, gate, dq, dkf, dkb, dvf, dvb = _even_in(ys, wts, tables(pos_s_rows), ts, TM_PROJ)
            olat = _mla_decode(qcat.reshape(bs, ss, -1), cache_mla_ckv[i], cache_mla_krope[i],
                               kcat.reshape(bs, ss, -1))
            odiff = _diff_decode(dq.reshape(bs, ss, -1), cache_diff_k[i].reshape(bs, past, DIFF_W), cache_diff_v[i],
                                 dkb.reshape(bs, ss, -1),
                                 dvb.reshape(bs, ss, -1), lq, lk, lam_init)
            ys_new = _even_out(olat.reshape(ts, -1), odiff.reshape(ts, -1), gate, ys, wuv_bd, sg, wout, lng, lnb,
                               lam_init, TM_PROJ)
            even_s.append((ckv.reshape(bs, ss, KV_LORA), krope.reshape(bs, ss, MLA_ROPE_DIM),
                           dkf.reshape(bs, ss, DIFF_HEADS, 2, DIFF_HD), dvf.reshape(bs, ss, DIFF_HEADS, 2 * DIFF_HD)))
            yp, ys = yp_new, ys_new
        else:
            w_in = w_in_odd[i].astype(BF16)
            wout = w_out_odd[i].astype(BF16)
            lng, lnb = ln_odd_g[i].reshape(1, D_MODEL), ln_odd_b[i].reshape(1, D_MODEL)
            q, kf, kb, vf, vb, gate = _odd_in(yp, w_in, TM_PROJ)
            o = _sb_prompt(q.reshape(bp, sp, -1), kb.reshape(bp, sp, -1), vb.reshape(bp, sp, -1), TQ_SB, SB_PAIRS)
            yp_new = _odd_out(o.reshape(tp, -1), gate, yp, wout, lng, lnb, TM_PROJ)
            odd_p.append((kf.reshape(bp, sp, SB_HEADS, SB_HD), vf.reshape(bp, sp, SB_HEADS, SB_HD)))
            q, kf, kb, vf, vb, gate = _odd_in(ys, w_in, TM_PROJ)
            o = _sb_decode(q.reshape(bs, ss, -1), cache_sb_k[i], cache_sb_v[i], kb.reshape(bs, ss, -1),
                           vb.reshape(bs, ss, -1), TK_SB_DECODE, SB_DECODE_WINDOW)
            ys_new = _odd_out(o.reshape(ts, -1), gate, ys, wout, lng, lnb, TM_PROJ)
            odd_s.append((kf.reshape(bs, ss, SB_HEADS, SB_HD), vf.reshape(bs, ss, SB_HEADS, SB_HD)))
            yp, ys = yp_new, ys_new
    stack = lambda lst, k: jnp.stack([t[k] for t in lst])
    return (yp.reshape(bp, sp, D_MODEL), ys.reshape(bs, ss, D_MODEL),
            stack(even_p, 0), stack(even_p, 1), stack(even_p, 2), stack(even_p, 3),
            stack(odd_p, 0), stack(odd_p, 1),
            stack(even_s, 0), stack(even_s, 1), stack(even_s, 2), stack(even_s, 3),
            stack(odd_s, 0), stack(odd_s, 1))
```

```python
import functools
import math

import numpy as np
import jax
import jax.numpy as jnp
from jax import lax
from jax.experimental import pallas as pl
from jax.experimental.pallas import tpu as pltpu

F32 = jnp.float32
BF16 = jnp.bfloat16

D_MODEL = 1024
DEPTH = 2
CHUNK = 64
ALPHA = (2 * DEPTH) ** 0.25
EPS = 1e-5
NEG_INF = -1e30
MLA_HEADS = 8
Q_LORA = 256
KV_LORA = 128
MLA_NOPE_DIM = 64
MLA_ROPE_DIM = 32
MLA_QK_DIM = MLA_NOPE_DIM + MLA_ROPE_DIM
MLA_V_DIM = 64
MLA_ROPE_THETA = 10000.0
MLA_W = MLA_HEADS * MLA_V_DIM
DIFF_HEADS = 4
DIFF_HD = 64
DIFF_ROT = DIFF_HD // 4
ROPE_THETA = 500000.0
DIFF_W = DIFF_HEADS * 2 * DIFF_HD
SB_HEADS = 16
SB_HD = 64
SB_W = SB_HEADS * SB_HD

LANES = 128
MLA_QCAT = 2 * LANES
VMEM_LIMIT = 56 * 1024 * 1024

_NT = (((1,), (1,)), ((), ()))
LOG2E = math.log2(math.e)


def _params(sem):
    return pltpu.CompilerParams(dimension_semantics=sem, vmem_limit_bytes=VMEM_LIMIT)


def _const_spec(shape):
    nd = len(shape)
    return pl.BlockSpec(shape, lambda *_: (0,) * nd, pipeline_mode=pl.Buffered(1))


def _rope_tables(pos, rot_dim, theta, period):
    half = rot_dim // 2
    inv = jnp.exp(-math.log(theta) * jnp.arange(half, dtype=F32) * (2.0 / rot_dim))
    ang = pos.astype(F32)[:, None] * inv[None, :]
    cos, sin = jnp.cos(ang), jnp.sin(ang)
    lane = np.arange(LANES) % period
    idx = lane % half
    in_rot = lane < rot_dim
    first = lane < half
    second = in_rot & ~first
    cos_l, sin_l = cos[:, idx], sin[:, idx]
    c = jnp.where(in_rot[None, :], cos_l, 1.0)
    sa = jnp.where(first[None, :], -sin_l, 0.0)
    sb = jnp.where(second[None, :], sin_l, 0.0)
    return c, sa, sb, half


def _apply_rope(x, c, sa, sb, half):
    return x * c + pltpu.roll(x, LANES - half, 1) * sa + pltpu.roll(x, half, 1) * sb


def _silu(x):
    return x / (1.0 + jnp.exp(-x))


def _rms(x, g):
    ms = jnp.mean(x * x, axis=-1, keepdims=True)
    return x * lax.rsqrt(ms + EPS) * g


def _layernorm(r, g, b):
    mu = jnp.mean(r, axis=-1, keepdims=True)
    rc = r - mu
    var = jnp.mean(rc * rc, axis=-1, keepdims=True)
    return rc * lax.rsqrt(var + EPS) * g + b


_C_QLAT = 0
_C_CKV = Q_LORA
_C_KROPE = _C_CKV + KV_LORA
_C_GMLA = _C_KROPE + LANES
_C_DQ = _C_GMLA + MLA_W
_C_DK = _C_DQ + DIFF_W
_C_DV = _C_DK + DIFF_W
_C_GDIFF = _C_DV + DIFF_W
_EVEN_COLS = _C_GDIFF + DIFF_W


def _even_in_kernel(x_ref, w_ref, wqn_ref, wqr_ref, wuk_ref, qg_ref, kvg_ref,
                    mc_ref, msa_ref, msb_ref, dc_ref, dsa_ref, dsb_ref,
                    qcat_ref, kcat_ref, ckv_ref, krope_ref, gate_ref, dq_ref,
                    dkf_ref, dkb_ref, dvf_ref, dvb_ref):
    xb = x_ref[...].astype(BF16)

    def proj(c0, c1):
        return jnp.dot(xb, w_ref[:, c0:c1], preferred_element_type=F32)

    mla_half = MLA_ROPE_DIM // 2
    diff_half = DIFF_ROT // 2
    mc, msa, msb = mc_ref[...], msa_ref[...], msb_ref[...]
    dc, dsa, dsb = dc_ref[...], dsa_ref[...], dsb_ref[...]

    qn = _rms(proj(_C_QLAT, _C_QLAT + Q_LORA), qg_ref[...]).astype(BF16)
    q_nope = jnp.dot(qn, wqn_ref[...], preferred_element_type=F32).astype(BF16)
    q_abs = jnp.dot(q_nope, wuk_ref[...], preferred_element_type=F32)
    q_rp = jnp.dot(qn, wqr_ref[...], preferred_element_type=F32)
    scale = MLA_QK_DIM ** -0.5 * LOG2E
    for h in range(MLA_HEADS):
        rot = _apply_rope(q_rp[:, h * LANES:(h + 1) * LANES], mc, msa, msb, mla_half)
        qcat_ref[:, h * MLA_QCAT:h * MLA_QCAT + LANES] = (q_abs[:, h * LANES:(h + 1) * LANES] * scale).astype(BF16)
        qcat_ref[:, h * MLA_QCAT + LANES:(h + 1) * MLA_QCAT] = (rot * scale).astype(BF16)

    kv = proj(_C_CKV, _C_GMLA)
    ckv = _rms(kv[:, :KV_LORA], kvg_ref[...])
    krope = _apply_rope(kv[:, KV_LORA:], mc, msa, msb, mla_half)
    ckv_ref[...] = ckv
    krope_ref[...] = krope[:, :MLA_ROPE_DIM]
    kcat_ref[:, :LANES] = ckv.astype(BF16)
    kcat_ref[:, LANES:] = krope.astype(BF16)

    gate_ref[:, :MLA_W] = _silu(proj(_C_GMLA, _C_DQ)).astype(BF16)
    gate_ref[:, MLA_W:] = _silu(proj(_C_GDIFF, _EVEN_COLS)).astype(BF16)

    dq = proj(_C_DQ, _C_DK)
    dk = proj(_C_DK, _C_DV)
    dv = proj(_C_DV, _C_GDIFF)
    dscale = DIFF_HD ** -0.5 * LOG2E
    tm = x_ref.shape[0]
    for h in range(DIFF_HEADS):
        sl = slice(h * LANES, (h + 1) * LANES)
        dq_ref[:, sl] = (_apply_rope(dq[:, sl], dc, dsa, dsb, diff_half) * dscale).astype(BF16)
        kr = _apply_rope(dk[:, sl], dc, dsa, dsb, diff_half)
        dkf_ref[pl.ds(2 * h, tm, stride=2 * DIFF_HEADS), :] = kr[:, :DIFF_HD]
        dkf_ref[pl.ds(2 * h + 1, tm, stride=2 * DIFF_HEADS), :] = kr[:, DIFF_HD:]
        dkb_ref[:, sl] = kr.astype(BF16)
        dvf_ref[pl.ds(h, tm, stride=DIFF_HEADS), :] = dv[:, sl]
    dvb_ref[...] = dv.astype(BF16)


def _even_in(x2d, wts, tabs, tab_rows_per_rep, tm):
    t = x2d.shape[0]
    tm = min(tm, t)
    nrep = tab_rows_per_rep // tm
    w_in, wqn, wqr, wuk, qg, kvg = wts
    row = lambda w: pl.BlockSpec((tm, w), lambda i: (i, 0))
    tab = pl.BlockSpec((tm, LANES), lambda i: (i % nrep, 0))
    outs = [
        (1, MLA_HEADS * MLA_QCAT, BF16),
        (1, MLA_QCAT, BF16),
        (1, KV_LORA, F32),
        (1, MLA_ROPE_DIM, F32),
        (1, MLA_W + DIFF_W, BF16),
        (1, DIFF_W, BF16),
        (2 * DIFF_HEADS, DIFF_HD, F32),
        (1, DIFF_W, BF16),
        (DIFF_HEADS, 2 * DIFF_HD, F32),
        (1, DIFF_W, BF16),
    ]
    return pl.pallas_call(
        _even_in_kernel,
        grid=(t // tm,),
        in_specs=[row(D_MODEL), _const_spec(w_in.shape), _const_spec(wqn.shape), _const_spec(wqr.shape),
                  _const_spec(wuk.shape), _const_spec(qg.shape), _const_spec(kvg.shape)] + [tab] * 6,
        out_specs=[pl.BlockSpec((tm * r, w), lambda i: (i, 0)) for r, w, _ in outs],
        out_shape=[jax.ShapeDtypeStruct((t * r, w), d) for r, w, d in outs],
        compiler_params=_params(("parallel",)),
        name="even_in",
    )(x2d, w_in, wqn, wqr, wuk, qg, kvg, *tabs)


def _softmax_step(s, vt, m_ref, l_ref, acc_ref, valid):
    if valid is not None:
        s = jnp.where(valid, s, NEG_INF)
    m_old = m_ref[...]
    m_new = jnp.maximum(m_old, jnp.max(s, axis=0, keepdims=True))
    alpha = jnp.exp2(m_old - m_new)
    p = jnp.exp2(s - m_new)
    l_ref[...] = alpha * l_ref[...] + jnp.sum(p, axis=0, keepdims=True)
    acc_ref[...] = alpha * acc_ref[...] + jnp.dot(vt, p.astype(BF16), preferred_element_type=F32)
    m_ref[...] = m_new


def _flash_scratch(n, tk, m_rows, dv):
    one = [pltpu.VMEM((tk, m_rows), F32), pltpu.VMEM((tk, m_rows), F32), pltpu.VMEM((1, m_rows), F32),
           pltpu.VMEM((1, m_rows), F32), pltpu.VMEM((dv, m_rows), F32)]
    return one * n


def _chunk_causal_flash(problems, q0, tq, tk):
    m_rows = problems[0][0].shape[0]
    for _, _, _, (_, _, m_ref, l_ref, acc_ref) in problems:
        m_ref[...] = jnp.full(m_ref.shape, NEG_INF, F32)
        l_ref[...] = jnp.zeros(l_ref.shape, F32)
        acc_ref[...] = jnp.zeros(acc_ref.shape, F32)
    n_full = q0 // tk

    def scores(prob, j):
        qs, get_k = prob[0], prob[1]
        return lax.dot_general(get_k(pl.multiple_of(j * tk, tk)), qs, _NT, preferred_element_type=F32)

    def consume(j, slot, valid):
        for _, _, get_vt, scr in problems:
            _softmax_step(scr[slot][...], get_vt(j), scr[2], scr[3], scr[4], valid)

    def step(j, slot):
        for prob in problems:
            prob[3][1 - slot][...] = scores(prob, j + 1)
        consume(j, slot, None)

    for prob in problems:
        prob[3][0][...] = scores(prob, 0)

    def body(jj, carry):
        step(2 * jj, 0)
        step(2 * jj + 1, 1)
        return carry

    lax.fori_loop(0, n_full // 2, body, 0)
    kpos = n_full * tk + lax.broadcasted_iota(jnp.int32, (tk, m_rows), 0)
    qpos = q0 + (lax.broadcasted_iota(jnp.int32, (tk, m_rows), 1) & (tq - 1))
    shift = CHUNK.bit_length() - 1
    valid = (kpos >> shift) <= (qpos >> shift)

    @pl.when(n_full % 2 == 1)
    def _():
        step(n_full - 1, 0)
        consume(n_full, 1, valid)

    @pl.when(n_full % 2 == 0)
    def _():
        consume(n_full, 0, valid)


MLA_GROUPS = 1


def _mla_prompt_kernel(q_ref, k_ref, vt_ref, o_ref, *scratch, tq, tk):
    q0 = pl.program_id(1) * tq
    get_k = lambda k0: k_ref[0, pl.ds(k0, tk), :]
    get_vt = lambda j: vt_ref[0, j]
    per = MLA_HEADS // MLA_GROUPS
    problems = []
    for g in range(MLA_GROUPS):
        qs = jnp.concatenate([q_ref[0, :, h * MLA_QCAT:(h + 1) * MLA_QCAT] for h in range(g * per, (g + 1) * per)],
                             axis=0)
        problems.append((qs, get_k, get_vt, scratch[5 * g:5 * g + 5]))
    _chunk_causal_flash(problems, q0, tq, tk)
    for g, (_, _, _, (_, _, _, l_ref, acc_ref)) in enumerate(problems):
        out = acc_ref[...] / l_ref[...]
        for hh in range(per):
            h = g * per + hh
            o_ref[0, :, h * KV_LORA:(h + 1) * KV_LORA] = out[:, hh * tq:(hh + 1) * tq].T.astype(o_ref.dtype)


def _key_major(x, tk):
    b, s, w = x.shape
    return x.reshape(b, s // tk, tk, w).transpose(0, 1, 3, 2)


def _mla_prompt(qcat, kcat, tq, tk):
    b, s, _ = qcat.shape
    tq, tk = min(tq, s), min(tk, s)
    m_rows = MLA_HEADS * tq
    vt = _key_major(kcat[:, :, :KV_LORA], tk)
    return pl.pallas_call(
        functools.partial(_mla_prompt_kernel, tq=tq, tk=tk),
        grid=(b, s // tq),
        in_specs=[pl.BlockSpec((1, tq, MLA_HEADS * MLA_QCAT), lambda bi, i: (bi, i, 0)),
                  pl.BlockSpec((1, s, MLA_QCAT), lambda bi, i: (bi, 0, 0)),
                  pl.BlockSpec((1, s // tk, KV_LORA, tk), lambda bi, i: (bi, 0, 0, 0))],
        out_specs=pl.BlockSpec((1, tq, MLA_HEADS * KV_LORA), lambda bi, i: (bi, i, 0)),
        out_shape=jax.ShapeDtypeStruct((b, s, MLA_HEADS * KV_LORA), BF16),
        scratch_shapes=_flash_scratch(MLA_GROUPS, tk, m_rows // MLA_GROUPS, KV_LORA),
        compiler_params=_params(("parallel", "parallel")),
        name="mla_prompt",
    )(qcat, kcat, vt)


def _lane_halves(x):
    lane = lax.broadcasted_iota(jnp.int32, x.shape, 1)
    zero = jnp.zeros_like(x)
    return jnp.concatenate([jnp.where(lane < LANES // 2, x, zero), jnp.where(lane < LANES // 2, zero, x)], axis=0)


def _diff_lambda(lq_ref, lk_ref, lam_init):
    e = jnp.exp(jnp.sum(lq_ref[...] * lk_ref[...], axis=-1, keepdims=True))
    return e[0:1] - e[1:2] + lam_init


def _diff_prompt_kernel(q_ref, k_ref, vt_ref, lq_ref, lk_ref, o_ref, *scratch, tq, tk, lam_init):
    q0 = pl.program_id(1) * tq
    lam = _diff_lambda(lq_ref, lk_ref, lam_init)
    problems = []
    for h in range(DIFF_HEADS):
        sl = slice(h * LANES, (h + 1) * LANES)
        problems.append((_lane_halves(q_ref[0, :, sl]),
                         lambda k0, sl=sl: k_ref[0, pl.ds(k0, tk), sl],
                         lambda j, sl=sl: vt_ref[0, j, sl, :],
                         scratch[5 * h:5 * h + 5]))
    _chunk_causal_flash(problems, q0, tq, tk)
    for h, (_, _, _, (_, _, _, l_ref, acc_ref)) in enumerate(problems):
        o = acc_ref[...] / l_ref[...]
        o_ref[0, :, h * LANES:(h + 1) * LANES] = (o[:, :tq] - lam * o[:, tq:]).T


def _diff_prompt(dq, dk, dv, lq, lk, lam_init, tq, tk):
    b, s, _ = dq.shape
    tq, tk = min(tq, s), min(tk, s)
    qspec = pl.BlockSpec((1, tq, DIFF_W), lambda bi, i: (bi, i, 0))
    return pl.pallas_call(
        functools.partial(_diff_prompt_kernel, tq=tq, tk=tk, lam_init=lam_init),
        grid=(b, s // tq),
        in_specs=[qspec, pl.BlockSpec((1, s, DIFF_W), lambda bi, i: (bi, 0, 0)),
                  pl.BlockSpec((1, s // tk, DIFF_W, tk), lambda bi, i: (bi, 0, 0, 0)),
                  _const_spec(lq.shape), _const_spec(lk.shape)],
        out_specs=qspec,
        out_shape=jax.ShapeDtypeStruct((b, s, DIFF_W), F32),
        scratch_shapes=_flash_scratch(DIFF_HEADS, tk, 2 * tq, LANES),
        compiler_params=_params(("parallel", "parallel")),
        name="diff_prompt",
    )(dq, dk, _key_major(dv, tk), lq, lk)


def _pad_rows(x, rows):
    return jnp.concatenate([x, jnp.zeros((rows - x.shape[0], x.shape[1]), x.dtype)], axis=0)


def _decode_softmax(s_c, s_n, v_cache, v_new, n_new):
    col = lax.broadcasted_iota(jnp.int32, s_n.shape, 1)
    s_n = jnp.where(col < n_new, s_n, NEG_INF)
    m = jnp.maximum(jnp.max(s_c, axis=-1, keepdims=True), jnp.max(s_n, axis=-1, keepdims=True))
    p_c = jnp.exp2(s_c - m)
    p_n = jnp.exp2(s_n - m)
    l = jnp.sum(p_c, axis=-1, keepdims=True) + jnp.sum(p_n, axis=-1, keepdims=True)
    acc = (jnp.dot(p_c.astype(BF16), v_cache, preferred_element_type=F32)
           + jnp.dot(p_n.astype(BF16), v_new, preferred_element_type=F32))
    return acc / l


def _mla_decode_kernel(q_ref, ckv_ref, kr_ref, kn_ref, o_ref, *, n_new):
    qs = jnp.concatenate([q_ref[0, :, h * MLA_QCAT:(h + 1) * MLA_QCAT] for h in range(MLA_HEADS)], axis=0)
    ckv = ckv_ref[0].astype(BF16)
    kr = kr_ref[0].astype(BF16)
    k_new = _pad_rows(kn_ref[0], LANES)
    s_c = (lax.dot_general(qs[:, :KV_LORA], ckv, _NT, preferred_element_type=F32)
           + lax.dot_general(qs[:, KV_LORA:KV_LORA + MLA_ROPE_DIM], kr, _NT, preferred_element_type=F32))
    s_n = lax.dot_general(qs, k_new, _NT, preferred_element_type=F32)
    out = _decode_softmax(s_c, s_n, ckv, k_new[:, :KV_LORA], n_new)
    for h in range(MLA_HEADS):
        o_ref[0, :, h * KV_LORA:(h + 1) * KV_LORA] = out[h * n_new:(h + 1) * n_new].astype(o_ref.dtype)


def _mla_decode(qcat, ckv_c, kr_c, kcat_new):
    b, n, _ = qcat.shape
    past = ckv_c.shape[1]
    return pl.pallas_call(
        functools.partial(_mla_decode_kernel, n_new=n),
        grid=(b,),
        in_specs=[pl.BlockSpec((1, n, MLA_HEADS * MLA_QCAT), lambda i: (i, 0, 0)),
                  pl.BlockSpec((1, past, KV_LORA), lambda i: (i, 0, 0)),
                  pl.BlockSpec((1, past, MLA_ROPE_DIM), lambda i: (i, 0, 0)),
                  pl.BlockSpec((1, n, MLA_QCAT), lambda i: (i, 0, 0))],
        out_specs=pl.BlockSpec((1, n, MLA_HEADS * KV_LORA), lambda i: (i, 0, 0)),
        out_shape=jax.ShapeDtypeStruct((b, n, MLA_HEADS * KV_LORA), BF16),
        compiler_params=_params(("parallel",)),
        name="mla_decode",
    )(qcat, ckv_c, kr_c, kcat_new)


def _diff_decode_kernel(q_ref, kc_ref, vc_ref, kn_ref, vn_ref, lq_ref, lk_ref, o_ref, *, n_new, past, lam_init):
    lam = _diff_lambda(lq_ref, lk_ref, lam_init)
    for h in range(DIFF_HEADS):
        sl = slice(h * LANES, (h + 1) * LANES)
        q = q_ref[0, :, sl]
        qs = _lane_halves(q)
        v = vc_ref[0, pl.ds(h, past, stride=DIFF_HEADS), :].astype(BF16)
        s_c = jnp.concatenate(
            [jnp.dot(q[:, :DIFF_HD], kc_ref[0, h, 0].astype(BF16), preferred_element_type=F32),
             jnp.dot(q[:, DIFF_HD:], kc_ref[0, h, 1].astype(BF16), preferred_element_type=F32)], axis=0)
        s_n = lax.dot_general(qs, _pad_rows(kn_ref[0, :, sl], LANES), _NT, preferred_element_type=F32)
        o = _decode_softmax(s_c, s_n, v, _pad_rows(vn_ref[0, :, sl], LANES), n_new)
        o_ref[0, :, sl] = o[:n_new] - lam * o[n_new:]


def _diff_decode(dq, kc, vc, dk_new, dv_new, lq, lk, lam_init):
    b, n, _ = dq.shape
    past = kc.shape[-1]
    new = pl.BlockSpec((1, n, DIFF_W), lambda i: (i, 0, 0))
    return pl.pallas_call(
        functools.partial(_diff_decode_kernel, n_new=n, past=past, lam_init=lam_init),
        grid=(b,),
        in_specs=[new, pl.BlockSpec((1, DIFF_HEADS, 2, DIFF_HD, past), lambda i: (i, 0, 0, 0, 0)),
                  pl.BlockSpec((1, past * DIFF_HEADS, 2 * DIFF_HD), lambda i: (i, 0, 0)),
                  new, new, _const_spec(lq.shape), _const_spec(lk.shape)],
        out_specs=new,
        out_shape=jax.ShapeDtypeStruct((b, n, DIFF_W), F32),
        compiler_params=_params(("parallel",)),
        name="diff_decode",
    )(dq, kc, vc.reshape(b, past * DIFF_HEADS, 2 * DIFF_HD), dk_new, dv_new, lq, lk)


def _even_out_kernel(olat_ref, odiff_ref, gate_ref, x_ref, wuv_ref, sg_ref, wout_ref, lng_ref, lnb_ref, y_ref,
                     *, lam_init):
    gate = gate_ref[...]
    o_mla = jnp.dot(olat_ref[...], wuv_ref[...], preferred_element_type=F32)
    parts = [(o_mla * gate[:, :MLA_W].astype(F32)).astype(BF16)]
    od = odiff_ref[...]
    for h in range(DIFF_HEADS):
        sl = slice(h * LANES, (h + 1) * LANES)
        n = _rms(od[:, sl], sg_ref[...]) * (1.0 - lam_init)
        parts.append((n * gate[:, MLA_W + h * LANES:MLA_W + (h + 1) * LANES].astype(F32)).astype(BF16))
    o = jnp.dot(jnp.concatenate(parts, axis=1), wout_ref[...], preferred_element_type=F32)
    y_ref[...] = _layernorm(ALPHA * x_ref[...] + o, lng_ref[...], lnb_ref[...])


def _even_out(olat, odiff, gate, x2d, wuv_bd, sg, wout, lng, lnb, lam_init, tm):
    t = x2d.shape[0]
    tm = min(tm, t)
    row = lambda w: pl.BlockSpec((tm, w), lambda i: (i, 0))
    return pl.pallas_call(
        functools.partial(_even_out_kernel, lam_init=lam_init),
        grid=(t // tm,),
        in_specs=[row(MLA_HEADS * KV_LORA), row(DIFF_W), row(MLA_W + DIFF_W), row(D_MODEL),
                  _const_spec(wuv_bd.shape), _const_spec(sg.shape), _const_spec(wout.shape),
                  _const_spec(lng.shape), _const_spec(lnb.shape)],
        out_specs=row(D_MODEL),
        out_shape=jax.ShapeDtypeStruct((t, D_MODEL), F32),
        compiler_params=_params(("parallel",)),
        name="even_out",
    )(olat, odiff, gate, x2d, wuv_bd, sg, wout, lng, lnb)


def _odd_in_kernel(x_ref, w_ref, q_ref, kf_ref, kb_ref, vf_ref, vb_ref, g_ref):
    xb = x_ref[...].astype(BF16)

    def proj(i):
        return jnp.dot(xb, w_ref[:, i * SB_W:(i + 1) * SB_W], preferred_element_type=F32)

    tm = x_ref.shape[0]

    def store_heads(ref, val):
        for h in range(SB_HEADS):
            ref[pl.ds(h, tm, stride=SB_HEADS), :] = val[:, h * SB_HD:(h + 1) * SB_HD]

    q_ref[...] = (proj(0) * (SB_HD ** -0.5 * LOG2E)).astype(BF16)
    k = proj(1)
    store_heads(kf_ref, k)
    kb_ref[...] = k.astype(BF16)
    v = proj(2)
    store_heads(vf_ref, v)
    vb_ref[...] = v.astype(BF16)
    g_ref[...] = _silu(proj(3)).astype(BF16)


def _odd_in(x2d, w_in, tm):
    t = x2d.shape[0]
    tm = min(tm, t)
    row = pl.BlockSpec((tm, SB_W), lambda i: (i, 0))
    heads = pl.BlockSpec((tm * SB_HEADS, SB_HD), lambda i: (i, 0))
    wide = lambda d: jax.ShapeDtypeStruct((t, SB_W), d)
    tall = jax.ShapeDtypeStruct((t * SB_HEADS, SB_HD), F32)
    return pl.pallas_call(
        _odd_in_kernel,
        grid=(t // tm,),
        in_specs=[pl.BlockSpec((tm, D_MODEL), lambda i: (i, 0)), _const_spec(w_in.shape)],
        out_specs=[row, heads, row, heads, row, row],
        out_shape=[wide(BF16), tall, wide(BF16), tall, wide(BF16), wide(BF16)],
        compiler_params=_params(("parallel",)),
        name="odd_in",
    )(x2d, w_in)


SB_EXIT_LOG2 = 160.0


def _log2(x):
    return jnp.log(x) * LOG2E


def _later_matrix(n):
    return (lax.broadcasted_iota(jnp.int32, (n, n), 0) > lax.broadcasted_iota(jnp.int32, (n, n), 1)).astype(BF16)


def _softplus2(z):
    return jnp.maximum(z, 0.0) + _log2(1.0 + jnp.exp2(-jnp.abs(z)))


def _sb_prompt_kernel(q_ref, k_ref, vt_ref, o_ref, c_ref, acc_ref, x_ref, sp0_ref, *, tq, pairs):
    i = pl.program_id(2)
    ut = (lax.broadcasted_iota(jnp.int32, (tq, tq), 1) > lax.broadcasted_iota(jnp.int32, (tq, tq), 0)).astype(BF16)
    keys = lax.broadcasted_iota(jnp.int32, (tq, 2 * tq), 0)
    queries = lax.broadcasted_iota(jnp.int32, (tq, 2 * tq), 1) & (tq - 1)
    diag_valid = keys < queries
    half = LANES // 2
    sls = [slice(p * LANES, (p + 1) * LANES) for p in range(pairs)]
    qss = [_lane_halves(q_ref[0, :, sl]) for sl in sls]

    def all_pairs(j, valid, first):
        k0 = pl.multiple_of(j * tq, tq)
        zs = [lax.dot_general(k_ref[0, pl.ds(k0, tq), sl], qss[p], _NT, preferred_element_type=F32)
              for p, sl in enumerate(sls)]
        laters = []
        for p, z in enumerate(zs):
            sp = _softplus2(z)
            x_ref[p] = z - sp
            if valid is not None:
                sp = jnp.where(valid, sp, 0.0)
            spb = sp.astype(BF16)
            sp0_ref[p] = spb[:1, :].astype(F32)
            laters.append(jnp.dot(ut, spb, preferred_element_type=F32))
        cmin = None
        for p, sl in enumerate(sls):
            a = jnp.exp2(x_ref[p] - laters[p])
            if valid is not None:
                a = jnp.where(valid, a, 0.0)
            pv = jnp.dot(vt_ref[0, j, sl, :], a.astype(BF16), preferred_element_type=F32)
            tot = laters[p][:1, :] + sp0_ref[p]
            if first:
                acc_ref[p] = pv
                c = tot
            else:
                c0 = c_ref[p]
                acc_ref[p] += pv * jnp.exp2(-c0)
                c = c0 + tot
            c_ref[p] = c
            cmin = c if cmin is None else jnp.minimum(cmin, c)
        return jnp.min(cmin)

    cmin = all_pairs(i, diag_valid, True)
    lax.while_loop(lambda jc: (jc[0] >= 0) & (jc[1] < SB_EXIT_LOG2),
                   lambda jc: (jc[0] - 1, all_pairs(jc[0], None, False)),
                   (i - 1, cmin))

    for p, sl in enumerate(sls):
        acc = acc_ref[p]
        o_t = jnp.concatenate([acc[:half, :tq], acc[half:, tq:]], axis=0)
        o_ref[0, :, sl] = o_t.T.astype(o_ref.dtype)


def _sb_prompt(q, k, v, tq, pairs):
    b, s, w = q.shape
    tq = min(tq, s)
    gw = pairs * LANES
    qspec = pl.BlockSpec((1, tq, gw), lambda bi, g, i: (bi, i, g))
    return pl.pallas_call(
        functools.partial(_sb_prompt_kernel, tq=tq, pairs=pairs),
        grid=(b, w // gw, s // tq),
        in_specs=[qspec, pl.BlockSpec((1, s, gw), lambda bi, g, i: (bi, 0, g)),
                  pl.BlockSpec((1, s // tq, gw, tq), lambda bi, g, i: (bi, 0, g, 0))],
        out_specs=qspec,
        out_shape=jax.ShapeDtypeStruct((b, s, w), BF16),
        scratch_shapes=[pltpu.VMEM((pairs, 1, 2 * tq), F32), pltpu.VMEM((pairs, LANES, 2 * tq), F32),
                        pltpu.VMEM((pairs, tq, 2 * tq), F32), pltpu.VMEM((pairs, 1, 2 * tq), F32)],
        compiler_params=_params(("parallel", "parallel", "parallel")),
        name="sb_prompt",
    )(q, k, _key_major(v, tq))


def _sb_rows_block(score_fns, value_fns, u, cs, valid):
    zs = [f() for f in score_fns]
    xs, sp0s, laters = [], [], []
    for z in zs:
        sp = _softplus2(z)
        xs.append(z - sp)
        if valid is not None:
            sp = jnp.where(valid, sp, 0.0)
        spb = sp.astype(BF16)
        sp0s.append(spb[:, :1].astype(F32))
        laters.append(jnp.dot(spb, u, preferred_element_type=F32))
    out = []
    for x, sp0, later, c, value in zip(xs, sp0s, laters, cs, value_fns):
        a = jnp.exp2(x - later - c)
        if valid is not None:
            a = jnp.where(valid, a, 0.0)
        out.append((value(a.astype(BF16)), c + later[:, :1] + sp0))
    return out


def _pair_fns_new(qs, kb, vb):
    return (lambda: lax.dot_general(qs, kb, _NT, preferred_element_type=F32),
            lambda a: jnp.dot(a, vb, preferred_element_type=F32))


def _pair_fns_cache(qs, kt_ref, vt_ref, h0, t0, tk, n_new):
    hd = LANES // 2
    q0, q1 = qs[:n_new, :hd], qs[n_new:, hd:]
    pad = jnp.zeros((n_new, hd), F32)

    def score():
        return jnp.concatenate(
            [jnp.dot(q0, kt_ref[0, h0, :, t0:t0 + tk].astype(BF16), preferred_element_type=F32),
             jnp.dot(q1, kt_ref[0, h0 + 1, :, t0:t0 + tk].astype(BF16), preferred_element_type=F32)], axis=0)

    def value(a):
        pv0 = lax.dot_general(a[:n_new], vt_ref[0, h0, :, t0:t0 + tk].astype(BF16), _NT, preferred_element_type=F32)
        pv1 = lax.dot_general(a[n_new:], vt_ref[0, h0 + 1, :, t0:t0 + tk].astype(BF16), _NT,
                              preferred_element_type=F32)
        return jnp.concatenate([jnp.concatenate([pv0, pad], axis=1), jnp.concatenate([pad, pv1], axis=1)], axis=0)

    return score, value


def _sb_decode_recent_kernel(q_ref, kw_ref, vw_ref, kn_ref, vn_ref, acc_ref, c_ref, *, n_new, tk, pairs):
    sls = [slice(p * LANES, (p + 1) * LANES) for p in range(pairs)]
    qss = [_lane_halves(q_ref[0, :, sl]) for sl in sls]
    rows = lax.broadcasted_iota(jnp.int32, (2 * n_new, LANES), 0) & (n_new - 1)
    cols = lax.broadcasted_iota(jnp.int32, (2 * n_new, LANES), 1)
    zero = jnp.zeros((2 * n_new, 1), F32)
    fns = [_pair_fns_new(qs, _pad_rows(kn_ref[0, :, sl], LANES), _pad_rows(vn_ref[0, :, sl], LANES))
           for qs, sl in zip(qss, sls)]
    state = _sb_rows_block([f for f, _ in fns], [g for _, g in fns], _later_matrix(LANES), [zero] * pairs,
                           cols < rows)
    u = _later_matrix(tk)
    for j in reversed(range(kw_ref.shape[3] // tk)):
        fns = [_pair_fns_cache(qs, kw_ref, vw_ref, 2 * p, j * tk, tk, n_new) for p, qs in enumerate(qss)]
        blk = _sb_rows_block([f for f, _ in fns], [g for _, g in fns], u, [c for _, c in state], None)
        state = [(acc + pv, c) for (acc, _), (pv, c) in zip(state, blk)]
    for p, (acc, c) in enumerate(state):
        acc_ref[0, p] = acc
        c_ref[0, p] = jnp.broadcast_to(c, (2 * n_new, LANES))


def _sb_decode_rest_kernel(q_ref, kc_ref, vc_ref, acc_in_ref, c_in_ref, acc_ref, *, n_new, tk):
    qs = _lane_halves(q_ref[0])
    acc = acc_in_ref[0, 0]
    c = c_in_ref[0, 0][:, :1]
    u = _later_matrix(tk)
    for j in reversed(range(kc_ref.shape[3] // tk)):
        def visit(acc, c, j=j):
            score, value = _pair_fns_cache(qs, kc_ref, vc_ref, 0, j * tk, tk, n_new)
            (pv, c), = _sb_rows_block([score], [value], u, [c], None)
            return acc + pv, c

        acc, c = lax.cond(jnp.min(c) < SB_EXIT_LOG2, visit, lambda acc, c: (acc, c), acc, c)
    acc_ref[0, 0] = acc


def _sb_decode_finish_kernel(acc_ref, o_ref, *, n_new, pairs):
    lane = lax.broadcasted_iota(jnp.int32, (n_new, LANES), 1)
    for p in range(pairs):
        acc = acc_ref[0, p]
        o_ref[0, :, p * LANES:(p + 1) * LANES] = jnp.where(lane < LANES // 2, acc[:n_new], acc[n_new:]).astype(o_ref.dtype)


def _sb_decode(q, kc4, vc4, k_new, v_new, tk, window):
    b, n, w = q.shape
    past, heads, hd = kc4.shape[1:]
    window = min(window, past)
    if past % window:
        raise NotImplementedError("the decode window must divide the cache length")
    tk = min(tk, window)
    pairs = w // LANES
    rest = past - window
    kt, vt = jnp.transpose(kc4, (0, 2, 3, 1)), jnp.transpose(vc4, (0, 2, 3, 1))
    state_shape = jax.ShapeDtypeStruct((b, pairs, 2 * n, LANES), F32)
    new = pl.BlockSpec((1, n, w), lambda i: (i, 0, 0))
    win = pl.BlockSpec((1, heads, hd, window), lambda i: (i, 0, 0, past // window - 1))
    state = pl.BlockSpec((1, pairs, 2 * n, LANES), lambda i: (i, 0, 0, 0))
    acc, c = pl.pallas_call(
        functools.partial(_sb_decode_recent_kernel, n_new=n, tk=tk, pairs=pairs),
        grid=(b,),
        in_specs=[new, win, win, new, new],
        out_specs=[state, state],
        out_shape=[state_shape, state_shape],
        compiler_params=_params(("parallel",)),
        name="sb_decode_recent",
    )(q, kt, vt, k_new, v_new)

    if rest:
        def older(acc, c):
            new1 = pl.BlockSpec((1, n, LANES), lambda i, g: (i, 0, g))
            cache = pl.BlockSpec((1, 2, hd, rest), lambda i, g: (i, g, 0, 0))
            state1 = pl.BlockSpec((1, 1, 2 * n, LANES), lambda i, g: (i, g, 0, 0))
            return pl.pallas_call(
                functools.partial(_sb_decode_rest_kernel, n_new=n, tk=math.gcd(tk, rest)),
                grid=(b, pairs),
                in_specs=[new1, cache, cache, state1, state1],
                out_specs=state1,
                out_shape=state_shape,
                compiler_params=_params(("parallel", "parallel")),
                name="sb_decode_rest",
            )(q, kt[..., :rest], vt[..., :rest], acc, c)

        acc = lax.cond(jnp.min(c) < SB_EXIT_LOG2, older, lambda acc, c: acc, acc, c)

    return pl.pallas_call(
        functools.partial(_sb_decode_finish_kernel, n_new=n, pairs=pairs),
        grid=(b,),
        in_specs=[state],
        out_specs=new,
        out_shape=jax.ShapeDtypeStruct((b, n, w), BF16),
        compiler_params=_params(("parallel",)),
        name="sb_decode_finish",
    )(acc)


def _odd_out_kernel(o_ref, g_ref, x_ref, wout_ref, lng_ref, lnb_ref, y_ref):
    a = (o_ref[...].astype(F32) * g_ref[...].astype(F32)).astype(BF16)
    o = jnp.dot(a, wout_ref[...], preferred_element_type=F32)
    y_ref[...] = _layernorm(ALPHA * x_ref[...] + o, lng_ref[...], lnb_ref[...])


def _odd_out(o, gate, x2d, wout, lng, lnb, tm):
    t = x2d.shape[0]
    tm = min(tm, t)
    row = lambda w: pl.BlockSpec((tm, w), lambda i: (i, 0))
    return pl.pallas_call(
        _odd_out_kernel,
        grid=(t // tm,),
        in_specs=[row(SB_W), row(SB_W), row(D_MODEL), _const_spec(wout.shape),
                  _const_spec(lng.shape), _const_spec(lnb.shape)],
        out_specs=row(D_MODEL),
        out_shape=jax.ShapeDtypeStruct((t, D_MODEL), F32),
        compiler_params=_params(("parallel",)),
        name="odd_out",
    )(o, gate, x2d, wout, lng, lnb)


def _even_weights(w_in, q_g, w_uq, kv_g, w_uk):
    pad = jnp.zeros((D_MODEL, LANES - MLA_ROPE_DIM), w_in.dtype)
    o = np.cumsum([0, Q_LORA, KV_LORA, MLA_ROPE_DIM, MLA_W, DIFF_W, DIFF_W, DIFF_W, DIFF_W])
    w_re = jnp.concatenate([w_in[:, o[0]:o[3]], pad, w_in[:, o[3]:]], axis=1).astype(BF16)
    w_uq3 = w_uq.reshape(Q_LORA, MLA_HEADS, MLA_QK_DIM)
    wqn = w_uq3[:, :, :MLA_NOPE_DIM].reshape(Q_LORA, MLA_HEADS * MLA_NOPE_DIM).astype(BF16)
    wqr = jnp.pad(w_uq3[:, :, MLA_NOPE_DIM:], ((0, 0), (0, 0), (0, LANES - MLA_ROPE_DIM)))
    wqr = wqr.reshape(Q_LORA, MLA_HEADS * LANES).astype(BF16)
    eye = jnp.eye(MLA_HEADS, dtype=w_uk.dtype)
    wuk_bd = jnp.einsum('chn,hg->hngc', w_uk, eye).reshape(MLA_HEADS * MLA_NOPE_DIM, MLA_HEADS * KV_LORA).astype(BF16)
    return w_re, wqn, wqr, wuk_bd, q_g.reshape(1, Q_LORA), kv_g.reshape(1, KV_LORA)


def _uv_block_diag(w_uv):
    eye = jnp.eye(MLA_HEADS, dtype=w_uv.dtype)
    return jnp.einsum('chv,hg->hcgv', w_uv, eye).reshape(MLA_HEADS * KV_LORA, MLA_HEADS * MLA_V_DIM).astype(BF16)


TM_PROJ = 512
TQ_MLA = 256
TQ_DIFF = 256
TK_SOFTMAX = 512
TQ_SB = 256
SB_PAIRS = 8
TK_SB_DECODE = 256
SB_DECODE_WINDOW = 256


def kernel(x_prompt, x_sample, cache_mla_ckv, cache_mla_krope, cache_diff_k, cache_diff_v, cache_sb_k, cache_sb_v,
           w_in_even, q_norm_g, w_uq, kv_norm_g, w_uk, w_uv, diff_lambda_q, diff_lambda_k, diff_subln_g,
           w_out_even, ln_even_g, ln_even_b, w_in_odd, w_out_odd, ln_odd_g, ln_odd_b):
    bp, sp, _ = x_prompt.shape
    bs, ss, _ = x_sample.shape
    past = cache_sb_k.shape[2]
    if past % CHUNK or ss > CHUNK or (ss & (ss - 1)) or sp % TQ_SB:
        raise NotImplementedError("decode frames must share one chunk after a chunk-aligned cache")
    tp, ts = bp * sp, bs * ss
    pos_p = jnp.arange(sp, dtype=jnp.int32)
    pos_s = past + jnp.arange(ss, dtype=jnp.int32)
    pos_s_rows = jnp.tile(pos_s, bs)

    def tables(pos):
        mc, msa, msb, _ = _rope_tables(pos, MLA_ROPE_DIM, MLA_ROPE_THETA, LANES)
        dc, dsa, dsb, _ = _rope_tables(pos, DIFF_ROT, ROPE_THETA, DIFF_HD)
        return mc, msa, msb, dc, dsa, dsb

    yp = x_prompt.reshape(tp, D_MODEL)
    ys = x_sample.reshape(ts, D_MODEL)
    even_p, even_s, odd_p, odd_s = [], [], [], []
    for layer in range(DEPTH):
        i = layer // 2
        if layer % 2 == 0:
            lam_init = 0.8 - 0.6 * math.exp(-0.3 * layer)
            wts = _even_weights(w_in_even[i], q_norm_g[i], w_uq[i], kv_norm_g[i], w_uk[i])
            wuv_bd = _uv_block_diag(w_uv[i])
            sg = diff_subln_g[i].reshape(1, 2 * DIFF_HD)
            wout = w_out_even[i].astype(BF16)
            lng, lnb = ln_even_g[i].reshape(1, D_MODEL), ln_even_b[i].reshape(1, D_MODEL)
            lq, lk = diff_lambda_q[i], diff_lambda_k[i]
            qcat, kcat, ckv, krope, gate, dq, dkf, dkb, dvf, dvb = _even_in(yp, wts, tables(pos_p), sp, TM_PROJ)
            olat = _mla_prompt(qcat.reshape(bp, sp, -1), kcat.reshape(bp, sp, -1), TQ_MLA, TK_SOFTMAX)
            odiff = _diff_prompt(dq.reshape(bp, sp, -1), dkb.reshape(bp, sp, -1), dvb.reshape(bp, sp, -1),
                                 lq, lk, lam_init, TQ_DIFF, TK_SOFTMAX)
            yp_new = _even_out(olat.reshape(tp, -1), odiff.reshape(tp, -1), gate, yp, wuv_bd, sg, wout, lng, lnb,
                               lam_init, TM_PROJ)
            even_p.append((ckv.reshape(bp, sp, KV_LORA), krope.reshape(bp, sp, MLA_ROPE_DIM),
                           dkf.reshape(bp, sp, DIFF_HEADS, 2, DIFF_HD), dvf.reshape(bp, sp, DIFF_HEADS, 2 * DIFF_HD)))
            qcat, kcat, ckv, krope, gate, dq, dkf, dkb, dvf, dvb = _even_in(ys, wts, tables(pos_s_rows), ts, TM_PROJ)
            olat = _mla_decode(qcat.reshape(bs, ss, -1), cache_mla_ckv[i], cache_mla_krope[i],
                               kcat.reshape(bs, ss, -1))
            odiff = _diff_decode(dq.reshape(bs, ss, -1), jnp.transpose(cache_diff_k[i], (0, 2, 3, 4, 1)),
                                 cache_diff_v[i],
                                 dkb.reshape(bs, ss, -1),
                                 dvb.reshape(bs, ss, -1), lq, lk, lam_init)
            ys_new = _even_out(olat.reshape(ts, -1), odiff.reshape(ts, -1), gate, ys, wuv_bd, sg, wout, lng, lnb,
                               lam_init, TM_PROJ)
            even_s.append((ckv.reshape(bs, ss, KV_LORA), krope.reshape(bs, ss, MLA_ROPE_DIM),
                           dkf.reshape(bs, ss, DIFF_HEADS, 2, DIFF_HD), dvf.reshape(bs, ss, DIFF_HEADS, 2 * DIFF_HD)))
            yp, ys = yp_new, ys_new
        else:
            w_in = w_in_odd[i].astype(BF16)
            wout = w_out_odd[i].astype(BF16)
            lng, lnb = ln_odd_g[i].reshape(1, D_MODEL), ln_odd_b[i].reshape(1, D_MODEL)
            q, kf, kb, vf, vb, gate = _odd_in(yp, w_in, TM_PROJ)
            o = _sb_prompt(q.reshape(bp, sp, -1), kb.reshape(bp, sp, -1), vb.reshape(bp, sp, -1), TQ_SB, SB_PAIRS)
            yp_new = _odd_out(o.reshape(tp, -1), gate, yp, wout, lng, lnb, TM_PROJ)
            odd_p.append((kf.reshape(bp, sp, SB_HEADS, SB_HD), vf.reshape(bp, sp, SB_HEADS, SB_HD)))
            q, kf, kb, vf, vb, gate = _odd_in(ys, w_in, TM_PROJ)
            o = _sb_decode(q.reshape(bs, ss, -1), cache_sb_k[i], cache_sb_v[i], kb.reshape(bs, ss, -1),
                           vb.reshape(bs, ss, -1), TK_SB_DECODE, SB_DECODE_WINDOW)
            ys_new = _odd_out(o.reshape(ts, -1), gate, ys, wout, lng, lnb, TM_PROJ)
            odd_s.append((kf.reshape(bs, ss, SB_HEADS, SB_HD), vf.reshape(bs, ss, SB_HEADS, SB_HD)))
            yp, ys = yp_new, ys_new
    stack = lambda lst, k: jnp.stack([t[k] for t in lst])
    return (yp.reshape(bp, sp, D_MODEL), ys.reshape(bs, ss, D_MODEL),
            stack(even_p, 0), stack(even_p, 1), stack(even_p, 2), stack(even_p, 3),
            stack(odd_p, 0), stack(odd_p, 1),
            stack(even_s, 0), stack(even_s, 1), stack(even_s, 2), stack(even_s, 3),
            stack(odd_s, 0), stack(odd_s, 1))
```

```python
import functools
import math

import numpy as np
import jax
import jax.numpy as jnp
from jax import lax
from jax.experimental import pallas as pl
from jax.experimental.pallas import tpu as pltpu

F32 = jnp.float32
BF16 = jnp.bfloat16

D_MODEL = 1024
DEPTH = 2
CHUNK = 64
ALPHA = (2 * DEPTH) ** 0.25
EPS = 1e-5
NEG_INF = -1e30
MLA_HEADS = 8
Q_LORA = 256
KV_LORA = 128
MLA_NOPE_DIM = 64
MLA_ROPE_DIM = 32
MLA_QK_DIM = MLA_NOPE_DIM + MLA_ROPE_DIM
MLA_V_DIM = 64
MLA_ROPE_THETA = 10000.0
MLA_W = MLA_HEADS * MLA_V_DIM
DIFF_HEADS = 4
DIFF_HD = 64
DIFF_ROT = DIFF_HD // 4
ROPE_THETA = 500000.0
DIFF_W = DIFF_HEADS * 2 * DIFF_HD
SB_HEADS = 16
SB_HD = 64
SB_W = SB_HEADS * SB_HD

LANES = 128
MLA_QCAT = 2 * LANES
VMEM_LIMIT = 56 * 1024 * 1024

_NT = (((1,), (1,)), ((), ()))
LOG2E = math.log2(math.e)


def _params(sem):
    return pltpu.CompilerParams(dimension_semantics=sem, vmem_limit_bytes=VMEM_LIMIT)


def _const_spec(shape):
    nd = len(shape)
    return pl.BlockSpec(shape, lambda *_: (0,) * nd, pipeline_mode=pl.Buffered(1))


def _rope_tables(pos, rot_dim, theta, period):
    half = rot_dim // 2
    inv = jnp.exp(-math.log(theta) * jnp.arange(half, dtype=F32) * (2.0 / rot_dim))
    ang = pos.astype(F32)[:, None] * inv[None, :]
    cos, sin = jnp.cos(ang), jnp.sin(ang)
    lane = np.arange(LANES) % period
    idx = lane % half
    in_rot = lane < rot_dim
    first = lane < half
    second = in_rot & ~first
    cos_l, sin_l = cos[:, idx], sin[:, idx]
    c = jnp.where(in_rot[None, :], cos_l, 1.0)
    sa = jnp.where(first[None, :], -sin_l, 0.0)
    sb = jnp.where(second[None, :], sin_l, 0.0)
    return c, sa, sb, half


def _apply_rope(x, c, sa, sb, half):
    return x * c + pltpu.roll(x, LANES - half, 1) * sa + pltpu.roll(x, half, 1) * sb


def _silu(x):
    return x / (1.0 + jnp.exp(-x))


def _rms(x, g):
    ms = jnp.mean(x * x, axis=-1, keepdims=True)
    return x * lax.rsqrt(ms + EPS) * g


def _layernorm(r, g, b):
    mu = jnp.mean(r, axis=-1, keepdims=True)
    rc = r - mu
    var = jnp.mean(rc * rc, axis=-1, keepdims=True)
    return rc * lax.rsqrt(var + EPS) * g + b


_C_QLAT = 0
_C_CKV = Q_LORA
_C_KROPE = _C_CKV + KV_LORA
_C_GMLA = _C_KROPE + LANES
_C_DQ = _C_GMLA + MLA_W
_C_DK = _C_DQ + DIFF_W
_C_DV = _C_DK + DIFF_W
_C_GDIFF = _C_DV + DIFF_W
_EVEN_COLS = _C_GDIFF + DIFF_W


def _even_in_kernel(x_ref, w_ref, wqn_ref, wqr_ref, wuk_ref, qg_ref, kvg_ref,
                    mc_ref, msa_ref, msb_ref, dc_ref, dsa_ref, dsb_ref,
                    qcat_ref, kcat_ref, ckv_ref, krope_ref, gate_ref, dq_ref,
                    dkf_ref, dkb_ref, dvf_ref, dvb_ref):
    xb = x_ref[...].astype(BF16)

    def proj(c0, c1):
        return jnp.dot(xb, w_ref[:, c0:c1], preferred_element_type=F32)

    mla_half = MLA_ROPE_DIM // 2
    diff_half = DIFF_ROT // 2
    mc, msa, msb = mc_ref[...], msa_ref[...], msb_ref[...]
    dc, dsa, dsb = dc_ref[...], dsa_ref[...], dsb_ref[...]

    qn = _rms(proj(_C_QLAT, _C_QLAT + Q_LORA), qg_ref[...]).astype(BF16)
    q_nope = jnp.dot(qn, wqn_ref[...], preferred_element_type=F32).astype(BF16)
    q_abs = jnp.dot(q_nope, wuk_ref[...], preferred_element_type=F32)
    q_rp = jnp.dot(qn, wqr_ref[...], preferred_element_type=F32)
    scale = MLA_QK_DIM ** -0.5 * LOG2E
    for h in range(MLA_HEADS):
        rot = _apply_rope(q_rp[:, h * LANES:(h + 1) * LANES], mc, msa, msb, mla_half)
        qcat_ref[:, h * MLA_QCAT:h * MLA_QCAT + LANES] = (q_abs[:, h * LANES:(h + 1) * LANES] * scale).astype(BF16)
        qcat_ref[:, h * MLA_QCAT + LANES:(h + 1) * MLA_QCAT] = (rot * scale).astype(BF16)

    kv = proj(_C_CKV, _C_GMLA)
    ckv = _rms(kv[:, :KV_LORA], kvg_ref[...])
    krope = _apply_rope(kv[:, KV_LORA:], mc, msa, msb, mla_half)
    ckv_ref[...] = ckv
    krope_ref[...] = krope[:, :MLA_ROPE_DIM]
    kcat_ref[:, :LANES] = ckv.astype(BF16)
    kcat_ref[:, LANES:] = krope.astype(BF16)

    gate_ref[:, :MLA_W] = _silu(proj(_C_GMLA, _C_DQ)).astype(BF16)
    gate_ref[:, MLA_W:] = _silu(proj(_C_GDIFF, _EVEN_COLS)).astype(BF16)

    dq = proj(_C_DQ, _C_DK)
    dk = proj(_C_DK, _C_DV)
    dv = proj(_C_DV, _C_GDIFF)
    dscale = DIFF_HD ** -0.5 * LOG2E
    tm = x_ref.shape[0]
    for h in range(DIFF_HEADS):
        sl = slice(h * LANES, (h + 1) * LANES)
        dq_ref[:, sl] = (_apply_rope(dq[:, sl], dc, dsa, dsb, diff_half) * dscale).astype(BF16)
        kr = _apply_rope(dk[:, sl], dc, dsa, dsb, diff_half)
        dkf_ref[pl.ds(2 * h, tm, stride=2 * DIFF_HEADS), :] = kr[:, :DIFF_HD]
        dkf_ref[pl.ds(2 * h + 1, tm, stride=2 * DIFF_HEADS), :] = kr[:, DIFF_HD:]
        dkb_ref[:, sl] = kr.astype(BF16)
        dvf_ref[pl.ds(h, tm, stride=DIFF_HEADS), :] = dv[:, sl]
    dvb_ref[...] = dv.astype(BF16)


def _even_in(x2d, wts, tabs, tab_rows_per_rep, tm):
    t = x2d.shape[0]
    tm = min(tm, t)
    nrep = tab_rows_per_rep // tm
    w_in, wqn, wqr, wuk, qg, kvg = wts
    row = lambda w: pl.BlockSpec((tm, w), lambda i: (i, 0))
    tab = pl.BlockSpec((tm, LANES), lambda i: (i % nrep, 0))
    outs = [
        (1, MLA_HEADS * MLA_QCAT, BF16),
        (1, MLA_QCAT, BF16),
        (1, KV_LORA, F32),
        (1, MLA_ROPE_DIM, F32),
        (1, MLA_W + DIFF_W, BF16),
        (1, DIFF_W, BF16),
        (2 * DIFF_HEADS, DIFF_HD, F32),
        (1, DIFF_W, BF16),
        (DIFF_HEADS, 2 * DIFF_HD, F32),
        (1, DIFF_W, BF16),
    ]
    return pl.pallas_call(
        _even_in_kernel,
        grid=(t // tm,),
        in_specs=[row(D_MODEL), _const_spec(w_in.shape), _const_spec(wqn.shape), _const_spec(wqr.shape),
                  _const_spec(wuk.shape), _const_spec(qg.shape), _const_spec(kvg.shape)] + [tab] * 6,
        out_specs=[pl.BlockSpec((tm * r, w), lambda i: (i, 0)) for r, w, _ in outs],
        out_shape=[jax.ShapeDtypeStruct((t * r, w), d) for r, w, d in outs],
        compiler_params=_params(("parallel",)),
        name="even_in",
    )(x2d, w_in, wqn, wqr, wuk, qg, kvg, *tabs)


def _softmax_step(s, vt, m_ref, l_ref, acc_ref, valid):
    if valid is not None:
        s = jnp.where(valid, s, NEG_INF)
    m_old = m_ref[...]
    m_new = jnp.maximum(m_old, jnp.max(s, axis=0, keepdims=True))
    alpha = jnp.exp2(m_old - m_new)
    p = jnp.exp2(s - m_new)
    l_ref[...] = alpha * l_ref[...] + jnp.sum(p, axis=0, keepdims=True)
    acc_ref[...] = alpha * acc_ref[...] + jnp.dot(vt, p.astype(BF16), preferred_element_type=F32)
    m_ref[...] = m_new


def _flash_scratch(n, tk, m_rows, dv):
    one = [pltpu.VMEM((tk, m_rows), F32), pltpu.VMEM((tk, m_rows), F32), pltpu.VMEM((1, m_rows), F32),
           pltpu.VMEM((1, m_rows), F32), pltpu.VMEM((dv, m_rows), F32)]
    return one * n


def _chunk_causal_flash(problems, q0, tq, tk):
    m_rows = problems[0][0].shape[0]
    for _, _, _, (_, _, m_ref, l_ref, acc_ref) in problems:
        m_ref[...] = jnp.full(m_ref.shape, NEG_INF, F32)
        l_ref[...] = jnp.zeros(l_ref.shape, F32)
        acc_ref[...] = jnp.zeros(acc_ref.shape, F32)
    n_full = q0 // tk

    def scores(prob, j):
        qs, get_k = prob[0], prob[1]
        return lax.dot_general(get_k(pl.multiple_of(j * tk, tk)), qs, _NT, preferred_element_type=F32)

    def consume(j, slot, valid):
        for _, _, get_vt, scr in problems:
            _softmax_step(scr[slot][...], get_vt(j), scr[2], scr[3], scr[4], valid)

    def step(j, slot):
        for prob in problems:
            prob[3][1 - slot][...] = scores(prob, j + 1)
        consume(j, slot, None)

    for prob in problems:
        prob[3][0][...] = scores(prob, 0)

    def body(jj, carry):
        step(2 * jj, 0)
        step(2 * jj + 1, 1)
        return carry

    lax.fori_loop(0, n_full // 2, body, 0)

    diag = (q0 // tq) % (tk // tq)
    shift = CHUNK.bit_length() - 1
    kchunk = lax.broadcasted_iota(jnp.int32, (tq, m_rows), 0) >> shift
    qchunk = (lax.broadcasted_iota(jnp.int32, (tq, m_rows), 1) & (tq - 1)) >> shift
    valid = kchunk <= qchunk

    def consume_last(slot):
        def sub_block(u, mask):
            rows = slice(u * tq, (u + 1) * tq)
            for _, _, get_vt, scr in problems:
                _softmax_step(scr[slot][rows, :], get_vt(n_full)[:, rows], scr[2], scr[3], scr[4], mask)

        for u in range(tk // tq):
            if u + 1 < tk // tq:
                pl.when(u < diag)(functools.partial(sub_block, u, None))
            pl.when(u == diag)(functools.partial(sub_block, u, valid))

    @pl.when(n_full % 2 == 1)
    def _():
        step(n_full - 1, 0)
        consume_last(1)

    @pl.when(n_full % 2 == 0)
    def _():
        consume_last(0)


MLA_GROUPS = 1


def _mla_prompt_kernel(q_ref, k_ref, vt_ref, o_ref, *scratch, tq, tk):
    q0 = pl.program_id(1) * tq
    get_k = lambda k0: k_ref[0, pl.ds(k0, tk), :]
    get_vt = lambda j: vt_ref[0, j]
    per = MLA_HEADS // MLA_GROUPS
    problems = []
    for g in range(MLA_GROUPS):
        qs = jnp.concatenate([q_ref[0, :, h * MLA_QCAT:(h + 1) * MLA_QCAT] for h in range(g * per, (g + 1) * per)],
                             axis=0)
        problems.append((qs, get_k, get_vt, scratch[5 * g:5 * g + 5]))
    _chunk_causal_flash(problems, q0, tq, tk)
    for g, (_, _, _, (_, _, _, l_ref, acc_ref)) in enumerate(problems):
        out = acc_ref[...] / l_ref[...]
        for hh in range(per):
            h = g * per + hh
            o_ref[0, :, h * KV_LORA:(h + 1) * KV_LORA] = out[:, hh * tq:(hh + 1) * tq].T.astype(o_ref.dtype)


def _key_major(x, tk):
    b, s, w = x.shape
    return x.reshape(b, s // tk, tk, w).transpose(0, 1, 3, 2)


def _mla_prompt(qcat, kcat, tq, tk):
    b, s, _ = qcat.shape
    tq, tk = min(tq, s), min(tk, s)
    m_rows = MLA_HEADS * tq
    vt = _key_major(kcat[:, :, :KV_LORA], tk)
    return pl.pallas_call(
        functools.partial(_mla_prompt_kernel, tq=tq, tk=tk),
        grid=(b, s // tq),
        in_specs=[pl.BlockSpec((1, tq, MLA_HEADS * MLA_QCAT), lambda bi, i: (bi, i, 0)),
                  pl.BlockSpec((1, s, MLA_QCAT), lambda bi, i: (bi, 0, 0)),
                  pl.BlockSpec((1, s // tk, KV_LORA, tk), lambda bi, i: (bi, 0, 0, 0))],
        out_specs=pl.BlockSpec((1, tq, MLA_HEADS * KV_LORA), lambda bi, i: (bi, i, 0)),
        out_shape=jax.ShapeDtypeStruct((b, s, MLA_HEADS * KV_LORA), BF16),
        scratch_shapes=_flash_scratch(MLA_GROUPS, tk, m_rows // MLA_GROUPS, KV_LORA),
        compiler_params=_params(("parallel", "parallel")),
        name="mla_prompt",
    )(qcat, kcat, vt)


def _lane_halves(x):
    lane = lax.broadcasted_iota(jnp.int32, x.shape, 1)
    zero = jnp.zeros_like(x)
    return jnp.concatenate([jnp.where(lane < LANES // 2, x, zero), jnp.where(lane < LANES // 2, zero, x)], axis=0)


def _diff_lambda(lq_ref, lk_ref, lam_init):
    e = jnp.exp(jnp.sum(lq_ref[...] * lk_ref[...], axis=-1, keepdims=True))
    return e[0:1] - e[1:2] + lam_init


def _diff_prompt_kernel(q_ref, k_ref, vt_ref, lq_ref, lk_ref, o_ref, *scratch, tq, tk, lam_init):
    q0 = pl.program_id(1) * tq
    lam = _diff_lambda(lq_ref, lk_ref, lam_init)
    problems = []
    for h in range(DIFF_HEADS):
        sl = slice(h * LANES, (h + 1) * LANES)
        problems.append((_lane_halves(q_ref[0, :, sl]),
                         lambda k0, sl=sl: k_ref[0, pl.ds(k0, tk), sl],
                         lambda j, sl=sl: vt_ref[0, j, sl, :],
                         scratch[5 * h:5 * h + 5]))
    _chunk_causal_flash(problems, q0, tq, tk)
    for h, (_, _, _, (_, _, _, l_ref, acc_ref)) in enumerate(problems):
        o = acc_ref[...] / l_ref[...]
        o_ref[0, :, h * LANES:(h + 1) * LANES] = (o[:, :tq] - lam * o[:, tq:]).T


def _diff_prompt(dq, dk, dv, lq, lk, lam_init, tq, tk):
    b, s, _ = dq.shape
    tq, tk = min(tq, s), min(tk, s)
    qspec = pl.BlockSpec((1, tq, DIFF_W), lambda bi, i: (bi, i, 0))
    return pl.pallas_call(
        functools.partial(_diff_prompt_kernel, tq=tq, tk=tk, lam_init=lam_init),
        grid=(b, s // tq),
        in_specs=[qspec, pl.BlockSpec((1, s, DIFF_W), lambda bi, i: (bi, 0, 0)),
                  pl.BlockSpec((1, s // tk, DIFF_W, tk), lambda bi, i: (bi, 0, 0, 0)),
                  _const_spec(lq.shape), _const_spec(lk.shape)],
        out_specs=qspec,
        out_shape=jax.ShapeDtypeStruct((b, s, DIFF_W), F32),
        scratch_shapes=_flash_scratch(DIFF_HEADS, tk, 2 * tq, LANES),
        compiler_params=_params(("parallel", "parallel")),
        name="diff_prompt",
    )(dq, dk, _key_major(dv, tk), lq, lk)


def _pad_rows(x, rows):
    return jnp.concatenate([x, jnp.zeros((rows - x.shape[0], x.shape[1]), x.dtype)], axis=0)


def _decode_softmax(s_c, s_n, v_cache, v_new, n_new):
    col = lax.broadcasted_iota(jnp.int32, s_n.shape, 1)
    s_n = jnp.where(col < n_new, s_n, NEG_INF)
    m = jnp.maximum(jnp.max(s_c, axis=-1, keepdims=True), jnp.max(s_n, axis=-1, keepdims=True))
    p_c = jnp.exp2(s_c - m)
    p_n = jnp.exp2(s_n - m)
    l = jnp.sum(p_c, axis=-1, keepdims=True) + jnp.sum(p_n, axis=-1, keepdims=True)
    acc = (jnp.dot(p_c.astype(BF16), v_cache, preferred_element_type=F32)
           + jnp.dot(p_n.astype(BF16), v_new, preferred_element_type=F32))
    return acc / l


def _mla_decode_kernel(q_ref, ckv_ref, kr_ref, kn_ref, o_ref, *, n_new):
    qs = jnp.concatenate([q_ref[0, :, h * MLA_QCAT:(h + 1) * MLA_QCAT] for h in range(MLA_HEADS)], axis=0)
    ckv = ckv_ref[0].astype(BF16)
    kr = kr_ref[0].astype(BF16)
    k_new = _pad_rows(kn_ref[0], LANES)
    s_c = (lax.dot_general(qs[:, :KV_LORA], ckv, _NT, preferred_element_type=F32)
           + lax.dot_general(qs[:, KV_LORA:KV_LORA + MLA_ROPE_DIM], kr, _NT, preferred_element_type=F32))
    s_n = lax.dot_general(qs, k_new, _NT, preferred_element_type=F32)
    out = _decode_softmax(s_c, s_n, ckv, k_new[:, :KV_LORA], n_new)
    for h in range(MLA_HEADS):
        o_ref[0, :, h * KV_LORA:(h + 1) * KV_LORA] = out[h * n_new:(h + 1) * n_new].astype(o_ref.dtype)


def _mla_decode(qcat, ckv_c, kr_c, kcat_new):
    b, n, _ = qcat.shape
    past = ckv_c.shape[1]
    return pl.pallas_call(
        functools.partial(_mla_decode_kernel, n_new=n),
        grid=(b,),
        in_specs=[pl.BlockSpec((1, n, MLA_HEADS * MLA_QCAT), lambda i: (i, 0, 0)),
                  pl.BlockSpec((1, past, KV_LORA), lambda i: (i, 0, 0)),
                  pl.BlockSpec((1, past, MLA_ROPE_DIM), lambda i: (i, 0, 0)),
                  pl.BlockSpec((1, n, MLA_QCAT), lambda i: (i, 0, 0))],
        out_specs=pl.BlockSpec((1, n, MLA_HEADS * KV_LORA), lambda i: (i, 0, 0)),
        out_shape=jax.ShapeDtypeStruct((b, n, MLA_HEADS * KV_LORA), BF16),
        compiler_params=_params(("parallel",)),
        name="mla_decode",
    )(qcat, ckv_c, kr_c, kcat_new)


def _diff_decode_kernel(q_ref, kc_ref, vc_ref, kn_ref, vn_ref, lq_ref, lk_ref, o_ref, *, n_new, past, lam_init):
    lam = _diff_lambda(lq_ref, lk_ref, lam_init)
    for h in range(DIFF_HEADS):
        sl = slice(h * LANES, (h + 1) * LANES)
        q = q_ref[0, :, sl]
        qs = _lane_halves(q)
        v = vc_ref[0, pl.ds(h, past, stride=DIFF_HEADS), :].astype(BF16)
        s_c = jnp.concatenate(
            [jnp.dot(q[:, :DIFF_HD], kc_ref[0, h, 0].astype(BF16), preferred_element_type=F32),
             jnp.dot(q[:, DIFF_HD:], kc_ref[0, h, 1].astype(BF16), preferred_element_type=F32)], axis=0)
        s_n = lax.dot_general(qs, _pad_rows(kn_ref[0, :, sl], LANES), _NT, preferred_element_type=F32)
        o = _decode_softmax(s_c, s_n, v, _pad_rows(vn_ref[0, :, sl], LANES), n_new)
        o_ref[0, :, sl] = o[:n_new] - lam * o[n_new:]


def _diff_decode(dq, kc, vc, dk_new, dv_new, lq, lk, lam_init):
    b, n, _ = dq.shape
    past = kc.shape[-1]
    new = pl.BlockSpec((1, n, DIFF_W), lambda i: (i, 0, 0))
    return pl.pallas_call(
        functools.partial(_diff_decode_kernel, n_new=n, past=past, lam_init=lam_init),
        grid=(b,),
        in_specs=[new, pl.BlockSpec((1, DIFF_HEADS, 2, DIFF_HD, past), lambda i: (i, 0, 0, 0, 0)),
                  pl.BlockSpec((1, past * DIFF_HEADS, 2 * DIFF_HD), lambda i: (i, 0, 0)),
                  new, new, _const_spec(lq.shape), _const_spec(lk.shape)],
        out_specs=new,
        out_shape=jax.ShapeDtypeStruct((b, n, DIFF_W), F32),
        compiler_params=_params(("parallel",)),
        name="diff_decode",
    )(dq, kc, vc.reshape(b, past * DIFF_HEADS, 2 * DIFF_HD), dk_new, dv_new, lq, lk)


def _even_out_kernel(olat_ref, odiff_ref, gate_ref, x_ref, wuv_ref, sg_ref, wout_ref, lng_ref, lnb_ref, y_ref,
                     *, lam_init):
    gate = gate_ref[...]
    o_mla = jnp.dot(olat_ref[...], wuv_ref[...], preferred_element_type=F32)
    parts = [(o_mla * gate[:, :MLA_W].astype(F32)).astype(BF16)]
    od = odiff_ref[...]
    for h in range(DIFF_HEADS):
        sl = slice(h * LANES, (h + 1) * LANES)
        n = _rms(od[:, sl], sg_ref[...]) * (1.0 - lam_init)
        parts.append((n * gate[:, MLA_W + h * LANES:MLA_W + (h + 1) * LANES].astype(F32)).astype(BF16))
    o = jnp.dot(jnp.concatenate(parts, axis=1), wout_ref[...], preferred_element_type=F32)
    y_ref[...] = _layernorm(ALPHA * x_ref[...] + o, lng_ref[...], lnb_ref[...])


def _even_out(olat, odiff, gate, x2d, wuv_bd, sg, wout, lng, lnb, lam_init, tm):
    t = x2d.shape[0]
    tm = min(tm, t)
    row = lambda w: pl.BlockSpec((tm, w), lambda i: (i, 0))
    return pl.pallas_call(
        functools.partial(_even_out_kernel, lam_init=lam_init),
        grid=(t // tm,),
        in_specs=[row(MLA_HEADS * KV_LORA), row(DIFF_W), row(MLA_W + DIFF_W), row(D_MODEL),
                  _const_spec(wuv_bd.shape), _const_spec(sg.shape), _const_spec(wout.shape),
                  _const_spec(lng.shape), _const_spec(lnb.shape)],
        out_specs=row(D_MODEL),
        out_shape=jax.ShapeDtypeStruct((t, D_MODEL), F32),
        compiler_params=_params(("parallel",)),
        name="even_out",
    )(olat, odiff, gate, x2d, wuv_bd, sg, wout, lng, lnb)


def _odd_in_kernel(x_ref, w_ref, q_ref, kf_ref, kb_ref, vf_ref, vb_ref, g_ref):
    xb = x_ref[...].astype(BF16)

    def proj(i):
        return jnp.dot(xb, w_ref[:, i * SB_W:(i + 1) * SB_W], preferred_element_type=F32)

    tm = x_ref.shape[0]

    def store_heads(ref, val):
        for h in range(SB_HEADS):
            ref[pl.ds(h, tm, stride=SB_HEADS), :] = val[:, h * SB_HD:(h + 1) * SB_HD]

    q_ref[...] = (proj(0) * (SB_HD ** -0.5 * LOG2E)).astype(BF16)
    k = proj(1)
    store_heads(kf_ref, k)
    kb_ref[...] = k.astype(BF16)
    v = proj(2)
    store_heads(vf_ref, v)
    vb_ref[...] = v.astype(BF16)
    g_ref[...] = _silu(proj(3)).astype(BF16)


def _odd_in(x2d, w_in, tm):
    t = x2d.shape[0]
    tm = min(tm, t)
    row = pl.BlockSpec((tm, SB_W), lambda i: (i, 0))
    heads = pl.BlockSpec((tm * SB_HEADS, SB_HD), lambda i: (i, 0))
    wide = lambda d: jax.ShapeDtypeStruct((t, SB_W), d)
    tall = jax.ShapeDtypeStruct((t * SB_HEADS, SB_HD), F32)
    return pl.pallas_call(
        _odd_in_kernel,
        grid=(t // tm,),
        in_specs=[pl.BlockSpec((tm, D_MODEL), lambda i: (i, 0)), _const_spec(w_in.shape)],
        out_specs=[row, heads, row, heads, row, row],
        out_shape=[wide(BF16), tall, wide(BF16), tall, wide(BF16), wide(BF16)],
        compiler_params=_params(("parallel",)),
        name="odd_in",
    )(x2d, w_in)


SB_EXIT_LOG2 = 160.0


def _log2(x):
    return jnp.log2(x)


def _later_matrix(n):
    return (lax.broadcasted_iota(jnp.int32, (n, n), 0) > lax.broadcasted_iota(jnp.int32, (n, n), 1)).astype(BF16)


def _softplus2(z):
    return jnp.maximum(z, 0.0) + _log2(1.0 + jnp.exp2(-jnp.abs(z)))


def _sb_prompt_kernel(q_ref, k_ref, vt_ref, o_ref, c_ref, acc_ref, x_ref, sp0_ref, *, tq, pairs):
    i = pl.program_id(2)
    ut = (lax.broadcasted_iota(jnp.int32, (tq, tq), 1) > lax.broadcasted_iota(jnp.int32, (tq, tq), 0)).astype(BF16)
    keys = lax.broadcasted_iota(jnp.int32, (tq, 2 * tq), 0)
    queries = lax.broadcasted_iota(jnp.int32, (tq, 2 * tq), 1) & (tq - 1)
    diag_valid = keys < queries
    half = LANES // 2
    sls = [slice(p * LANES, (p + 1) * LANES) for p in range(pairs)]
    qss = [_lane_halves(q_ref[0, :, sl]) for sl in sls]

    def all_pairs(j, valid, first):
        k0 = pl.multiple_of(j * tq, tq)
        zs = [lax.dot_general(k_ref[0, pl.ds(k0, tq), sl], qss[p], _NT, preferred_element_type=F32)
              for p, sl in enumerate(sls)]
        laters = []
        for p, z in enumerate(zs):
            sp = _softplus2(z)
            x_ref[p] = z - sp
            if valid is not None:
                sp = jnp.where(valid, sp, 0.0)
            spb = sp.astype(BF16)
            sp0_ref[p] = spb[:1, :].astype(F32)
            laters.append(jnp.dot(ut, spb, preferred_element_type=F32))
        cmin = None
        for p, sl in enumerate(sls):
            a = jnp.exp2(x_ref[p] - laters[p])
            if valid is not None:
                a = jnp.where(valid, a, 0.0)
            pv = jnp.dot(vt_ref[0, j, sl, :], a.astype(BF16), preferred_element_type=F32)
            tot = laters[p][:1, :] + sp0_ref[p]
            if first:
                acc_ref[p] = pv
                c = tot
            else:
                c0 = c_ref[p]
                acc_ref[p] += pv * jnp.exp2(-c0)
                c = c0 + tot
            c_ref[p] = c
            cmin = c if cmin is None else jnp.minimum(cmin, c)
        return jnp.min(cmin)

    cmin = all_pairs(i, diag_valid, True)
    lax.while_loop(lambda jc: (jc[0] >= 0) & (jc[1] < SB_EXIT_LOG2),
                   lambda jc: (jc[0] - 1, all_pairs(jc[0], None, False)),
                   (i - 1, cmin))

    for p, sl in enumerate(sls):
        acc = acc_ref[p]
        o_t = jnp.concatenate([acc[:half, :tq], acc[half:, tq:]], axis=0)
        o_ref[0, :, sl] = o_t.T.astype(o_ref.dtype)


def _sb_prompt(q, k, v, tq, pairs):
    b, s, w = q.shape
    tq = min(tq, s)
    gw = pairs * LANES
    qspec = pl.BlockSpec((1, tq, gw), lambda bi, g, i: (bi, i, g))
    return pl.pallas_call(
        functools.partial(_sb_prompt_kernel, tq=tq, pairs=pairs),
        grid=(b, w // gw, s // tq),
        in_specs=[qspec, pl.BlockSpec((1, s, gw), lambda bi, g, i: (bi, 0, g)),
                  pl.BlockSpec((1, s // tq, gw, tq), lambda bi, g, i: (bi, 0, g, 0))],
        out_specs=qspec,
        out_shape=jax.ShapeDtypeStruct((b, s, w), BF16),
        scratch_shapes=[pltpu.VMEM((pairs, 1, 2 * tq), F32), pltpu.VMEM((pairs, LANES, 2 * tq), F32),
                        pltpu.VMEM((pairs, tq, 2 * tq), F32), pltpu.VMEM((pairs, 1, 2 * tq), F32)],
        compiler_params=_params(("parallel", "parallel", "parallel")),
        name="sb_prompt",
    )(q, k, _key_major(v, tq))


def _sb_rows_block(score_fns, value_fns, u, cs, valid):
    zs = [f() for f in score_fns]
    xs, sp0s, laters = [], [], []
    for z in zs:
        sp = _softplus2(z)
        xs.append(z - sp)
        if valid is not None:
            sp = jnp.where(valid, sp, 0.0)
        spb = sp.astype(BF16)
        sp0s.append(spb[:, :1].astype(F32))
        laters.append(jnp.dot(spb, u, preferred_element_type=F32))
    out = []
    for x, sp0, later, c, value in zip(xs, sp0s, laters, cs, value_fns):
        a = jnp.exp2(x - later - c)
        if valid is not None:
            a = jnp.where(valid, a, 0.0)
        out.append((value(a.astype(BF16)), c + later[:, :1] + sp0))
    return out


def _pair_fns_new(qs, kb, vb):
    return (lambda: lax.dot_general(qs, kb, _NT, preferred_element_type=F32),
            lambda a: jnp.dot(a, vb, preferred_element_type=F32))


def _pair_fns_cache(qs, kt_ref, vt_ref, h0, t0, tk, n_new):
    hd = LANES // 2
    q0, q1 = qs[:n_new, :hd], qs[n_new:, hd:]
    pad = jnp.zeros((n_new, hd), F32)

    def score():
        return jnp.concatenate(
            [jnp.dot(q0, kt_ref[0, h0, :, t0:t0 + tk].astype(BF16), preferred_element_type=F32),
             jnp.dot(q1, kt_ref[0, h0 + 1, :, t0:t0 + tk].astype(BF16), preferred_element_type=F32)], axis=0)

    def value(a):
        pv0 = lax.dot_general(a[:n_new], vt_ref[0, h0, :, t0:t0 + tk].astype(BF16), _NT, preferred_element_type=F32)
        pv1 = lax.dot_general(a[n_new:], vt_ref[0, h0 + 1, :, t0:t0 + tk].astype(BF16), _NT,
                              preferred_element_type=F32)
        return jnp.concatenate([jnp.concatenate([pv0, pad], axis=1), jnp.concatenate([pad, pv1], axis=1)], axis=0)

    return score, value


def _sb_decode_recent_kernel(q_ref, kw_ref, vw_ref, kn_ref, vn_ref, acc_ref, c_ref, *, n_new, tk, pairs):
    sls = [slice(p * LANES, (p + 1) * LANES) for p in range(pairs)]
    qss = [_lane_halves(q_ref[0, :, sl]) for sl in sls]
    rows = lax.broadcasted_iota(jnp.int32, (2 * n_new, LANES), 0) & (n_new - 1)
    cols = lax.broadcasted_iota(jnp.int32, (2 * n_new, LANES), 1)
    zero = jnp.zeros((2 * n_new, 1), F32)
    fns = [_pair_fns_new(qs, _pad_rows(kn_ref[0, :, sl], LANES), _pad_rows(vn_ref[0, :, sl], LANES))
           for qs, sl in zip(qss, sls)]
    state = _sb_rows_block([f for f, _ in fns], [g for _, g in fns], _later_matrix(LANES), [zero] * pairs,
                           cols < rows)
    u = _later_matrix(tk)
    for j in reversed(range(kw_ref.shape[3] // tk)):
        fns = [_pair_fns_cache(qs, kw_ref, vw_ref, 2 * p, j * tk, tk, n_new) for p, qs in enumerate(qss)]
        blk = _sb_rows_block([f for f, _ in fns], [g for _, g in fns], u, [c for _, c in state], None)
        state = [(acc + pv, c) for (acc, _), (pv, c) in zip(state, blk)]
    for p, (acc, c) in enumerate(state):
        acc_ref[0, p] = acc
        c_ref[0, p] = jnp.broadcast_to(c, (2 * n_new, LANES))


def _sb_decode_rest_kernel(q_ref, kc_ref, vc_ref, acc_in_ref, c_in_ref, acc_ref, *, n_new, tk):
    qs = _lane_halves(q_ref[0])
    acc = acc_in_ref[0, 0]
    c = c_in_ref[0, 0][:, :1]
    u = _later_matrix(tk)
    for j in reversed(range(kc_ref.shape[3] // tk)):
        def visit(acc, c, j=j):
            score, value = _pair_fns_cache(qs, kc_ref, vc_ref, 0, j * tk, tk, n_new)
            (pv, c), = _sb_rows_block([score], [value], u, [c], None)
            return acc + pv, c

        acc, c = lax.cond(jnp.min(c) < SB_EXIT_LOG2, visit, lambda acc, c: (acc, c), acc, c)
    acc_ref[0, 0] = acc


def _sb_decode_finish_kernel(acc_ref, o_ref, *, n_new, pairs):
    lane = lax.broadcasted_iota(jnp.int32, (n_new, LANES), 1)
    for p in range(pairs):
        acc = acc_ref[0, p]
        o_ref[0, :, p * LANES:(p + 1) * LANES] = jnp.where(lane < LANES // 2, acc[:n_new], acc[n_new:]).astype(o_ref.dtype)


def _sb_decode(q, kc4, vc4, k_new, v_new, tk, window):
    b, n, w = q.shape
    past, heads, hd = kc4.shape[1:]
    window = min(window, past)
    if past % window:
        raise NotImplementedError("the decode window must divide the cache length")
    tk = min(tk, window)
    pairs = w // LANES
    rest = past - window
    kt, vt = jnp.transpose(kc4, (0, 2, 3, 1)), jnp.transpose(vc4, (0, 2, 3, 1))
    state_shape = jax.ShapeDtypeStruct((b, pairs, 2 * n, LANES), F32)
    new = pl.BlockSpec((1, n, w), lambda i: (i, 0, 0))
    win = pl.BlockSpec((1, heads, hd, window), lambda i: (i, 0, 0, past // window - 1))
    state = pl.BlockSpec((1, pairs, 2 * n, LANES), lambda i: (i, 0, 0, 0))
    acc, c = pl.pallas_call(
        functools.partial(_sb_decode_recent_kernel, n_new=n, tk=tk, pairs=pairs),
        grid=(b,),
        in_specs=[new, win, win, new, new],
        out_specs=[state, state],
        out_shape=[state_shape, state_shape],
        compiler_params=_params(("parallel",)),
        name="sb_decode_recent",
    )(q, kt, vt, k_new, v_new)

    if rest:
        def older(acc, c):
            new1 = pl.BlockSpec((1, n, LANES), lambda i, g: (i, 0, g))
            cache = pl.BlockSpec((1, 2, hd, rest), lambda i, g: (i, g, 0, 0))
            state1 = pl.BlockSpec((1, 1, 2 * n, LANES), lambda i, g: (i, g, 0, 0))
            return pl.pallas_call(
                functools.partial(_sb_decode_rest_kernel, n_new=n, tk=math.gcd(tk, rest)),
                grid=(b, pairs),
                in_specs=[new1, cache, cache, state1, state1],
                out_specs=state1,
                out_shape=state_shape,
                compiler_params=_params(("parallel", "parallel")),
                name="sb_decode_rest",
            )(q, kt[..., :rest], vt[..., :rest], acc, c)

        acc = lax.cond(jnp.min(c) < SB_EXIT_LOG2, older, lambda acc, c: acc, acc, c)

    return pl.pallas_call(
        functools.partial(_sb_decode_finish_kernel, n_new=n, pairs=pairs),
        grid=(b,),
        in_specs=[state],
        out_specs=new,
        out_shape=jax.ShapeDtypeStruct((b, n, w), BF16),
        compiler_params=_params(("parallel",)),
        name="sb_decode_finish",
    )(acc)


def _odd_out_kernel(o_ref, g_ref, x_ref, wout_ref, lng_ref, lnb_ref, y_ref):
    a = (o_ref[...].astype(F32) * g_ref[...].astype(F32)).astype(BF16)
    o = jnp.dot(a, wout_ref[...], preferred_element_type=F32)
    y_ref[...] = _layernorm(ALPHA * x_ref[...] + o, lng_ref[...], lnb_ref[...])


def _odd_out(o, gate, x2d, wout, lng, lnb, tm):
    t = x2d.shape[0]
    tm = min(tm, t)
    row = lambda w: pl.BlockSpec((tm, w), lambda i: (i, 0))
    return pl.pallas_call(
        _odd_out_kernel,
        grid=(t // tm,),
        in_specs=[row(SB_W), row(SB_W), row(D_MODEL), _const_spec(wout.shape),
                  _const_spec(lng.shape), _const_spec(lnb.shape)],
        out_specs=row(D_MODEL),
        out_shape=jax.ShapeDtypeStruct((t, D_MODEL), F32),
        compiler_params=_params(("parallel",)),
        name="odd_out",
    )(o, gate, x2d, wout, lng, lnb)


def _even_weights(w_in, q_g, w_uq, kv_g, w_uk):
    pad = jnp.zeros((D_MODEL, LANES - MLA_ROPE_DIM), w_in.dtype)
    o = np.cumsum([0, Q_LORA, KV_LORA, MLA_ROPE_DIM, MLA_W, DIFF_W, DIFF_W, DIFF_W, DIFF_W])
    w_re = jnp.concatenate([w_in[:, o[0]:o[3]], pad, w_in[:, o[3]:]], axis=1).astype(BF16)
    w_uq3 = w_uq.reshape(Q_LORA, MLA_HEADS, MLA_QK_DIM)
    wqn = w_uq3[:, :, :MLA_NOPE_DIM].reshape(Q_LORA, MLA_HEADS * MLA_NOPE_DIM).astype(BF16)
    wqr = jnp.pad(w_uq3[:, :, MLA_NOPE_DIM:], ((0, 0), (0, 0), (0, LANES - MLA_ROPE_DIM)))
    wqr = wqr.reshape(Q_LORA, MLA_HEADS * LANES).astype(BF16)
    eye = jnp.eye(MLA_HEADS, dtype=w_uk.dtype)
    wuk_bd = jnp.einsum('chn,hg->hngc', w_uk, eye).reshape(MLA_HEADS * MLA_NOPE_DIM, MLA_HEADS * KV_LORA).astype(BF16)
    return w_re, wqn, wqr, wuk_bd, q_g.reshape(1, Q_LORA), kv_g.reshape(1, KV_LORA)


def _uv_block_diag(w_uv):
    eye = jnp.eye(MLA_HEADS, dtype=w_uv.dtype)
    return jnp.einsum('chv,hg->hcgv', w_uv, eye).reshape(MLA_HEADS * KV_LORA, MLA_HEADS * MLA_V_DIM).astype(BF16)


TM_PROJ = 512
TQ_MLA = 256
TQ_DIFF = 256
TK_SOFTMAX = 512
TQ_SB = 256
SB_PAIRS = 8
TK_SB_DECODE = 256
SB_DECODE_WINDOW = 256


def kernel(x_prompt, x_sample, cache_mla_ckv, cache_mla_krope, cache_diff_k, cache_diff_v, cache_sb_k, cache_sb_v,
           w_in_even, q_norm_g, w_uq, kv_norm_g, w_uk, w_uv, diff_lambda_q, diff_lambda_k, diff_subln_g,
           w_out_even, ln_even_g, ln_even_b, w_in_odd, w_out_odd, ln_odd_g, ln_odd_b):
    bp, sp, _ = x_prompt.shape
    bs, ss, _ = x_sample.shape
    past = cache_sb_k.shape[2]
    if past % CHUNK or ss > CHUNK or (ss & (ss - 1)) or sp % TQ_SB:
        raise NotImplementedError("decode frames must share one chunk after a chunk-aligned cache")
    tp, ts = bp * sp, bs * ss
    pos_p = jnp.arange(sp, dtype=jnp.int32)
    pos_s = past + jnp.arange(ss, dtype=jnp.int32)
    pos_s_rows = jnp.tile(pos_s, bs)

    def tables(pos):
        mc, msa, msb, _ = _rope_tables(pos, MLA_ROPE_DIM, MLA_ROPE_THETA, LANES)
        dc, dsa, dsb, _ = _rope_tables(pos, DIFF_ROT, ROPE_THETA, DIFF_HD)
        return mc, msa, msb, dc, dsa, dsb

    yp = x_prompt.reshape(tp, D_MODEL)
    ys = x_sample.reshape(ts, D_MODEL)
    even_p, even_s, odd_p, odd_s = [], [], [], []
    for layer in range(DEPTH):
        i = layer // 2
        if layer % 2 == 0:
            lam_init = 0.8 - 0.6 * math.exp(-0.3 * layer)
            wts = _even_weights(w_in_even[i], q_norm_g[i], w_uq[i], kv_norm_g[i], w_uk[i])
            wuv_bd = _uv_block_diag(w_uv[i])
            sg = diff_subln_g[i].reshape(1, 2 * DIFF_HD)
            wout = w_out_even[i].astype(BF16)
            lng, lnb = ln_even_g[i].reshape(1, D_MODEL), ln_even_b[i].reshape(1, D_MODEL)
            lq, lk = diff_lambda_q[i], diff_lambda_k[i]
            qcat, kcat, ckv, krope, gate, dq, dkf, dkb, dvf, dvb = _even_in(yp, wts, tables(pos_p), sp, TM_PROJ)
            olat = _mla_prompt(qcat.reshape(bp, sp, -1), kcat.reshape(bp, sp, -1), TQ_MLA, TK_SOFTMAX)
            odiff = _diff_prompt(dq.reshape(bp, sp, -1), dkb.reshape(bp, sp, -1), dvb.reshape(bp, sp, -1),
                                 lq, lk, lam_init, TQ_DIFF, TK_SOFTMAX)
            yp_new = _even_out(olat.reshape(tp, -1), odiff.reshape(tp, -1), gate, yp, wuv_bd, sg, wout, lng, lnb,
                               lam_init, TM_PROJ)
            even_p.append((ckv.reshape(bp, sp, KV_LORA), krope.reshape(bp, sp, MLA_ROPE_DIM),
                           dkf.reshape(bp, sp, DIFF_HEADS, 2, DIFF_HD), dvf.reshape(bp, sp, DIFF_HEADS, 2 * DIFF_HD)))
            qcat, kcat, ckv, krope, gate, dq, dkf, dkb, dvf, dvb = _even_in(ys, wts, tables(pos_s_rows), ts, TM_PROJ)
            olat = _mla_decode(qcat.reshape(bs, ss, -1), cache_mla_ckv[i], cache_mla_krope[i],
                               kcat.reshape(bs, ss, -1))
            odiff = _diff_decode(dq.reshape(bs, ss, -1), jnp.transpose(cache_diff_k[i], (0, 2, 3, 4, 1)),
                                 cache_diff_v[i],
                                 dkb.reshape(bs, ss, -1),
                                 dvb.reshape(bs, ss, -1), lq, lk, lam_init)
            ys_new = _even_out(olat.reshape(ts, -1), odiff.reshape(ts, -1), gate, ys, wuv_bd, sg, wout, lng, lnb,
                               lam_init, TM_PROJ)
            even_s.append((ckv.reshape(bs, ss, KV_LORA), krope.reshape(bs, ss, MLA_ROPE_DIM),
                           dkf.reshape(bs, ss, DIFF_HEADS, 2, DIFF_HD), dvf.reshape(bs, ss, DIFF_HEADS, 2 * DIFF_HD)))
            yp, ys = yp_new, ys_new
        else:
            w_in = w_in_odd[i].astype(BF16)
            wout = w_out_odd[i].astype(BF16)
            lng, lnb = ln_odd_g[i].reshape(1, D_MODEL), ln_odd_b[i].reshape(1, D_MODEL)
            q, kf, kb, vf, vb, gate = _odd_in(yp, w_in, TM_PROJ)
            o = _sb_prompt(q.reshape(bp, sp, -1), kb.reshape(bp, sp, -1), vb.reshape(bp, sp, -1), TQ_SB, SB_PAIRS)
            yp_new = _odd_out(o.reshape(tp, -1), gate, yp, wout, lng, lnb, TM_PROJ)
            odd_p.append((kf.reshape(bp, sp, SB_HEADS, SB_HD), vf.reshape(bp, sp, SB_HEADS, SB_HD)))
            q, kf, kb, vf, vb, gate = _odd_in(ys, w_in, TM_PROJ)
            o = _sb_decode(q.reshape(bs, ss, -1), cache_sb_k[i], cache_sb_v[i], kb.reshape(bs, ss, -1),
                           vb.reshape(bs, ss, -1), TK_SB_DECODE, SB_DECODE_WINDOW)
            ys_new = _odd_out(o.reshape(ts, -1), gate, ys, wout, lng, lnb, TM_PROJ)
            odd_s.append((kf.reshape(bs, ss, SB_HEADS, SB_HD), vf.reshape(bs, ss, SB_HEADS, SB_HD)))
            yp, ys = yp_new, ys_new
    stack = lambda lst, k: jnp.stack([t[k] for t in lst])
    return (yp.reshape(bp, sp, D_MODEL), ys.reshape(bs, ss, D_MODEL),
            stack(even_p, 0), stack(even_p, 1), stack(even_p, 2), stack(even_p, 3),
            stack(odd_p, 0), stack(odd_p, 1),
            stack(even_s, 0), stack(even_s, 1), stack(even_s, 2), stack(even_s, 3),
            stack(odd_s, 0), stack(odd_s, 1))
```

```python
import functools
import math

import numpy as np
import jax
import jax.numpy as jnp
from jax import lax
from jax.experimental import pallas as pl
from jax.experimental.pallas import tpu as pltpu

F32 = jnp.float32
BF16 = jnp.bfloat16

D_MODEL = 1024
DEPTH = 2
CHUNK = 64
ALPHA = (2 * DEPTH) ** 0.25
EPS = 1e-5
NEG_INF = -1e30
MLA_HEADS = 8
Q_LORA = 256
KV_LORA = 128
MLA_NOPE_DIM = 64
MLA_ROPE_DIM = 32
MLA_QK_DIM = MLA_NOPE_DIM + MLA_ROPE_DIM
MLA_V_DIM = 64
MLA_ROPE_THETA = 10000.0
MLA_W = MLA_HEADS * MLA_V_DIM
DIFF_HEADS = 4
DIFF_HD = 64
DIFF_ROT = DIFF_HD // 4
ROPE_THETA = 500000.0
DIFF_W = DIFF_HEADS * 2 * DIFF_HD
SB_HEADS = 16
SB_HD = 64
SB_W = SB_HEADS * SB_HD

LANES = 128
MLA_QCAT = 2 * LANES
VMEM_LIMIT = 56 * 1024 * 1024

_NT = (((1,), (1,)), ((), ()))
LOG2E = math.log2(math.e)


def _params(sem):
    return pltpu.CompilerParams(dimension_semantics=sem, vmem_limit_bytes=VMEM_LIMIT)


def _const_spec(shape):
    nd = len(shape)
    return pl.BlockSpec(shape, lambda *_: (0,) * nd, pipeline_mode=pl.Buffered(1))


def _rope_tables(pos, rot_dim, theta, period):
    half = rot_dim // 2
    inv = jnp.exp(-math.log(theta) * jnp.arange(half, dtype=F32) * (2.0 / rot_dim))
    ang = pos.astype(F32)[:, None] * inv[None, :]
    cos, sin = jnp.cos(ang), jnp.sin(ang)
    lane = np.arange(LANES) % period
    idx = lane % half
    in_rot = lane < rot_dim
    first = lane < half
    second = in_rot & ~first
    cos_l, sin_l = cos[:, idx], sin[:, idx]
    c = jnp.where(in_rot[None, :], cos_l, 1.0)
    sa = jnp.where(first[None, :], -sin_l, 0.0)
    sb = jnp.where(second[None, :], sin_l, 0.0)
    return c, sa, sb, half


def _apply_rope(x, c, sa, sb, half):
    return x * c + pltpu.roll(x, LANES - half, 1) * sa + pltpu.roll(x, half, 1) * sb


def _silu(x):
    return x / (1.0 + jnp.exp(-x))


def _rms(x, g):
    ms = jnp.mean(x * x, axis=-1, keepdims=True)
    return x * lax.rsqrt(ms + EPS) * g


def _layernorm(r, g, b):
    mu = jnp.mean(r, axis=-1, keepdims=True)
    rc = r - mu
    var = jnp.mean(rc * rc, axis=-1, keepdims=True)
    return rc * lax.rsqrt(var + EPS) * g + b


_C_QLAT = 0
_C_CKV = Q_LORA
_C_KROPE = _C_CKV + KV_LORA
_C_GMLA = _C_KROPE + LANES
_C_DQ = _C_GMLA + MLA_W
_C_DK = _C_DQ + DIFF_W
_C_DV = _C_DK + DIFF_W
_C_GDIFF = _C_DV + DIFF_W
_EVEN_COLS = _C_GDIFF + DIFF_W


def _even_in_kernel(x_ref, w_ref, wqn_ref, wqr_ref, wuk_ref, qg_ref, kvg_ref,
                    mc_ref, msa_ref, msb_ref, dc_ref, dsa_ref, dsb_ref,
                    qcat_ref, kcat_ref, ckv_ref, krope_ref, gate_ref, dq_ref,
                    dkf_ref, dkb_ref, dvf_ref, dvb_ref):
    xb = x_ref[...].astype(BF16)

    def proj(c0, c1):
        return jnp.dot(xb, w_ref[:, c0:c1], preferred_element_type=F32)

    mla_half = MLA_ROPE_DIM // 2
    diff_half = DIFF_ROT // 2
    mc, msa, msb = mc_ref[...], msa_ref[...], msb_ref[...]
    dc, dsa, dsb = dc_ref[...], dsa_ref[...], dsb_ref[...]

    qn = _rms(proj(_C_QLAT, _C_QLAT + Q_LORA), qg_ref[...]).astype(BF16)
    q_nope = jnp.dot(qn, wqn_ref[...], preferred_element_type=F32).astype(BF16)
    q_abs = jnp.dot(q_nope, wuk_ref[...], preferred_element_type=F32)
    q_rp = jnp.dot(qn, wqr_ref[...], preferred_element_type=F32)
    scale = MLA_QK_DIM ** -0.5 * LOG2E
    for h in range(MLA_HEADS):
        rot = _apply_rope(q_rp[:, h * LANES:(h + 1) * LANES], mc, msa, msb, mla_half)
        qcat_ref[:, h * MLA_QCAT:h * MLA_QCAT + LANES] = (q_abs[:, h * LANES:(h + 1) * LANES] * scale).astype(BF16)
        qcat_ref[:, h * MLA_QCAT + LANES:(h + 1) * MLA_QCAT] = (rot * scale).astype(BF16)

    kv = proj(_C_CKV, _C_GMLA)
    ckv = _rms(kv[:, :KV_LORA], kvg_ref[...])
    krope = _apply_rope(kv[:, KV_LORA:], mc, msa, msb, mla_half)
    ckv_ref[...] = ckv
    krope_ref[...] = krope[:, :MLA_ROPE_DIM]
    kcat_ref[:, :LANES] = ckv.astype(BF16)
    kcat_ref[:, LANES:] = krope.astype(BF16)

    gate_ref[:, :MLA_W] = _silu(proj(_C_GMLA, _C_DQ)).astype(BF16)
    gate_ref[:, MLA_W:] = _silu(proj(_C_GDIFF, _EVEN_COLS)).astype(BF16)

    dq = proj(_C_DQ, _C_DK)
    dk = proj(_C_DK, _C_DV)
    dv = proj(_C_DV, _C_GDIFF)
    dscale = DIFF_HD ** -0.5 * LOG2E
    tm = x_ref.shape[0]
    for h in range(DIFF_HEADS):
        sl = slice(h * LANES, (h + 1) * LANES)
        dq_ref[:, sl] = (_apply_rope(dq[:, sl], dc, dsa, dsb, diff_half) * dscale).astype(BF16)
        kr = _apply_rope(dk[:, sl], dc, dsa, dsb, diff_half)
        dkf_ref[pl.ds(2 * h, tm, stride=2 * DIFF_HEADS), :] = kr[:, :DIFF_HD]
        dkf_ref[pl.ds(2 * h + 1, tm, stride=2 * DIFF_HEADS), :] = kr[:, DIFF_HD:]
        dkb_ref[:, sl] = kr.astype(BF16)
        dvf_ref[pl.ds(h, tm, stride=DIFF_HEADS), :] = dv[:, sl]
    dvb_ref[...] = dv.astype(BF16)


def _even_in(x2d, wts, tabs, tab_rows_per_rep, tm):
    t = x2d.shape[0]
    tm = min(tm, t)
    nrep = tab_rows_per_rep // tm
    w_in, wqn, wqr, wuk, qg, kvg = wts
    row = lambda w: pl.BlockSpec((tm, w), lambda i: (i, 0))
    tab = pl.BlockSpec((tm, LANES), lambda i: (i % nrep, 0))
    outs = [
        (1, MLA_HEADS * MLA_QCAT, BF16),
        (1, MLA_QCAT, BF16),
        (1, KV_LORA, F32),
        (1, MLA_ROPE_DIM, F32),
        (1, MLA_W + DIFF_W, BF16),
        (1, DIFF_W, BF16),
        (2 * DIFF_HEADS, DIFF_HD, F32),
        (1, DIFF_W, BF16),
        (DIFF_HEADS, 2 * DIFF_HD, F32),
        (1, DIFF_W, BF16),
    ]
    return pl.pallas_call(
        _even_in_kernel,
        grid=(t // tm,),
        in_specs=[row(D_MODEL), _const_spec(w_in.shape), _const_spec(wqn.shape), _const_spec(wqr.shape),
                  _const_spec(wuk.shape), _const_spec(qg.shape), _const_spec(kvg.shape)] + [tab] * 6,
        out_specs=[pl.BlockSpec((tm * r, w), lambda i: (i, 0)) for r, w, _ in outs],
        out_shape=[jax.ShapeDtypeStruct((t * r, w), d) for r, w, d in outs],
        compiler_params=_params(("parallel",)),
        name="even_in",
    )(x2d, w_in, wqn, wqr, wuk, qg, kvg, *tabs)


def _softmax_step(s, vt, m_ref, l_ref, acc_ref, valid):
    if valid is not None:
        s = jnp.where(valid, s, NEG_INF)
    m_old = m_ref[...]
    m_new = jnp.maximum(m_old, jnp.max(s, axis=0, keepdims=True))
    alpha = jnp.exp2(m_old - m_new)
    p = jnp.exp2(s - m_new)
    l_ref[...] = alpha * l_ref[...] + jnp.sum(p, axis=0, keepdims=True)
    acc_ref[...] = alpha * acc_ref[...] + jnp.dot(vt, p.astype(BF16), preferred_element_type=F32)
    m_ref[...] = m_new


def _flash_scratch(n, tk, m_rows, dv):
    one = [pltpu.VMEM((tk, m_rows), F32), pltpu.VMEM((tk, m_rows), F32), pltpu.VMEM((1, m_rows), F32),
           pltpu.VMEM((1, m_rows), F32), pltpu.VMEM((dv, m_rows), F32)]
    return one * n


def _chunk_causal_flash(problems, q0, tq, tk):
    m_rows = problems[0][0].shape[0]
    for _, _, _, (_, _, m_ref, l_ref, acc_ref) in problems:
        m_ref[...] = jnp.full(m_ref.shape, NEG_INF, F32)
        l_ref[...] = jnp.zeros(l_ref.shape, F32)
        acc_ref[...] = jnp.zeros(acc_ref.shape, F32)
    n_full = q0 // tk

    def scores(prob, j):
        qs, get_k = prob[0], prob[1]
        return lax.dot_general(get_k(pl.multiple_of(j * tk, tk)), qs, _NT, preferred_element_type=F32)

    def consume(j, slot, valid):
        for _, _, get_vt, scr in problems:
            _softmax_step(scr[slot][...], get_vt(j), scr[2], scr[3], scr[4], valid)

    def step(j, slot):
        for prob in problems:
            prob[3][1 - slot][...] = scores(prob, j + 1)
        consume(j, slot, None)

    for prob in problems:
        prob[3][0][...] = scores(prob, 0)

    def body(jj, carry):
        step(2 * jj, 0)
        step(2 * jj + 1, 1)
        return carry

    lax.fori_loop(0, n_full // 2, body, 0)

    diag = (q0 // tq) % (tk // tq)
    shift = CHUNK.bit_length() - 1
    kchunk = lax.broadcasted_iota(jnp.int32, (tq, m_rows), 0) >> shift
    qchunk = (lax.broadcasted_iota(jnp.int32, (tq, m_rows), 1) & (tq - 1)) >> shift
    valid = kchunk <= qchunk

    def consume_last(slot):
        def sub_block(u, mask):
            rows = slice(u * tq, (u + 1) * tq)
            for _, _, get_vt, scr in problems:
                _softmax_step(scr[slot][rows, :], get_vt(n_full)[:, rows], scr[2], scr[3], scr[4], mask)

        for u in range(tk // tq):
            if u + 1 < tk // tq:
                pl.when(u < diag)(functools.partial(sub_block, u, None))
            pl.when(u == diag)(functools.partial(sub_block, u, valid))

    @pl.when(n_full % 2 == 1)
    def _():
        step(n_full - 1, 0)
        consume_last(1)

    @pl.when(n_full % 2 == 0)
    def _():
        consume_last(0)


MLA_GROUPS = 1


def _mla_prompt_kernel(q_ref, k_ref, vt_ref, o_ref, *scratch, tq, tk):
    q0 = pl.program_id(1) * tq
    get_k = lambda k0: k_ref[0, pl.ds(k0, tk), :]
    get_vt = lambda j: vt_ref[0, j]
    per = MLA_HEADS // MLA_GROUPS
    problems = []
    for g in range(MLA_GROUPS):
        qs = jnp.concatenate([q_ref[0, :, h * MLA_QCAT:(h + 1) * MLA_QCAT] for h in range(g * per, (g + 1) * per)],
                             axis=0)
        problems.append((qs, get_k, get_vt, scratch[5 * g:5 * g + 5]))
    _chunk_causal_flash(problems, q0, tq, tk)
    for g, (_, _, _, (_, _, _, l_ref, acc_ref)) in enumerate(problems):
        out = acc_ref[...] / l_ref[...]
        for hh in range(per):
            h = g * per + hh
            o_ref[0, :, h * KV_LORA:(h + 1) * KV_LORA] = out[:, hh * tq:(hh + 1) * tq].T.astype(o_ref.dtype)


def _key_major(x, tk):
    b, s, w = x.shape
    return x.reshape(b, s // tk, tk, w).transpose(0, 1, 3, 2)


def _mla_prompt(qcat, kcat, tq, tk):
    b, s, _ = qcat.shape
    tq, tk = min(tq, s), min(tk, s)
    m_rows = MLA_HEADS * tq
    vt = _key_major(kcat[:, :, :KV_LORA], tk)
    return pl.pallas_call(
        functools.partial(_mla_prompt_kernel, tq=tq, tk=tk),
        grid=(b, s // tq),
        in_specs=[pl.BlockSpec((1, tq, MLA_HEADS * MLA_QCAT), lambda bi, i: (bi, i, 0)),
                  pl.BlockSpec((1, s, MLA_QCAT), lambda bi, i: (bi, 0, 0)),
                  pl.BlockSpec((1, s // tk, KV_LORA, tk), lambda bi, i: (bi, 0, 0, 0))],
        out_specs=pl.BlockSpec((1, tq, MLA_HEADS * KV_LORA), lambda bi, i: (bi, i, 0)),
        out_shape=jax.ShapeDtypeStruct((b, s, MLA_HEADS * KV_LORA), BF16),
        scratch_shapes=_flash_scratch(MLA_GROUPS, tk, m_rows // MLA_GROUPS, KV_LORA),
        compiler_params=_params(("parallel", "parallel")),
        name="mla_prompt",
    )(qcat, kcat, vt)


def _lane_halves(x):
    lane = lax.broadcasted_iota(jnp.int32, x.shape, 1)
    zero = jnp.zeros_like(x)
    return jnp.concatenate([jnp.where(lane < LANES // 2, x, zero), jnp.where(lane < LANES // 2, zero, x)], axis=0)


def _diff_lambda(lq_ref, lk_ref, lam_init):
    e = jnp.exp(jnp.sum(lq_ref[...] * lk_ref[...], axis=-1, keepdims=True))
    return e[0:1] - e[1:2] + lam_init


def _diff_prompt_kernel(q_ref, k_ref, vt_ref, lq_ref, lk_ref, o_ref, *scratch, tq, tk, lam_init):
    q0 = pl.program_id(1) * tq
    lam = _diff_lambda(lq_ref, lk_ref, lam_init)
    problems = []
    for h in range(DIFF_HEADS):
        sl = slice(h * LANES, (h + 1) * LANES)
        problems.append((_lane_halves(q_ref[0, :, sl]),
                         lambda k0, sl=sl: k_ref[0, pl.ds(k0, tk), sl],
                         lambda j, sl=sl: vt_ref[0, j, sl, :],
                         scratch[5 * h:5 * h + 5]))
    _chunk_causal_flash(problems, q0, tq, tk)
    for h, (_, _, _, (_, _, _, l_ref, acc_ref)) in enumerate(problems):
        o = acc_ref[...] / l_ref[...]
        o_ref[0, :, h * LANES:(h + 1) * LANES] = (o[:, :tq] - lam * o[:, tq:]).T


def _diff_prompt(dq, dk, dv, lq, lk, lam_init, tq, tk):
    b, s, _ = dq.shape
    tq, tk = min(tq, s), min(tk, s)
    qspec = pl.BlockSpec((1, tq, DIFF_W), lambda bi, i: (bi, i, 0))
    return pl.pallas_call(
        functools.partial(_diff_prompt_kernel, tq=tq, tk=tk, lam_init=lam_init),
        grid=(b, s // tq),
        in_specs=[qspec, pl.BlockSpec((1, s, DIFF_W), lambda bi, i: (bi, 0, 0)),
                  pl.BlockSpec((1, s // tk, DIFF_W, tk), lambda bi, i: (bi, 0, 0, 0)),
                  _const_spec(lq.shape), _const_spec(lk.shape)],
        out_specs=qspec,
        out_shape=jax.ShapeDtypeStruct((b, s, DIFF_W), F32),
        scratch_shapes=_flash_scratch(DIFF_HEADS, tk, 2 * tq, LANES),
        compiler_params=_params(("parallel", "parallel")),
        name="diff_prompt",
    )(dq, dk, _key_major(dv, tk), lq, lk)


def _pad_rows(x, rows):
    return jnp.concatenate([x, jnp.zeros((rows - x.shape[0], x.shape[1]), x.dtype)], axis=0)


def _decode_softmax(s_c, s_n, v_cache, v_new, n_new):
    col = lax.broadcasted_iota(jnp.int32, s_n.shape, 1)
    s_n = jnp.where(col < n_new, s_n, NEG_INF)
    m = jnp.maximum(jnp.max(s_c, axis=-1, keepdims=True), jnp.max(s_n, axis=-1, keepdims=True))
    p_c = jnp.exp2(s_c - m)
    p_n = jnp.exp2(s_n - m)
    l = jnp.sum(p_c, axis=-1, keepdims=True) + jnp.sum(p_n, axis=-1, keepdims=True)
    acc = (jnp.dot(p_c.astype(BF16), v_cache, preferred_element_type=F32)
           + jnp.dot(p_n.astype(BF16), v_new, preferred_element_type=F32))
    return acc / l


def _mla_decode_kernel(q_ref, ckv_ref, kr_ref, kn_ref, o_ref, *, n_new):
    qs = jnp.concatenate([q_ref[0, :, h * MLA_QCAT:(h + 1) * MLA_QCAT] for h in range(MLA_HEADS)], axis=0)
    ckv = ckv_ref[0].astype(BF16)
    kr = kr_ref[0].astype(BF16)
    k_new = _pad_rows(kn_ref[0], LANES)
    s_c = (lax.dot_general(qs[:, :KV_LORA], ckv, _NT, preferred_element_type=F32)
           + lax.dot_general(qs[:, KV_LORA:KV_LORA + MLA_ROPE_DIM], kr, _NT, preferred_element_type=F32))
    s_n = lax.dot_general(qs, k_new, _NT, preferred_element_type=F32)
    out = _decode_softmax(s_c, s_n, ckv, k_new[:, :KV_LORA], n_new)
    for h in range(MLA_HEADS):
        o_ref[0, :, h * KV_LORA:(h + 1) * KV_LORA] = out[h * n_new:(h + 1) * n_new].astype(o_ref.dtype)


def _mla_decode(qcat, ckv_c, kr_c, kcat_new):
    b, n, _ = qcat.shape
    past = ckv_c.shape[1]
    return pl.pallas_call(
        functools.partial(_mla_decode_kernel, n_new=n),
        grid=(b,),
        in_specs=[pl.BlockSpec((1, n, MLA_HEADS * MLA_QCAT), lambda i: (i, 0, 0)),
                  pl.BlockSpec((1, past, KV_LORA), lambda i: (i, 0, 0)),
                  pl.BlockSpec((1, past, MLA_ROPE_DIM), lambda i: (i, 0, 0)),
                  pl.BlockSpec((1, n, MLA_QCAT), lambda i: (i, 0, 0))],
        out_specs=pl.BlockSpec((1, n, MLA_HEADS * KV_LORA), lambda i: (i, 0, 0)),
        out_shape=jax.ShapeDtypeStruct((b, n, MLA_HEADS * KV_LORA), BF16),
        compiler_params=_params(("parallel",)),
        name="mla_decode",
    )(qcat, ckv_c, kr_c, kcat_new)


def _diff_decode_kernel(q_ref, kc_ref, vc_ref, kn_ref, vn_ref, lq_ref, lk_ref, o_ref, *, n_new, past, lam_init):
    lam = _diff_lambda(lq_ref, lk_ref, lam_init)
    for h in range(DIFF_HEADS):
        sl = slice(h * LANES, (h + 1) * LANES)
        q = q_ref[0, :, sl]
        qs = _lane_halves(q)
        v = vc_ref[0, pl.ds(h, past, stride=DIFF_HEADS), :].astype(BF16)
        s_c = jnp.concatenate(
            [jnp.dot(q[:, :DIFF_HD], kc_ref[0, h, 0].astype(BF16), preferred_element_type=F32),
             jnp.dot(q[:, DIFF_HD:], kc_ref[0, h, 1].astype(BF16), preferred_element_type=F32)], axis=0)
        s_n = lax.dot_general(qs, _pad_rows(kn_ref[0, :, sl], LANES), _NT, preferred_element_type=F32)
        o = _decode_softmax(s_c, s_n, v, _pad_rows(vn_ref[0, :, sl], LANES), n_new)
        o_ref[0, :, sl] = o[:n_new] - lam * o[n_new:]


def _diff_decode(dq, kc, vc, dk_new, dv_new, lq, lk, lam_init):
    b, n, _ = dq.shape
    past = kc.shape[-1]
    new = pl.BlockSpec((1, n, DIFF_W), lambda i: (i, 0, 0))
    return pl.pallas_call(
        functools.partial(_diff_decode_kernel, n_new=n, past=past, lam_init=lam_init),
        grid=(b,),
        in_specs=[new, pl.BlockSpec((1, DIFF_HEADS, 2, DIFF_HD, past), lambda i: (i, 0, 0, 0, 0)),
                  pl.BlockSpec((1, past * DIFF_HEADS, 2 * DIFF_HD), lambda i: (i, 0, 0)),
                  new, new, _const_spec(lq.shape), _const_spec(lk.shape)],
        out_specs=new,
        out_shape=jax.ShapeDtypeStruct((b, n, DIFF_W), F32),
        compiler_params=_params(("parallel",)),
        name="diff_decode",
    )(dq, kc, vc.reshape(b, past * DIFF_HEADS, 2 * DIFF_HD), dk_new, dv_new, lq, lk)


def _even_out_kernel(olat_ref, odiff_ref, gate_ref, x_ref, wuv_ref, sg_ref, wout_ref, lng_ref, lnb_ref, y_ref,
                     *, lam_init):
    gate = gate_ref[...]
    o_mla = jnp.dot(olat_ref[...], wuv_ref[...], preferred_element_type=F32)
    parts = [(o_mla * gate[:, :MLA_W].astype(F32)).astype(BF16)]
    od = odiff_ref[...]
    for h in range(DIFF_HEADS):
        sl = slice(h * LANES, (h + 1) * LANES)
        n = _rms(od[:, sl], sg_ref[...]) * (1.0 - lam_init)
        parts.append((n * gate[:, MLA_W + h * LANES:MLA_W + (h + 1) * LANES].astype(F32)).astype(BF16))
    o = jnp.dot(jnp.concatenate(parts, axis=1), wout_ref[...], preferred_element_type=F32)
    y_ref[...] = _layernorm(ALPHA * x_ref[...] + o, lng_ref[...], lnb_ref[...])


def _even_out(olat, odiff, gate, x2d, wuv_bd, sg, wout, lng, lnb, lam_init, tm):
    t = x2d.shape[0]
    tm = min(tm, t)
    row = lambda w: pl.BlockSpec((tm, w), lambda i: (i, 0))
    return pl.pallas_call(
        functools.partial(_even_out_kernel, lam_init=lam_init),
        grid=(t // tm,),
        in_specs=[row(MLA_HEADS * KV_LORA), row(DIFF_W), row(MLA_W + DIFF_W), row(D_MODEL),
                  _const_spec(wuv_bd.shape), _const_spec(sg.shape), _const_spec(wout.shape),
                  _const_spec(lng.shape), _const_spec(lnb.shape)],
        out_specs=row(D_MODEL),
        out_shape=jax.ShapeDtypeStruct((t, D_MODEL), F32),
        compiler_params=_params(("parallel",)),
        name="even_out",
    )(olat, odiff, gate, x2d, wuv_bd, sg, wout, lng, lnb)


def _odd_in_kernel(x_ref, w_ref, q_ref, kf_ref, kb_ref, vf_ref, vb_ref, g_ref):
    xb = x_ref[...].astype(BF16)

    def proj(i):
        return jnp.dot(xb, w_ref[:, i * SB_W:(i + 1) * SB_W], preferred_element_type=F32)

    tm = x_ref.shape[0]

    def store_heads(ref, val):
        for h in range(SB_HEADS):
            ref[pl.ds(h, tm, stride=SB_HEADS), :] = val[:, h * SB_HD:(h + 1) * SB_HD]

    q_ref[...] = (proj(0) * (SB_HD ** -0.5 * LOG2E)).astype(BF16)
    k = proj(1)
    store_heads(kf_ref, k)
    kb_ref[...] = k.astype(BF16)
    v = proj(2)
    store_heads(vf_ref, v)
    vb_ref[...] = v.astype(BF16)
    g_ref[...] = _silu(proj(3)).astype(BF16)


def _odd_in(x2d, w_in, tm):
    t = x2d.shape[0]
    tm = min(tm, t)
    row = pl.BlockSpec((tm, SB_W), lambda i: (i, 0))
    heads = pl.BlockSpec((tm * SB_HEADS, SB_HD), lambda i: (i, 0))
    wide = lambda d: jax.ShapeDtypeStruct((t, SB_W), d)
    tall = jax.ShapeDtypeStruct((t * SB_HEADS, SB_HD), F32)
    return pl.pallas_call(
        _odd_in_kernel,
        grid=(t // tm,),
        in_specs=[pl.BlockSpec((tm, D_MODEL), lambda i: (i, 0)), _const_spec(w_in.shape)],
        out_specs=[row, heads, row, heads, row, row],
        out_shape=[wide(BF16), tall, wide(BF16), tall, wide(BF16), wide(BF16)],
        compiler_params=_params(("parallel",)),
        name="odd_in",
    )(x2d, w_in)


SB_EXIT_LOG2 = 160.0


def _log2(x):
    return jnp.log2(x)


def _later_matrix(n):
    return (lax.broadcasted_iota(jnp.int32, (n, n), 0) > lax.broadcasted_iota(jnp.int32, (n, n), 1)).astype(BF16)


def _softplus2(z):
    return jnp.maximum(z, 0.0) + _log2(1.0 + jnp.exp2(-jnp.abs(z)))


def _sb_prompt_kernel(q_ref, k_ref, vt_ref, o_ref, c_ref, acc_ref, x_ref, sp0_ref, *, tq, pairs):
    i = pl.program_id(2)
    ut = (lax.broadcasted_iota(jnp.int32, (tq, tq), 1) > lax.broadcasted_iota(jnp.int32, (tq, tq), 0)).astype(BF16)
    keys = lax.broadcasted_iota(jnp.int32, (tq, 2 * tq), 0)
    queries = lax.broadcasted_iota(jnp.int32, (tq, 2 * tq), 1) & (tq - 1)
    diag_valid = keys < queries
    half = LANES // 2
    sls = [slice(p * LANES, (p + 1) * LANES) for p in range(pairs)]
    qss = [_lane_halves(q_ref[0, :, sl]) for sl in sls]

    def all_pairs(j, valid, first):
        k0 = pl.multiple_of(j * tq, tq)
        zs = [lax.dot_general(k_ref[0, pl.ds(k0, tq), sl], qss[p], _NT, preferred_element_type=F32)
              for p, sl in enumerate(sls)]
        laters = []
        for p, z in enumerate(zs):
            sp = _softplus2(z)
            x_ref[p] = z - sp
            if valid is not None:
                sp = jnp.where(valid, sp, 0.0)
            spb = sp.astype(BF16)
            sp0_ref[p] = spb[:1, :].astype(F32)
            laters.append(jnp.dot(ut, spb, preferred_element_type=F32))
        cmin = None
        for p, sl in enumerate(sls):
            a = jnp.exp2(x_ref[p] - laters[p])
            if valid is not None:
                a = jnp.where(valid, a, 0.0)
            pv = jnp.dot(vt_ref[0, j, sl, :], a.astype(BF16), preferred_element_type=F32)
            tot = laters[p][:1, :] + sp0_ref[p]
            if first:
                acc_ref[p] = pv
                c = tot
            else:
                c0 = c_ref[p]
                acc_ref[p] += pv * jnp.exp2(-c0)
                c = c0 + tot
            c_ref[p] = c
            cmin = c if cmin is None else jnp.minimum(cmin, c)
        return jnp.min(cmin)

    cmin = all_pairs(i, diag_valid, True)
    lax.while_loop(lambda jc: (jc[0] >= 0) & (jc[1] < SB_EXIT_LOG2),
                   lambda jc: (jc[0] - 1, all_pairs(jc[0], None, False)),
                   (i - 1, cmin))

    for p, sl in enumerate(sls):
        acc = acc_ref[p]
        o_t = jnp.concatenate([acc[:half, :tq], acc[half:, tq:]], axis=0)
        o_ref[0, :, sl] = o_t.T.astype(o_ref.dtype)


def _sb_prompt(q, k, v, tq, pairs):
    b, s, w = q.shape
    tq = min(tq, s)
    gw = pairs * LANES
    qspec = pl.BlockSpec((1, tq, gw), lambda bi, g, i: (bi, i, g))
    return pl.pallas_call(
        functools.partial(_sb_prompt_kernel, tq=tq, pairs=pairs),
        grid=(b, w // gw, s // tq),
        in_specs=[qspec, pl.BlockSpec((1, s, gw), lambda bi, g, i: (bi, 0, g)),
                  pl.BlockSpec((1, s // tq, gw, tq), lambda bi, g, i: (bi, 0, g, 0))],
        out_specs=qspec,
        out_shape=jax.ShapeDtypeStruct((b, s, w), BF16),
        scratch_shapes=[pltpu.VMEM((pairs, 1, 2 * tq), F32), pltpu.VMEM((pairs, LANES, 2 * tq), F32),
                        pltpu.VMEM((pairs, tq, 2 * tq), F32), pltpu.VMEM((pairs, 1, 2 * tq), F32)],
        compiler_params=_params(("parallel", "parallel", "parallel")),
        name="sb_prompt",
    )(q, k, _key_major(v, tq))


def _sb_rows_block(score_fns, value_fns, u, cs, valid):
    zs = [f() for f in score_fns]
    xs, sp0s, laters = [], [], []
    for z in zs:
        sp = _softplus2(z)
        xs.append(z - sp)
        if valid is not None:
            sp = jnp.where(valid, sp, 0.0)
        spb = sp.astype(BF16)
        sp0s.append(spb[:, :1].astype(F32))
        laters.append(jnp.dot(spb, u, preferred_element_type=F32))
    out = []
    for x, sp0, later, c, value in zip(xs, sp0s, laters, cs, value_fns):
        a = jnp.exp2(x - later - c)
        if valid is not None:
            a = jnp.where(valid, a, 0.0)
        out.append((value(a.astype(BF16)), c + later[:, :1] + sp0))
    return out


def _pair_fns_new(qs, kb, vb):
    return (lambda: lax.dot_general(qs, kb, _NT, preferred_element_type=F32),
            lambda a: jnp.dot(a, vb, preferred_element_type=F32))


def _pair_fns_cache(qs, kt_ref, vt_ref, h0, t0, tk, n_new):
    hd = LANES // 2
    q0, q1 = qs[:n_new, :hd], qs[n_new:, hd:]
    pad = jnp.zeros((n_new, hd), F32)

    def score():
        return jnp.concatenate(
            [jnp.dot(q0, kt_ref[0, h0, :, t0:t0 + tk].astype(BF16), preferred_element_type=F32),
             jnp.dot(q1, kt_ref[0, h0 + 1, :, t0:t0 + tk].astype(BF16), preferred_element_type=F32)], axis=0)

    def value(a):
        pv0 = lax.dot_general(a[:n_new], vt_ref[0, h0, :, t0:t0 + tk].astype(BF16), _NT, preferred_element_type=F32)
        pv1 = lax.dot_general(a[n_new:], vt_ref[0, h0 + 1, :, t0:t0 + tk].astype(BF16), _NT,
                              preferred_element_type=F32)
        return jnp.concatenate([jnp.concatenate([pv0, pad], axis=1), jnp.concatenate([pad, pv1], axis=1)], axis=0)

    return score, value


def _sb_decode_recent_kernel(q_ref, kw_ref, vw_ref, kn_ref, vn_ref, acc_ref, c_ref, *, n_new, tk, pairs):
    sls = [slice(p * LANES, (p + 1) * LANES) for p in range(pairs)]
    qss = [_lane_halves(q_ref[0, :, sl]) for sl in sls]
    rows = lax.broadcasted_iota(jnp.int32, (2 * n_new, LANES), 0) & (n_new - 1)
    cols = lax.broadcasted_iota(jnp.int32, (2 * n_new, LANES), 1)
    zero = jnp.zeros((2 * n_new, 1), F32)
    fns = [_pair_fns_new(qs, _pad_rows(kn_ref[0, :, sl], LANES), _pad_rows(vn_ref[0, :, sl], LANES))
           for qs, sl in zip(qss, sls)]
    state = _sb_rows_block([f for f, _ in fns], [g for _, g in fns], _later_matrix(LANES), [zero] * pairs,
                           cols < rows)
    u = _later_matrix(tk)
    for j in reversed(range(kw_ref.shape[3] // tk)):
        fns = [_pair_fns_cache(qs, kw_ref, vw_ref, 2 * p, j * tk, tk, n_new) for p, qs in enumerate(qss)]
        blk = _sb_rows_block([f for f, _ in fns], [g for _, g in fns], u, [c for _, c in state], None)
        state = [(acc + pv, c) for (acc, _), (pv, c) in zip(state, blk)]
    for p, (acc, c) in enumerate(state):
        acc_ref[0, p] = acc
        c_ref[0, p] = jnp.broadcast_to(c, (2 * n_new, LANES))


def _sb_decode_rest_kernel(q_ref, kc_ref, vc_ref, acc_in_ref, c_in_ref, acc_ref, *, n_new, tk):
    qs = _lane_halves(q_ref[0])
    acc = acc_in_ref[0, 0]
    c = c_in_ref[0, 0][:, :1]
    u = _later_matrix(tk)
    for j in reversed(range(kc_ref.shape[3] // tk)):
        def visit(acc, c, j=j):
            score, value = _pair_fns_cache(qs, kc_ref, vc_ref, 0, j * tk, tk, n_new)
            (pv, c), = _sb_rows_block([score], [value], u, [c], None)
            return acc + pv, c

        acc, c = lax.cond(jnp.min(c) < SB_EXIT_LOG2, visit, lambda acc, c: (acc, c), acc, c)
    acc_ref[0, 0] = acc


def _sb_decode_finish_kernel(acc_ref, o_ref, *, n_new, pairs):
    lane = lax.broadcasted_iota(jnp.int32, (n_new, LANES), 1)
    for p in range(pairs):
        acc = acc_ref[0, p]
        o_ref[0, :, p * LANES:(p + 1) * LANES] = jnp.where(lane < LANES // 2, acc[:n_new], acc[n_new:]).astype(o_ref.dtype)


def _sb_decode(q, kc4, vc4, k_new, v_new, tk, window):
    b, n, w = q.shape
    past, heads, hd = kc4.shape[1:]
    window = min(window, past)
    if past % window:
        raise NotImplementedError("the decode window must divide the cache length")
    tk = min(tk, window)
    pairs = w // LANES
    rest = past - window
    kt, vt = jnp.transpose(kc4, (0, 2, 3, 1)), jnp.transpose(vc4, (0, 2, 3, 1))
    state_shape = jax.ShapeDtypeStruct((b, pairs, 2 * n, LANES), F32)
    new = pl.BlockSpec((1, n, w), lambda i: (i, 0, 0))
    win = pl.BlockSpec((1, heads, hd, window), lambda i: (i, 0, 0, past // window - 1))
    state = pl.BlockSpec((1, pairs, 2 * n, LANES), lambda i: (i, 0, 0, 0))
    acc, c = pl.pallas_call(
        functools.partial(_sb_decode_recent_kernel, n_new=n, tk=tk, pairs=pairs),
        grid=(b,),
        in_specs=[new, win, win, new, new],
        out_specs=[state, state],
        out_shape=[state_shape, state_shape],
        compiler_params=_params(("parallel",)),
        name="sb_decode_recent",
    )(q, kt, vt, k_new, v_new)

    if rest:
        def older(acc, c):
            new1 = pl.BlockSpec((1, n, LANES), lambda i, g: (i, 0, g))
            cache = pl.BlockSpec((1, 2, hd, rest), lambda i, g: (i, g, 0, 0))
            state1 = pl.BlockSpec((1, 1, 2 * n, LANES), lambda i, g: (i, g, 0, 0))
            return pl.pallas_call(
                functools.partial(_sb_decode_rest_kernel, n_new=n, tk=math.gcd(tk, rest)),
                grid=(b, pairs),
                in_specs=[new1, cache, cache, state1, state1],
                out_specs=state1,
                out_shape=state_shape,
                compiler_params=_params(("parallel", "parallel")),
                name="sb_decode_rest",
            )(q, kt[..., :rest], vt[..., :rest], acc, c)

        acc = lax.cond(jnp.min(c) < SB_EXIT_LOG2, older, lambda acc, c: acc, acc, c)

    return pl.pallas_call(
        functools.partial(_sb_decode_finish_kernel, n_new=n, pairs=pairs),
        grid=(b,),
        in_specs=[state],
        out_specs=new,
        out_shape=jax.ShapeDtypeStruct((b, n, w), BF16),
        compiler_params=_params(("parallel",)),
        name="sb_decode_finish",
    )(acc)


def _odd_out_kernel(o_ref, g_ref, x_ref, wout_ref, lng_ref, lnb_ref, y_ref):
    a = (o_ref[...].astype(F32) * g_ref[...].astype(F32)).astype(BF16)
    o = jnp.dot(a, wout_ref[...], preferred_element_type=F32)
    y_ref[...] = _layernorm(ALPHA * x_ref[...] + o, lng_ref[...], lnb_ref[...])


def _odd_out(o, gate, x2d, wout, lng, lnb, tm):
    t = x2d.shape[0]
    tm = min(tm, t)
    row = lambda w: pl.BlockSpec((tm, w), lambda i: (i, 0))
    return pl.pallas_call(
        _odd_out_kernel,
        grid=(t // tm,),
        in_specs=[row(SB_W), row(SB_W), row(D_MODEL), _const_spec(wout.shape),
                  _const_spec(lng.shape), _const_spec(lnb.shape)],
        out_specs=row(D_MODEL),
        out_shape=jax.ShapeDtypeStruct((t, D_MODEL), F32),
        compiler_params=_params(("parallel",)),
        name="odd_out",
    )(o, gate, x2d, wout, lng, lnb)


def _even_weights(w_in, q_g, w_uq, kv_g, w_uk):
    pad = jnp.zeros((D_MODEL, LANES - MLA_ROPE_DIM), w_in.dtype)
    o = np.cumsum([0, Q_LORA, KV_LORA, MLA_ROPE_DIM, MLA_W, DIFF_W, DIFF_W, DIFF_W, DIFF_W])
    w_re = jnp.concatenate([w_in[:, o[0]:o[3]], pad, w_in[:, o[3]:]], axis=1).astype(BF16)
    w_uq3 = w_uq.reshape(Q_LORA, MLA_HEADS, MLA_QK_DIM)
    wqn = w_uq3[:, :, :MLA_NOPE_DIM].reshape(Q_LORA, MLA_HEADS * MLA_NOPE_DIM).astype(BF16)
    wqr = jnp.pad(w_uq3[:, :, MLA_NOPE_DIM:], ((0, 0), (0, 0), (0, LANES - MLA_ROPE_DIM)))
    wqr = wqr.reshape(Q_LORA, MLA_HEADS * LANES).astype(BF16)
    eye = jnp.eye(MLA_HEADS, dtype=w_uk.dtype)
    wuk_bd = jnp.einsum('chn,hg->hngc', w_uk, eye).reshape(MLA_HEADS * MLA_NOPE_DIM, MLA_HEADS * KV_LORA).astype(BF16)
    return w_re, wqn, wqr, wuk_bd, q_g.reshape(1, Q_LORA), kv_g.reshape(1, KV_LORA)


def _uv_block_diag(w_uv):
    eye = jnp.eye(MLA_HEADS, dtype=w_uv.dtype)
    return jnp.einsum('chv,hg->hcgv', w_uv, eye).reshape(MLA_HEADS * KV_LORA, MLA_HEADS * MLA_V_DIM).astype(BF16)


TM_PROJ = 512
TQ_MLA = 512
TQ_DIFF = 512
TK_SOFTMAX = 512
TQ_SB = 256
SB_PAIRS = 8
TK_SB_DECODE = 256
SB_DECODE_WINDOW = 256


def kernel(x_prompt, x_sample, cache_mla_ckv, cache_mla_krope, cache_diff_k, cache_diff_v, cache_sb_k, cache_sb_v,
           w_in_even, q_norm_g, w_uq, kv_norm_g, w_uk, w_uv, diff_lambda_q, diff_lambda_k, diff_subln_g,
           w_out_even, ln_even_g, ln_even_b, w_in_odd, w_out_odd, ln_odd_g, ln_odd_b):
    bp, sp, _ = x_prompt.shape
    bs, ss, _ = x_sample.shape
    past = cache_sb_k.shape[2]
    if past % CHUNK or ss > CHUNK or (ss & (ss - 1)) or sp % TQ_SB:
        raise NotImplementedError("decode frames must share one chunk after a chunk-aligned cache")
    tp, ts = bp * sp, bs * ss
    pos_p = jnp.arange(sp, dtype=jnp.int32)
    pos_s = past + jnp.arange(ss, dtype=jnp.int32)
    pos_s_rows = jnp.tile(pos_s, bs)

    def tables(pos):
        mc, msa, msb, _ = _rope_tables(pos, MLA_ROPE_DIM, MLA_ROPE_THETA, LANES)
        dc, dsa, dsb, _ = _rope_tables(pos, DIFF_ROT, ROPE_THETA, DIFF_HD)
        return mc, msa, msb, dc, dsa, dsb

    yp = x_prompt.reshape(tp, D_MODEL)
    ys = x_sample.reshape(ts, D_MODEL)
    even_p, even_s, odd_p, odd_s = [], [], [], []
    for layer in range(DEPTH):
        i = layer // 2
        if layer % 2 == 0:
            lam_init = 0.8 - 0.6 * math.exp(-0.3 * layer)
            wts = _even_weights(w_in_even[i], q_norm_g[i], w_uq[i], kv_norm_g[i], w_uk[i])
            wuv_bd = _uv_block_diag(w_uv[i])
            sg = diff_subln_g[i].reshape(1, 2 * DIFF_HD)
            wout = w_out_even[i].astype(BF16)
            lng, lnb = ln_even_g[i].reshape(1, D_MODEL), ln_even_b[i].reshape(1, D_MODEL)
            lq, lk = diff_lambda_q[i], diff_lambda_k[i]
            qcat, kcat, ckv, krope, gate, dq, dkf, dkb, dvf, dvb = _even_in(yp, wts, tables(pos_p), sp, TM_PROJ)
            olat = _mla_prompt(qcat.reshape(bp, sp, -1), kcat.reshape(bp, sp, -1), TQ_MLA, TK_SOFTMAX)
            odiff = _diff_prompt(dq.reshape(bp, sp, -1), dkb.reshape(bp, sp, -1), dvb.reshape(bp, sp, -1),
                                 lq, lk, lam_init, TQ_DIFF, TK_SOFTMAX)
            yp_new = _even_out(olat.reshape(tp, -1), odiff.reshape(tp, -1), gate, yp, wuv_bd, sg, wout, lng, lnb,
                               lam_init, TM_PROJ)
            even_p.append((ckv.reshape(bp, sp, KV_LORA), krope.reshape(bp, sp, MLA_ROPE_DIM),
                           dkf.reshape(bp, sp, DIFF_HEADS, 2, DIFF_HD), dvf.reshape(bp, sp, DIFF_HEADS, 2 * DIFF_HD)))
            qcat, kcat, ckv, krope, gate, dq, dkf, dkb, dvf, dvb = _even_in(ys, wts, tables(pos_s_rows), ts, TM_PROJ)
            olat = _mla_decode(qcat.reshape(bs, ss, -1), cache_mla_ckv[i], cache_mla_krope[i],
                               kcat.reshape(bs, ss, -1))
            odiff = _diff_decode(dq.reshape(bs, ss, -1), jnp.transpose(cache_diff_k[i], (0, 2, 3, 4, 1)),
                                 cache_diff_v[i],
                                 dkb.reshape(bs, ss, -1),
                                 dvb.reshape(bs, ss, -1), lq, lk, lam_init)
            ys_new = _even_out(olat.reshape(ts, -1), odiff.reshape(ts, -1), gate, ys, wuv_bd, sg, wout, lng, lnb,
                               lam_init, TM_PROJ)
            even_s.append((ckv.reshape(bs, ss, KV_LORA), krope.reshape(bs, ss, MLA_ROPE_DIM),
                           dkf.reshape(bs, ss, DIFF_HEADS, 2, DIFF_HD), dvf.reshape(bs, ss, DIFF_HEADS, 2 * DIFF_HD)))
            yp, ys = yp_new, ys_new
        else:
            w_in = w_in_odd[i].astype(BF16)
            wout = w_out_odd[i].astype(BF16)
            lng, lnb = ln_odd_g[i].reshape(1, D_MODEL), ln_odd_b[i].reshape(1, D_MODEL)
            q, kf, kb, vf, vb, gate = _odd_in(yp, w_in, TM_PROJ)
            o = _sb_prompt(q.reshape(bp, sp, -1), kb.reshape(bp, sp, -1), vb.reshape(bp, sp, -1), TQ_SB, SB_PAIRS)
            yp_new = _odd_out(o.reshape(tp, -1), gate, yp, wout, lng, lnb, TM_PROJ)
            odd_p.append((kf.reshape(bp, sp, SB_HEADS, SB_HD), vf.reshape(bp, sp, SB_HEADS, SB_HD)))
            q, kf, kb, vf, vb, gate = _odd_in(ys, w_in, TM_PROJ)
            o = _sb_decode(q.reshape(bs, ss, -1), cache_sb_k[i], cache_sb_v[i], kb.reshape(bs, ss, -1),
                           vb.reshape(bs, ss, -1), TK_SB_DECODE, SB_DECODE_WINDOW)
            ys_new = _odd_out(o.reshape(ts, -1), gate, ys, wout, lng, lnb, TM_PROJ)
            odd_s.append((kf.reshape(bs, ss, SB_HEADS, SB_HD), vf.reshape(bs, ss, SB_HEADS, SB_HD)))
            yp, ys = yp_new, ys_new
    stack = lambda lst, k: jnp.stack([t[k] for t in lst])
    return (yp.reshape(bp, sp, D_MODEL), ys.reshape(bs, ss, D_MODEL),
            stack(even_p, 0), stack(even_p, 1), stack(even_p, 2), stack(even_p, 3),
            stack(odd_p, 0), stack(odd_p, 1),
            stack(even_s, 0), stack(even_s, 1), stack(even_s, 2), stack(even_s, 3),
            stack(odd_s, 0), stack(odd_s, 1))
```

```python
import functools
import math

import numpy as np
import jax
import jax.numpy as jnp
from jax import lax
from jax.experimental import pallas as pl
from jax.experimental.pallas import tpu as pltpu

F32 = jnp.float32
BF16 = jnp.bfloat16

D_MODEL = 1024
DEPTH = 2
CHUNK = 64
ALPHA = (2 * DEPTH) ** 0.25
EPS = 1e-5
NEG_INF = -1e30
MLA_HEADS = 8
Q_LORA = 256
KV_LORA = 128
MLA_NOPE_DIM = 64
MLA_ROPE_DIM = 32
MLA_QK_DIM = MLA_NOPE_DIM + MLA_ROPE_DIM
MLA_V_DIM = 64
MLA_ROPE_THETA = 10000.0
MLA_W = MLA_HEADS * MLA_V_DIM
DIFF_HEADS = 4
DIFF_HD = 64
DIFF_ROT = DIFF_HD // 4
ROPE_THETA = 500000.0
DIFF_W = DIFF_HEADS * 2 * DIFF_HD
SB_HEADS = 16
SB_HD = 64
SB_W = SB_HEADS * SB_HD

LANES = 128
MLA_QCAT = 2 * LANES
VMEM_LIMIT = 56 * 1024 * 1024

_NT = (((1,), (1,)), ((), ()))
LOG2E = math.log2(math.e)


def _params(sem):
    return pltpu.CompilerParams(dimension_semantics=sem, vmem_limit_bytes=VMEM_LIMIT)


def _const_spec(shape):
    nd = len(shape)
    return pl.BlockSpec(shape, lambda *_: (0,) * nd, pipeline_mode=pl.Buffered(1))


def _rope_tables(pos, rot_dim, theta, period):
    half = rot_dim // 2
    inv = jnp.exp(-math.log(theta) * jnp.arange(half, dtype=F32) * (2.0 / rot_dim))
    ang = pos.astype(F32)[:, None] * inv[None, :]
    cos, sin = jnp.cos(ang), jnp.sin(ang)
    lane = np.arange(LANES) % period
    idx = lane % half
    in_rot = lane < rot_dim
    first = lane < half
    second = in_rot & ~first
    cos_l, sin_l = cos[:, idx], sin[:, idx]
    c = jnp.where(in_rot[None, :], cos_l, 1.0)
    sa = jnp.where(first[None, :], -sin_l, 0.0)
    sb = jnp.where(second[None, :], sin_l, 0.0)
    return c, sa, sb, half


def _apply_rope(x, c, sa, sb, half):
    return x * c + pltpu.roll(x, LANES - half, 1) * sa + pltpu.roll(x, half, 1) * sb


def _silu(x):
    return x / (1.0 + jnp.exp(-x))


def _rms(x, g):
    ms = jnp.mean(x * x, axis=-1, keepdims=True)
    return x * lax.rsqrt(ms + EPS) * g


def _layernorm(r, g, b):
    mu = jnp.mean(r, axis=-1, keepdims=True)
    rc = r - mu
    var = jnp.mean(rc * rc, axis=-1, keepdims=True)
    return rc * lax.rsqrt(var + EPS) * g + b


_C_QLAT = 0
_C_CKV = Q_LORA
_C_KROPE = _C_CKV + KV_LORA
_C_GMLA = _C_KROPE + LANES
_C_DQ = _C_GMLA + MLA_W
_C_DK = _C_DQ + DIFF_W
_C_DV = _C_DK + DIFF_W
_C_GDIFF = _C_DV + DIFF_W
_EVEN_COLS = _C_GDIFF + DIFF_W


def _even_in_kernel(x_ref, w_ref, wqn_ref, wqr_ref, wuk_ref, qg_ref, kvg_ref,
                    mc_ref, msa_ref, msb_ref, dc_ref, dsa_ref, dsb_ref,
                    qcat_ref, kcat_ref, ckv_ref, krope_ref, gate_ref, dq_ref,
                    dkf_ref, dkb_ref, dvf_ref, dvb_ref):
    xb = x_ref[...].astype(BF16)

    def proj(c0, c1):
        return jnp.dot(xb, w_ref[:, c0:c1], preferred_element_type=F32)

    mla_half = MLA_ROPE_DIM // 2
    diff_half = DIFF_ROT // 2
    mc, msa, msb = mc_ref[...], msa_ref[...], msb_ref[...]
    dc, dsa, dsb = dc_ref[...], dsa_ref[...], dsb_ref[...]

    qn = _rms(proj(_C_QLAT, _C_QLAT + Q_LORA), qg_ref[...]).astype(BF16)
    q_nope = jnp.dot(qn, wqn_ref[...], preferred_element_type=F32).astype(BF16)
    q_abs = jnp.dot(q_nope, wuk_ref[...], preferred_element_type=F32)
    q_rp = jnp.dot(qn, wqr_ref[...], preferred_element_type=F32)
    scale = MLA_QK_DIM ** -0.5 * LOG2E
    for h in range(MLA_HEADS):
        rot = _apply_rope(q_rp[:, h * LANES:(h + 1) * LANES], mc, msa, msb, mla_half)
        qcat_ref[:, h * MLA_QCAT:h * MLA_QCAT + LANES] = (q_abs[:, h * LANES:(h + 1) * LANES] * scale).astype(BF16)
        qcat_ref[:, h * MLA_QCAT + LANES:(h + 1) * MLA_QCAT] = (rot * scale).astype(BF16)

    kv = proj(_C_CKV, _C_GMLA)
    ckv = _rms(kv[:, :KV_LORA], kvg_ref[...])
    krope = _apply_rope(kv[:, KV_LORA:], mc, msa, msb, mla_half)
    ckv_ref[...] = ckv
    krope_ref[...] = krope[:, :MLA_ROPE_DIM]
    kcat_ref[:, :LANES] = ckv.astype(BF16)
    kcat_ref[:, LANES:] = krope.astype(BF16)

    gate_ref[:, :MLA_W] = _silu(proj(_C_GMLA, _C_DQ)).astype(BF16)
    gate_ref[:, MLA_W:] = _silu(proj(_C_GDIFF, _EVEN_COLS)).astype(BF16)

    dq = proj(_C_DQ, _C_DK)
    dk = proj(_C_DK, _C_DV)
    dv = proj(_C_DV, _C_GDIFF)
    dscale = DIFF_HD ** -0.5 * LOG2E
    tm = x_ref.shape[0]
    for h in range(DIFF_HEADS):
        sl = slice(h * LANES, (h + 1) * LANES)
        dq_ref[:, sl] = (_apply_rope(dq[:, sl], dc, dsa, dsb, diff_half) * dscale).astype(BF16)
        kr = _apply_rope(dk[:, sl], dc, dsa, dsb, diff_half)
        dkf_ref[pl.ds(2 * h, tm, stride=2 * DIFF_HEADS), :] = kr[:, :DIFF_HD]
        dkf_ref[pl.ds(2 * h + 1, tm, stride=2 * DIFF_HEADS), :] = kr[:, DIFF_HD:]
        dkb_ref[:, sl] = kr.astype(BF16)
        dvf_ref[pl.ds(h, tm, stride=DIFF_HEADS), :] = dv[:, sl]
    dvb_ref[...] = dv.astype(BF16)


def _even_in(x2d, wts, tabs, tab_rows_per_rep, tm):
    t = x2d.shape[0]
    tm = min(tm, t)
    nrep = tab_rows_per_rep // tm
    w_in, wqn, wqr, wuk, qg, kvg = wts
    row = lambda w: pl.BlockSpec((tm, w), lambda i: (i, 0))
    tab = pl.BlockSpec((tm, LANES), lambda i: (i % nrep, 0))
    outs = [
        (1, MLA_HEADS * MLA_QCAT, BF16),
        (1, MLA_QCAT, BF16),
        (1, KV_LORA, F32),
        (1, MLA_ROPE_DIM, F32),
        (1, MLA_W + DIFF_W, BF16),
        (1, DIFF_W, BF16),
        (2 * DIFF_HEADS, DIFF_HD, F32),
        (1, DIFF_W, BF16),
        (DIFF_HEADS, 2 * DIFF_HD, F32),
        (1, DIFF_W, BF16),
    ]
    return pl.pallas_call(
        _even_in_kernel,
        grid=(t // tm,),
        in_specs=[row(D_MODEL), _const_spec(w_in.shape), _const_spec(wqn.shape), _const_spec(wqr.shape),
                  _const_spec(wuk.shape), _const_spec(qg.shape), _const_spec(kvg.shape)] + [tab] * 6,
        out_specs=[pl.BlockSpec((tm * r, w), lambda i: (i, 0)) for r, w, _ in outs],
        out_shape=[jax.ShapeDtypeStruct((t * r, w), d) for r, w, d in outs],
        compiler_params=_params(("parallel",)),
        name="even_in",
    )(x2d, w_in, wqn, wqr, wuk, qg, kvg, *tabs)


def _softmax_step(s, vt, m_ref, l_ref, acc_ref, valid):
    if valid is not None:
        s = jnp.where(valid, s, NEG_INF)
    m_old = m_ref[...]
    m_new = jnp.maximum(m_old, jnp.max(s, axis=0, keepdims=True))
    alpha = jnp.exp2(m_old - m_new)
    p = jnp.exp2(s - m_new)
    l_ref[...] = alpha * l_ref[...] + jnp.sum(p, axis=0, keepdims=True)
    acc_ref[...] = alpha * acc_ref[...] + jnp.dot(vt, p.astype(BF16), preferred_element_type=F32)
    m_ref[...] = m_new


def _flash_scratch(n, tk, m_rows, dv):
    one = [pltpu.VMEM((tk, m_rows), F32), pltpu.VMEM((tk, m_rows), F32), pltpu.VMEM((1, m_rows), F32),
           pltpu.VMEM((1, m_rows), F32), pltpu.VMEM((dv, m_rows), F32)]
    return one * n


def _chunk_causal_flash(problems, q0, tq, tk):
    m_rows = problems[0][0].shape[0]
    for _, _, _, (_, _, m_ref, l_ref, acc_ref) in problems:
        m_ref[...] = jnp.full(m_ref.shape, NEG_INF, F32)
        l_ref[...] = jnp.zeros(l_ref.shape, F32)
        acc_ref[...] = jnp.zeros(acc_ref.shape, F32)
    n_full = q0 // tk

    def scores(prob, j):
        qs, get_k = prob[0], prob[1]
        return lax.dot_general(get_k(pl.multiple_of(j * tk, tk)), qs, _NT, preferred_element_type=F32)

    def consume(j, slot, valid):
        for _, _, get_vt, scr in problems:
            _softmax_step(scr[slot][...], get_vt(j), scr[2], scr[3], scr[4], valid)

    def step(j, slot):
        for prob in problems:
            prob[3][1 - slot][...] = scores(prob, j + 1)
        consume(j, slot, None)

    for prob in problems:
        prob[3][0][...] = scores(prob, 0)

    def body(jj, carry):
        step(2 * jj, 0)
        step(2 * jj + 1, 1)
        return carry

    lax.fori_loop(0, n_full // 2, body, 0)

    diag = (q0 // tq) % (tk // tq)
    shift = CHUNK.bit_length() - 1
    kchunk = lax.broadcasted_iota(jnp.int32, (tq, m_rows), 0) >> shift
    qchunk = (lax.broadcasted_iota(jnp.int32, (tq, m_rows), 1) & (tq - 1)) >> shift
    valid = kchunk <= qchunk

    def consume_last(slot):
        def sub_block(u, mask):
            rows = slice(u * tq, (u + 1) * tq)
            for _, _, get_vt, scr in problems:
                _softmax_step(scr[slot][rows, :], get_vt(n_full)[:, rows], scr[2], scr[3], scr[4], mask)

        for u in range(tk // tq):
            if u + 1 < tk // tq:
                pl.when(u < diag)(functools.partial(sub_block, u, None))
            pl.when(u == diag)(functools.partial(sub_block, u, valid))

    @pl.when(n_full % 2 == 1)
    def _():
        step(n_full - 1, 0)
        consume_last(1)

    @pl.when(n_full % 2 == 0)
    def _():
        consume_last(0)


MLA_GROUPS = 1


def _mla_prompt_kernel(q_ref, k_ref, vt_ref, o_ref, *scratch, tq, tk):
    q0 = pl.program_id(1) * tq
    get_k = lambda k0: k_ref[0, pl.ds(k0, tk), :]
    get_vt = lambda j: vt_ref[0, j]
    per = MLA_HEADS // MLA_GROUPS
    problems = []
    for g in range(MLA_GROUPS):
        qs = jnp.concatenate([q_ref[0, :, h * MLA_QCAT:(h + 1) * MLA_QCAT] for h in range(g * per, (g + 1) * per)],
                             axis=0)
        problems.append((qs, get_k, get_vt, scratch[5 * g:5 * g + 5]))
    _chunk_causal_flash(problems, q0, tq, tk)
    for g, (_, _, _, (_, _, _, l_ref, acc_ref)) in enumerate(problems):
        out = acc_ref[...] / l_ref[...]
        for hh in range(per):
            h = g * per + hh
            o_ref[0, :, h * KV_LORA:(h + 1) * KV_LORA] = out[:, hh * tq:(hh + 1) * tq].T.astype(o_ref.dtype)


def _key_major(x, tk):
    b, s, w = x.shape
    return x.reshape(b, s // tk, tk, w).transpose(0, 1, 3, 2)


def _mla_prompt(qcat, kcat, tq, tk):
    b, s, _ = qcat.shape
    tq, tk = min(tq, s), min(tk, s)
    m_rows = MLA_HEADS * tq
    vt = _key_major(kcat[:, :, :KV_LORA], tk)
    return pl.pallas_call(
        functools.partial(_mla_prompt_kernel, tq=tq, tk=tk),
        grid=(b, s // tq),
        in_specs=[pl.BlockSpec((1, tq, MLA_HEADS * MLA_QCAT), lambda bi, i: (bi, i, 0)),
                  pl.BlockSpec((1, s, MLA_QCAT), lambda bi, i: (bi, 0, 0)),
                  pl.BlockSpec((1, s // tk, KV_LORA, tk), lambda bi, i: (bi, 0, 0, 0))],
        out_specs=pl.BlockSpec((1, tq, MLA_HEADS * KV_LORA), lambda bi, i: (bi, i, 0)),
        out_shape=jax.ShapeDtypeStruct((b, s, MLA_HEADS * KV_LORA), BF16),
        scratch_shapes=_flash_scratch(MLA_GROUPS, tk, m_rows // MLA_GROUPS, KV_LORA),
        compiler_params=_params(("parallel", "parallel")),
        name="mla_prompt",
    )(qcat, kcat, vt)


def _lane_halves(x):
    lane = lax.broadcasted_iota(jnp.int32, x.shape, 1)
    zero = jnp.zeros_like(x)
    return jnp.concatenate([jnp.where(lane < LANES // 2, x, zero), jnp.where(lane < LANES // 2, zero, x)], axis=0)


def _diff_lambda(lq_ref, lk_ref, lam_init):
    e = jnp.exp(jnp.sum(lq_ref[...] * lk_ref[...], axis=-1, keepdims=True))
    return e[0:1] - e[1:2] + lam_init


def _diff_prompt_kernel(q_ref, k_ref, vt_ref, lq_ref, lk_ref, o_ref, *scratch, tq, tk, lam_init):
    q0 = pl.program_id(1) * tq
    lam = _diff_lambda(lq_ref, lk_ref, lam_init)
    problems = []
    for h in range(DIFF_HEADS):
        sl = slice(h * LANES, (h + 1) * LANES)
        problems.append((_lane_halves(q_ref[0, :, sl]),
                         lambda k0, sl=sl: k_ref[0, pl.ds(k0, tk), sl],
                         lambda j, sl=sl: vt_ref[0, j, sl, :],
                         scratch[5 * h:5 * h + 5]))
    _chunk_causal_flash(problems, q0, tq, tk)
    for h, (_, _, _, (_, _, _, l_ref, acc_ref)) in enumerate(problems):
        o = acc_ref[...] / l_ref[...]
        o_ref[0, :, h * LANES:(h + 1) * LANES] = (o[:, :tq] - lam * o[:, tq:]).T


def _diff_prompt(dq, dk, dv, lq, lk, lam_init, tq, tk):
    b, s, _ = dq.shape
    tq, tk = min(tq, s), min(tk, s)
    qspec = pl.BlockSpec((1, tq, DIFF_W), lambda bi, i: (bi, i, 0))
    return pl.pallas_call(
        functools.partial(_diff_prompt_kernel, tq=tq, tk=tk, lam_init=lam_init),
        grid=(b, s // tq),
        in_specs=[qspec, pl.BlockSpec((1, s, DIFF_W), lambda bi, i: (bi, 0, 0)),
                  pl.BlockSpec((1, s // tk, DIFF_W, tk), lambda bi, i: (bi, 0, 0, 0)),
                  _const_spec(lq.shape), _const_spec(lk.shape)],
        out_specs=qspec,
        out_shape=jax.ShapeDtypeStruct((b, s, DIFF_W), F32),
        scratch_shapes=_flash_scratch(DIFF_HEADS, tk, 2 * tq, LANES),
        compiler_params=_params(("parallel", "parallel")),
        name="diff_prompt",
    )(dq, dk, _key_major(dv, tk), lq, lk)


def _pad_rows(x, rows):
    return jnp.concatenate([x, jnp.zeros((rows - x.shape[0], x.shape[1]), x.dtype)], axis=0)


def _decode_softmax(s_c, s_n, v_cache, v_new, n_new):
    col = lax.broadcasted_iota(jnp.int32, s_n.shape, 1)
    s_n = jnp.where(col < n_new, s_n, NEG_INF)
    m = jnp.maximum(jnp.max(s_c, axis=-1, keepdims=True), jnp.max(s_n, axis=-1, keepdims=True))
    p_c = jnp.exp2(s_c - m)
    p_n = jnp.exp2(s_n - m)
    l = jnp.sum(p_c, axis=-1, keepdims=True) + jnp.sum(p_n, axis=-1, keepdims=True)
    acc = (jnp.dot(p_c.astype(BF16), v_cache, preferred_element_type=F32)
           + jnp.dot(p_n.astype(BF16), v_new, preferred_element_type=F32))
    return acc / l


def _mla_decode_kernel(q_ref, ckv_ref, kr_ref, kn_ref, o_ref, *, n_new):
    qs = jnp.concatenate([q_ref[0, :, h * MLA_QCAT:(h + 1) * MLA_QCAT] for h in range(MLA_HEADS)], axis=0)
    ckv = ckv_ref[0].astype(BF16)
    kr = kr_ref[0].astype(BF16)
    k_new = _pad_rows(kn_ref[0], LANES)
    s_c = (lax.dot_general(qs[:, :KV_LORA], ckv, _NT, preferred_element_type=F32)
           + lax.dot_general(qs[:, KV_LORA:KV_LORA + MLA_ROPE_DIM], kr, _NT, preferred_element_type=F32))
    s_n = lax.dot_general(qs, k_new, _NT, preferred_element_type=F32)
    out = _decode_softmax(s_c, s_n, ckv, k_new[:, :KV_LORA], n_new)
    for h in range(MLA_HEADS):
        o_ref[0, :, h * KV_LORA:(h + 1) * KV_LORA] = out[h * n_new:(h + 1) * n_new].astype(o_ref.dtype)


def _mla_decode(qcat, ckv_c, kr_c, kcat_new):
    b, n, _ = qcat.shape
    past = ckv_c.shape[1]
    return pl.pallas_call(
        functools.partial(_mla_decode_kernel, n_new=n),
        grid=(b,),
        in_specs=[pl.BlockSpec((1, n, MLA_HEADS * MLA_QCAT), lambda i: (i, 0, 0)),
                  pl.BlockSpec((1, past, KV_LORA), lambda i: (i, 0, 0)),
                  pl.BlockSpec((1, past, MLA_ROPE_DIM), lambda i: (i, 0, 0)),
                  pl.BlockSpec((1, n, MLA_QCAT), lambda i: (i, 0, 0))],
        out_specs=pl.BlockSpec((1, n, MLA_HEADS * KV_LORA), lambda i: (i, 0, 0)),
        out_shape=jax.ShapeDtypeStruct((b, n, MLA_HEADS * KV_LORA), BF16),
        compiler_params=_params(("parallel",)),
        name="mla_decode",
    )(qcat, ckv_c, kr_c, kcat_new)


def _diff_decode_kernel(q_ref, kc_ref, vc_ref, kn_ref, vn_ref, lq_ref, lk_ref, o_ref, *, n_new, past, lam_init):
    lam = _diff_lambda(lq_ref, lk_ref, lam_init)
    for h in range(DIFF_HEADS):
        sl = slice(h * LANES, (h + 1) * LANES)
        q = q_ref[0, :, sl]
        qs = _lane_halves(q)
        v = vc_ref[0, pl.ds(h, past, stride=DIFF_HEADS), :].astype(BF16)
        s_c = jnp.concatenate(
            [jnp.dot(q[:, :DIFF_HD], kc_ref[0, h, 0].astype(BF16), preferred_element_type=F32),
             jnp.dot(q[:, DIFF_HD:], kc_ref[0, h, 1].astype(BF16), preferred_element_type=F32)], axis=0)
        s_n = lax.dot_general(qs, _pad_rows(kn_ref[0, :, sl], LANES), _NT, preferred_element_type=F32)
        o = _decode_softmax(s_c, s_n, v, _pad_rows(vn_ref[0, :, sl], LANES), n_new)
        o_ref[0, :, sl] = o[:n_new] - lam * o[n_new:]


def _diff_decode(dq, kc, vc, dk_new, dv_new, lq, lk, lam_init):
    b, n, _ = dq.shape
    past = kc.shape[-1]
    new = pl.BlockSpec((1, n, DIFF_W), lambda i: (i, 0, 0))
    return pl.pallas_call(
        functools.partial(_diff_decode_kernel, n_new=n, past=past, lam_init=lam_init),
        grid=(b,),
        in_specs=[new, pl.BlockSpec((1, DIFF_HEADS, 2, DIFF_HD, past), lambda i: (i, 0, 0, 0, 0)),
                  pl.BlockSpec((1, past * DIFF_HEADS, 2 * DIFF_HD), lambda i: (i, 0, 0)),
                  new, new, _const_spec(lq.shape), _const_spec(lk.shape)],
        out_specs=new,
        out_shape=jax.ShapeDtypeStruct((b, n, DIFF_W), F32),
        compiler_params=_params(("parallel",)),
        name="diff_decode",
    )(dq, kc, vc.reshape(b, past * DIFF_HEADS, 2 * DIFF_HD), dk_new, dv_new, lq, lk)


def _even_out_kernel(olat_ref, odiff_ref, gate_ref, x_ref, wuv_ref, sg_ref, wout_ref, lng_ref, lnb_ref, y_ref,
                     *, lam_init):
    gate = gate_ref[...]
    o_mla = jnp.dot(olat_ref[...], wuv_ref[...], preferred_element_type=F32)
    parts = [(o_mla * gate[:, :MLA_W].astype(F32)).astype(BF16)]
    od = odiff_ref[...]
    for h in range(DIFF_HEADS):
        sl = slice(h * LANES, (h + 1) * LANES)
        n = _rms(od[:, sl], sg_ref[...]) * (1.0 - lam_init)
        parts.append((n * gate[:, MLA_W + h * LANES:MLA_W + (h + 1) * LANES].astype(F32)).astype(BF16))
    o = jnp.dot(jnp.concatenate(parts, axis=1), wout_ref[...], preferred_element_type=F32)
    y_ref[...] = _layernorm(ALPHA * x_ref[...] + o, lng_ref[...], lnb_ref[...])


def _even_out(olat, odiff, gate, x2d, wuv_bd, sg, wout, lng, lnb, lam_init, tm):
    t = x2d.shape[0]
    tm = min(tm, t)
    row = lambda w: pl.BlockSpec((tm, w), lambda i: (i, 0))
    return pl.pallas_call(
        functools.partial(_even_out_kernel, lam_init=lam_init),
        grid=(t // tm,),
        in_specs=[row(MLA_HEADS * KV_LORA), row(DIFF_W), row(MLA_W + DIFF_W), row(D_MODEL),
                  _const_spec(wuv_bd.shape), _const_spec(sg.shape), _const_spec(wout.shape),
                  _const_spec(lng.shape), _const_spec(lnb.shape)],
        out_specs=row(D_MODEL),
        out_shape=jax.ShapeDtypeStruct((t, D_MODEL), F32),
        compiler_params=_params(("parallel",)),
        name="even_out",
    )(olat, odiff, gate, x2d, wuv_bd, sg, wout, lng, lnb)


def _odd_in_kernel(x_ref, w_ref, q_ref, kf_ref, kb_ref, vf_ref, vb_ref, g_ref):
    xb = x_ref[...].astype(BF16)

    def proj(i):
        return jnp.dot(xb, w_ref[:, i * SB_W:(i + 1) * SB_W], preferred_element_type=F32)

    tm = x_ref.shape[0]

    def store_heads(ref, val):
        for h in range(SB_HEADS):
            ref[pl.ds(h, tm, stride=SB_HEADS), :] = val[:, h * SB_HD:(h + 1) * SB_HD]

    q_ref[...] = (proj(0) * (SB_HD ** -0.5 * LOG2E)).astype(BF16)
    k = proj(1)
    store_heads(kf_ref, k)
    kb_ref[...] = k.astype(BF16)
    v = proj(2)
    store_heads(vf_ref, v)
    vb_ref[...] = v.astype(BF16)
    g_ref[...] = _silu(proj(3)).astype(BF16)


def _odd_in(x2d, w_in, tm):
    t = x2d.shape[0]
    tm = min(tm, t)
    row = pl.BlockSpec((tm, SB_W), lambda i: (i, 0))
    heads = pl.BlockSpec((tm * SB_HEADS, SB_HD), lambda i: (i, 0))
    wide = lambda d: jax.ShapeDtypeStruct((t, SB_W), d)
    tall = jax.ShapeDtypeStruct((t * SB_HEADS, SB_HD), F32)
    return pl.pallas_call(
        _odd_in_kernel,
        grid=(t // tm,),
        in_specs=[pl.BlockSpec((tm, D_MODEL), lambda i: (i, 0)), _const_spec(w_in.shape)],
        out_specs=[row, heads, row, heads, row, row],
        out_shape=[wide(BF16), tall, wide(BF16), tall, wide(BF16), wide(BF16)],
        compiler_params=_params(("parallel",)),
        name="odd_in",
    )(x2d, w_in)


SB_EXIT_LOG2 = 160.0


def _log2(x):
    return jnp.log2(x)


def _later_matrix(n):
    return (lax.broadcasted_iota(jnp.int32, (n, n), 0) > lax.broadcasted_iota(jnp.int32, (n, n), 1)).astype(BF16)


def _softplus2(z):
    return jnp.maximum(z, 0.0) + _log2(1.0 + jnp.exp2(-jnp.abs(z)))


def _sb_prompt_kernel(q_ref, k_ref, vt_ref, o_ref, c_ref, acc_ref, x_ref, sp0_ref, *, tq, pairs):
    i = pl.program_id(2)
    ut = (lax.broadcasted_iota(jnp.int32, (tq, tq), 1) > lax.broadcasted_iota(jnp.int32, (tq, tq), 0)).astype(BF16)
    keys = lax.broadcasted_iota(jnp.int32, (tq, 2 * tq), 0)
    queries = lax.broadcasted_iota(jnp.int32, (tq, 2 * tq), 1) & (tq - 1)
    diag_valid = keys < queries
    half = LANES // 2
    sls = [slice(p * LANES, (p + 1) * LANES) for p in range(pairs)]
    qss = [_lane_halves(q_ref[0, :, sl]) for sl in sls]

    def all_pairs(j, valid, first):
        k0 = pl.multiple_of(j * tq, tq)
        zs = [lax.dot_general(k_ref[0, pl.ds(k0, tq), sl], qss[p], _NT, preferred_element_type=F32)
              for p, sl in enumerate(sls)]
        laters = []
        for p, z in enumerate(zs):
            sp = _softplus2(z)
            x_ref[p] = z - sp
            if valid is not None:
                sp = jnp.where(valid, sp, 0.0)
            spb = sp.astype(BF16)
            sp0_ref[p] = spb[:1, :].astype(F32)
            laters.append(jnp.dot(ut, spb, preferred_element_type=F32))
        cmin = None
        for p, sl in enumerate(sls):
            a = jnp.exp2(x_ref[p] - laters[p])
            if valid is not None:
                a = jnp.where(valid, a, 0.0)
            pv = lax.dot_general(vt_ref[0, pl.ds(k0, tq), sl], a.astype(BF16), (((0,), (0,)), ((), ())),
                                 preferred_element_type=F32)
            tot = laters[p][:1, :] + sp0_ref[p]
            if first:
                acc_ref[p] = pv
                c = tot
            else:
                c0 = c_ref[p]
                acc_ref[p] += pv * jnp.exp2(-c0)
                c = c0 + tot
            c_ref[p] = c
            cmin = c if cmin is None else jnp.minimum(cmin, c)
        return jnp.min(cmin)

    cmin = all_pairs(i, diag_valid, True)
    lax.while_loop(lambda jc: (jc[0] >= 0) & (jc[1] < SB_EXIT_LOG2),
                   lambda jc: (jc[0] - 1, all_pairs(jc[0], None, False)),
                   (i - 1, cmin))

    for p, sl in enumerate(sls):
        acc = acc_ref[p]
        o_t = jnp.concatenate([acc[:half, :tq], acc[half:, tq:]], axis=0)
        o_ref[0, :, sl] = o_t.T.astype(o_ref.dtype)


def _sb_prompt(q, k, v, tq, pairs):
    b, s, w = q.shape
    tq = min(tq, s)
    gw = pairs * LANES
    qspec = pl.BlockSpec((1, tq, gw), lambda bi, g, i: (bi, i, g))
    return pl.pallas_call(
        functools.partial(_sb_prompt_kernel, tq=tq, pairs=pairs),
        grid=(b, w // gw, s // tq),
        in_specs=[qspec, pl.BlockSpec((1, s, gw), lambda bi, g, i: (bi, 0, g)),
                  pl.BlockSpec((1, s, gw), lambda bi, g, i: (bi, 0, g))],
        out_specs=qspec,
        out_shape=jax.ShapeDtypeStruct((b, s, w), BF16),
        scratch_shapes=[pltpu.VMEM((pairs, 1, 2 * tq), F32), pltpu.VMEM((pairs, LANES, 2 * tq), F32),
                        pltpu.VMEM((pairs, tq, 2 * tq), F32), pltpu.VMEM((pairs, 1, 2 * tq), F32)],
        compiler_params=_params(("parallel", "parallel", "parallel")),
        name="sb_prompt",
    )(q, k, v)


def _sb_rows_block(score_fns, value_fns, u, cs, valid):
    zs = [f() for f in score_fns]
    xs, sp0s, laters = [], [], []
    for z in zs:
        sp = _softplus2(z)
        xs.append(z - sp)
        if valid is not None:
            sp = jnp.where(valid, sp, 0.0)
        spb = sp.astype(BF16)
        sp0s.append(spb[:, :1].astype(F32))
        laters.append(jnp.dot(spb, u, preferred_element_type=F32))
    out = []
    for x, sp0, later, c, value in zip(xs, sp0s, laters, cs, value_fns):
        a = jnp.exp2(x - later - c)
        if valid is not None:
            a = jnp.where(valid, a, 0.0)
        out.append((value(a.astype(BF16)), c + later[:, :1] + sp0))
    return out


def _pair_fns_new(qs, kb, vb):
    return (lambda: lax.dot_general(qs, kb, _NT, preferred_element_type=F32),
            lambda a: jnp.dot(a, vb, preferred_element_type=F32))


def _pair_fns_cache(qs, kt_ref, vt_ref, h0, t0, tk, n_new):
    hd = LANES // 2
    q0, q1 = qs[:n_new, :hd], qs[n_new:, hd:]
    pad = jnp.zeros((n_new, hd), F32)

    def score():
        return jnp.concatenate(
            [jnp.dot(q0, kt_ref[0, h0, :, t0:t0 + tk].astype(BF16), preferred_element_type=F32),
             jnp.dot(q1, kt_ref[0, h0 + 1, :, t0:t0 + tk].astype(BF16), preferred_element_type=F32)], axis=0)

    def value(a):
        pv0 = lax.dot_general(a[:n_new], vt_ref[0, h0, :, t0:t0 + tk].astype(BF16), _NT, preferred_element_type=F32)
        pv1 = lax.dot_general(a[n_new:], vt_ref[0, h0 + 1, :, t0:t0 + tk].astype(BF16), _NT,
                              preferred_element_type=F32)
        return jnp.concatenate([jnp.concatenate([pv0, pad], axis=1), jnp.concatenate([pad, pv1], axis=1)], axis=0)

    return score, value


def _sb_decode_recent_kernel(q_ref, kw_ref, vw_ref, kn_ref, vn_ref, acc_ref, c_ref, *, n_new, tk, pairs):
    sls = [slice(p * LANES, (p + 1) * LANES) for p in range(pairs)]
    qss = [_lane_halves(q_ref[0, :, sl]) for sl in sls]
    rows = lax.broadcasted_iota(jnp.int32, (2 * n_new, LANES), 0) & (n_new - 1)
    cols = lax.broadcasted_iota(jnp.int32, (2 * n_new, LANES), 1)
    zero = jnp.zeros((2 * n_new, 1), F32)
    fns = [_pair_fns_new(qs, _pad_rows(kn_ref[0, :, sl], LANES), _pad_rows(vn_ref[0, :, sl], LANES))
           for qs, sl in zip(qss, sls)]
    state = _sb_rows_block([f for f, _ in fns], [g for _, g in fns], _later_matrix(LANES), [zero] * pairs,
                           cols < rows)
    u = _later_matrix(tk)
    for j in reversed(range(kw_ref.shape[3] // tk)):
        fns = [_pair_fns_cache(qs, kw_ref, vw_ref, 2 * p, j * tk, tk, n_new) for p, qs in enumerate(qss)]
        blk = _sb_rows_block([f for f, _ in fns], [g for _, g in fns], u, [c for _, c in state], None)
        state = [(acc + pv, c) for (acc, _), (pv, c) in zip(state, blk)]
    for p, (acc, c) in enumerate(state):
        acc_ref[0, p] = acc
        c_ref[0, p] = jnp.broadcast_to(c, (2 * n_new, LANES))


def _sb_decode_rest_kernel(q_ref, kc_ref, vc_ref, acc_in_ref, c_in_ref, acc_ref, *, n_new, tk):
    qs = _lane_halves(q_ref[0])
    acc = acc_in_ref[0, 0]
    c = c_in_ref[0, 0][:, :1]
    u = _later_matrix(tk)
    for j in reversed(range(kc_ref.shape[3] // tk)):
        def visit(acc, c, j=j):
            score, value = _pair_fns_cache(qs, kc_ref, vc_ref, 0, j * tk, tk, n_new)
            (pv, c), = _sb_rows_block([score], [value], u, [c], None)
            return acc + pv, c

        acc, c = lax.cond(jnp.min(c) < SB_EXIT_LOG2, visit, lambda acc, c: (acc, c), acc, c)
    acc_ref[0, 0] = acc


def _sb_decode_finish_kernel(acc_ref, o_ref, *, n_new, pairs):
    lane = lax.broadcasted_iota(jnp.int32, (n_new, LANES), 1)
    for p in range(pairs):
        acc = acc_ref[0, p]
        o_ref[0, :, p * LANES:(p + 1) * LANES] = jnp.where(lane < LANES // 2, acc[:n_new], acc[n_new:]).astype(o_ref.dtype)


def _sb_decode(q, kc4, vc4, k_new, v_new, tk, window):
    b, n, w = q.shape
    past, heads, hd = kc4.shape[1:]
    window = min(window, past)
    if past % window:
        raise NotImplementedError("the decode window must divide the cache length")
    tk = min(tk, window)
    pairs = w // LANES
    rest = past - window
    kt, vt = jnp.transpose(kc4, (0, 2, 3, 1)), jnp.transpose(vc4, (0, 2, 3, 1))
    state_shape = jax.ShapeDtypeStruct((b, pairs, 2 * n, LANES), F32)
    new = pl.BlockSpec((1, n, w), lambda i: (i, 0, 0))
    win = pl.BlockSpec((1, heads, hd, window), lambda i: (i, 0, 0, past // window - 1))
    state = pl.BlockSpec((1, pairs, 2 * n, LANES), lambda i: (i, 0, 0, 0))
    acc, c = pl.pallas_call(
        functools.partial(_sb_decode_recent_kernel, n_new=n, tk=tk, pairs=pairs),
        grid=(b,),
        in_specs=[new, win, win, new, new],
        out_specs=[state, state],
        out_shape=[state_shape, state_shape],
        compiler_params=_params(("parallel",)),
        name="sb_decode_recent",
    )(q, kt, vt, k_new, v_new)

    if rest:
        def older(acc, c):
            new1 = pl.BlockSpec((1, n, LANES), lambda i, g: (i, 0, g))
            cache = pl.BlockSpec((1, 2, hd, rest), lambda i, g: (i, g, 0, 0))
            state1 = pl.BlockSpec((1, 1, 2 * n, LANES), lambda i, g: (i, g, 0, 0))
            return pl.pallas_call(
                functools.partial(_sb_decode_rest_kernel, n_new=n, tk=math.gcd(tk, rest)),
                grid=(b, pairs),
                in_specs=[new1, cache, cache, state1, state1],
                out_specs=state1,
                out_shape=state_shape,
                compiler_params=_params(("parallel", "parallel")),
                name="sb_decode_rest",
            )(q, kt[..., :rest], vt[..., :rest], acc, c)

        acc = lax.cond(jnp.min(c) < SB_EXIT_LOG2, older, lambda acc, c: acc, acc, c)

    return pl.pallas_call(
        functools.partial(_sb_decode_finish_kernel, n_new=n, pairs=pairs),
        grid=(b,),
        in_specs=[state],
        out_specs=new,
        out_shape=jax.ShapeDtypeStruct((b, n, w), BF16),
        compiler_params=_params(("parallel",)),
        name="sb_decode_finish",
    )(acc)


def _odd_out_kernel(o_ref, g_ref, x_ref, wout_ref, lng_ref, lnb_ref, y_ref):
    a = (o_ref[...].astype(F32) * g_ref[...].astype(F32)).astype(BF16)
    o = jnp.dot(a, wout_ref[...], preferred_element_type=F32)
    y_ref[...] = _layernorm(ALPHA * x_ref[...] + o, lng_ref[...], lnb_ref[...])


def _odd_out(o, gate, x2d, wout, lng, lnb, tm):
    t = x2d.shape[0]
    tm = min(tm, t)
    row = lambda w: pl.BlockSpec((tm, w), lambda i: (i, 0))
    return pl.pallas_call(
        _odd_out_kernel,
        grid=(t // tm,),
        in_specs=[row(SB_W), row(SB_W), row(D_MODEL), _const_spec(wout.shape),
                  _const_spec(lng.shape), _const_spec(lnb.shape)],
        out_specs=row(D_MODEL),
        out_shape=jax.ShapeDtypeStruct((t, D_MODEL), F32),
        compiler_params=_params(("parallel",)),
        name="odd_out",
    )(o, gate, x2d, wout, lng, lnb)


def _even_weights(w_in, q_g, w_uq, kv_g, w_uk):
    pad = jnp.zeros((D_MODEL, LANES - MLA_ROPE_DIM), w_in.dtype)
    o = np.cumsum([0, Q_LORA, KV_LORA, MLA_ROPE_DIM, MLA_W, DIFF_W, DIFF_W, DIFF_W, DIFF_W])
    w_re = jnp.concatenate([w_in[:, o[0]:o[3]], pad, w_in[:, o[3]:]], axis=1).astype(BF16)
    w_uq3 = w_uq.reshape(Q_LORA, MLA_HEADS, MLA_QK_DIM)
    wqn = w_uq3[:, :, :MLA_NOPE_DIM].reshape(Q_LORA, MLA_HEADS * MLA_NOPE_DIM).astype(BF16)
    wqr = jnp.pad(w_uq3[:, :, MLA_NOPE_DIM:], ((0, 0), (0, 0), (0, LANES - MLA_ROPE_DIM)))
    wqr = wqr.reshape(Q_LORA, MLA_HEADS * LANES).astype(BF16)
    eye = jnp.eye(MLA_HEADS, dtype=w_uk.dtype)
    wuk_bd = jnp.einsum('chn,hg->hngc', w_uk, eye).reshape(MLA_HEADS * MLA_NOPE_DIM, MLA_HEADS * KV_LORA).astype(BF16)
    return w_re, wqn, wqr, wuk_bd, q_g.reshape(1, Q_LORA), kv_g.reshape(1, KV_LORA)


def _uv_block_diag(w_uv):
    eye = jnp.eye(MLA_HEADS, dtype=w_uv.dtype)
    return jnp.einsum('chv,hg->hcgv', w_uv, eye).reshape(MLA_HEADS * KV_LORA, MLA_HEADS * MLA_V_DIM).astype(BF16)


TM_PROJ = 512
TQ_MLA = 512
TQ_DIFF = 512
TK_SOFTMAX = 512
TQ_SB = 256
SB_PAIRS = 8
TK_SB_DECODE = 256
SB_DECODE_WINDOW = 256


def kernel(x_prompt, x_sample, cache_mla_ckv, cache_mla_krope, cache_diff_k, cache_diff_v, cache_sb_k, cache_sb_v,
           w_in_even, q_norm_g, w_uq, kv_norm_g, w_uk, w_uv, diff_lambda_q, diff_lambda_k, diff_subln_g,
           w_out_even, ln_even_g, ln_even_b, w_in_odd, w_out_odd, ln_odd_g, ln_odd_b):
    bp, sp, _ = x_prompt.shape
    bs, ss, _ = x_sample.shape
    past = cache_sb_k.shape[2]
    if past % CHUNK or ss > CHUNK or (ss & (ss - 1)) or sp % TQ_SB:
        raise NotImplementedError("decode frames must share one chunk after a chunk-aligned cache")
    tp, ts = bp * sp, bs * ss
    pos_p = jnp.arange(sp, dtype=jnp.int32)
    pos_s = past + jnp.arange(ss, dtype=jnp.int32)
    pos_s_rows = jnp.tile(pos_s, bs)

    def tables(pos):
        mc, msa, msb, _ = _rope_tables(pos, MLA_ROPE_DIM, MLA_ROPE_THETA, LANES)
        dc, dsa, dsb, _ = _rope_tables(pos, DIFF_ROT, ROPE_THETA, DIFF_HD)
        return mc, msa, msb, dc, dsa, dsb

    yp = x_prompt.reshape(tp, D_MODEL)
    ys = x_sample.reshape(ts, D_MODEL)
    even_p, even_s, odd_p, odd_s = [], [], [], []
    for layer in range(DEPTH):
        i = layer // 2
        if layer % 2 == 0:
            lam_init = 0.8 - 0.6 * math.exp(-0.3 * layer)
            wts = _even_weights(w_in_even[i], q_norm_g[i], w_uq[i], kv_norm_g[i], w_uk[i])
            wuv_bd = _uv_block_diag(w_uv[i])
            sg = diff_subln_g[i].reshape(1, 2 * DIFF_HD)
            wout = w_out_even[i].astype(BF16)
            lng, lnb = ln_even_g[i].reshape(1, D_MODEL), ln_even_b[i].reshape(1, D_MODEL)
            lq, lk = diff_lambda_q[i], diff_lambda_k[i]
            qcat, kcat, ckv, krope, gate, dq, dkf, dkb, dvf, dvb = _even_in(yp, wts, tables(pos_p), sp, TM_PROJ)
            olat = _mla_prompt(qcat.reshape(bp, sp, -1), kcat.reshape(bp, sp, -1), TQ_MLA, TK_SOFTMAX)
            odiff = _diff_prompt(dq.reshape(bp, sp, -1), dkb.reshape(bp, sp, -1), dvb.reshape(bp, sp, -1),
                                 lq, lk, lam_init, TQ_DIFF, TK_SOFTMAX)
            yp_new = _even_out(olat.reshape(tp, -1), odiff.reshape(tp, -1), gate, yp, wuv_bd, sg, wout, lng, lnb,
                               lam_init, TM_PROJ)
            even_p.append((ckv.reshape(bp, sp, KV_LORA), krope.reshape(bp, sp, MLA_ROPE_DIM),
                           dkf.reshape(bp, sp, DIFF_HEADS, 2, DIFF_HD), dvf.reshape(bp, sp, DIFF_HEADS, 2 * DIFF_HD)))
            qcat, kcat, ckv, krope, gate, dq, dkf, dkb, dvf, dvb = _even_in(ys, wts, tables(pos_s_rows), ts, TM_PROJ)
            olat = _mla_decode(qcat.reshape(bs, ss, -1), cache_mla_ckv[i], cache_mla_krope[i],
                               kcat.reshape(bs, ss, -1))
            odiff = _diff_decode(dq.reshape(bs, ss, -1), jnp.transpose(cache_diff_k[i], (0, 2, 3, 4, 1)),
                                 cache_diff_v[i],
                                 dkb.reshape(bs, ss, -1),
                                 dvb.reshape(bs, ss, -1), lq, lk, lam_init)
            ys_new = _even_out(olat.reshape(ts, -1), odiff.reshape(ts, -1), gate, ys, wuv_bd, sg, wout, lng, lnb,
                               lam_init, TM_PROJ)
            even_s.append((ckv.reshape(bs, ss, KV_LORA), krope.reshape(bs, ss, MLA_ROPE_DIM),
                           dkf.reshape(bs, ss, DIFF_HEADS, 2, DIFF_HD), dvf.reshape(bs, ss, DIFF_HEADS, 2 * DIFF_HD)))
            yp, ys = yp_new, ys_new
        else:
            w_in = w_in_odd[i].astype(BF16)
            wout = w_out_odd[i].astype(BF16)
            lng, lnb = ln_odd_g[i].reshape(1, D_MODEL), ln_odd_b[i].reshape(1, D_MODEL)
            q, kf, kb, vf, vb, gate = _odd_in(yp, w_in, TM_PROJ)
            o = _sb_prompt(q.reshape(bp, sp, -1), kb.reshape(bp, sp, -1), vb.reshape(bp, sp, -1), TQ_SB, SB_PAIRS)
            yp_new = _odd_out(o.reshape(tp, -1), gate, yp, wout, lng, lnb, TM_PROJ)
            odd_p.append((kf.reshape(bp, sp, SB_HEADS, SB_HD), vf.reshape(bp, sp, SB_HEADS, SB_HD)))
            q, kf, kb, vf, vb, gate = _odd_in(ys, w_in, TM_PROJ)
            o = _sb_decode(q.reshape(bs, ss, -1), cache_sb_k[i], cache_sb_v[i], kb.reshape(bs, ss, -1),
                           vb.reshape(bs, ss, -1), TK_SB_DECODE, SB_DECODE_WINDOW)
            ys_new = _odd_out(o.reshape(ts, -1), gate, ys, wout, lng, lnb, TM_PROJ)
            odd_s.append((kf.reshape(bs, ss, SB_HEADS, SB_HD), vf.reshape(bs, ss, SB_HEADS, SB_HD)))
            yp, ys = yp_new, ys_new
    stack = lambda lst, k: jnp.stack([t[k] for t in lst])
    return (yp.reshape(bp, sp, D_MODEL), ys.reshape(bs, ss, D_MODEL),
            stack(even_p, 0), stack(even_p, 1), stack(even_p, 2), stack(even_p, 3),
            stack(odd_p, 0), stack(odd_p, 1),
            stack(even_s, 0), stack(even_s, 1), stack(even_s, 2), stack(even_s, 3),
            stack(odd_s, 0), stack(odd_s, 1))
```
